```python
import math
import jax, jax.numpy as jnp
from jax import lax
import numpy as np

D_MODEL = 2048
BATCH = 4
SEQ = 4096
DEPTH = 1
DEC_BATCH = 32
DEC_SEQ = 64
PAST_LEN = 1024

CHUNK = 64
D_SSM = D_MODEL // 2
GROUP_CH = 16
N_GROUPS = D_SSM // GROUP_CH
STATE = 64
HEAD_DIM = 64
N_HEADS = (D_MODEL // 2) // HEAD_DIM
N_KV = 4
Q_PER_KV = N_HEADS // N_KV
D_ATTN = N_HEADS * HEAD_DIM
D_KV = N_KV * HEAD_DIM
IN_COLS = D_SSM + D_ATTN + 2 * D_KV
WINDOW = 128
BAND = WINDOW // CHUNK
ROT_DIM = HEAD_DIM // 4
ROPE_THETA = 500000.0
D_FF = ((8 * D_MODEL // 3 + 255) // 256) * 256
EPS = 1e-6
NEG = -1e30

kernel_name = 'chunk_causal_s5_swa_sink_hybrid'


def rmsnorm(x, g):
    xf = x.astype(jnp.float32)
    return xf * lax.rsqrt(jnp.mean(xf * xf, axis=-1, keepdims=True) + EPS) * g.astype(jnp.float32)


def rope_partial(x, pos):
    half = ROT_DIM // 2
    inv = ROPE_THETA ** (-jnp.arange(half, dtype=jnp.float32) * 2.0 / ROT_DIM)
    ang = pos.astype(jnp.float32)[:, None] * inv[None, :]
    cos = jnp.cos(ang)[:, None, :]
    sin = jnp.sin(ang)[:, None, :]
    x1 = x[..., :half]
    x2 = x[..., half:ROT_DIM]
    return jnp.concatenate([x1 * cos - x2 * sin, x2 * cos + x1 * sin, x[..., ROT_DIM:]], axis=-1)


def ssm_discretise(a_re, a_im, log_dt, b_re, b_im):
    a_re = a_re.astype(jnp.float32)
    a_im = a_im.astype(jnp.float32)
    dt = jnp.exp(log_dt.astype(jnp.float32))[:, None]
    mag = jnp.exp(a_re * dt)
    abar_re = mag * jnp.cos(a_im * dt)
    abar_im = mag * jnp.sin(a_im * dt)
    fr, fi = abar_re - 1.0, abar_im
    den = a_re * a_re + a_im * a_im
    cr = (fr * a_re + fi * a_im) / den
    ci = (fi * a_re - fr * a_im) / den
    b_re = b_re.astype(jnp.float32)
    b_im = b_im.astype(jnp.float32)
    bbar_re = cr[..., None] * b_re - ci[..., None] * b_im
    bbar_im = cr[..., None] * b_im + ci[..., None] * b_re
    return abar_re, abar_im, bbar_re, bbar_im


def _complex_affine_combine(e1, e2):
    a1r, a1i, b1r, b1i = e1
    a2r, a2i, b2r, b2i = e2
    return (a2r * a1r - a2i * a1i,
            a2r * a1i + a2i * a1r,
            a2r * b1r - a2i * b1i + b2r,
            a2r * b1i + a2i * b1r + b2i)


def ssm_mixer(u, h0r, h0i, a_re, a_im, log_dt, b_re, b_im, c_re, c_im, d_skip):
    B, L, _ = u.shape
    blk = min(L, CHUNK)
    nb = L // blk
    abr, abi, bbr, bbi = ssm_discretise(a_re, a_im, log_dt, b_re, b_im)
    c_re = c_re.astype(jnp.float32)
    c_im = c_im.astype(jnp.float32)
    a_r = jnp.broadcast_to(abr, (B, blk, N_GROUPS, STATE))
    a_i = jnp.broadcast_to(abi, (B, blk, N_GROUPS, STATE))
    ug = u.reshape(B, nb, blk, N_GROUPS, GROUP_CH).transpose(1, 0, 2, 3, 4)

    def body(carry, ub):
        hr0, hi0 = carry
        br = jnp.einsum('blgc,gnc->blgn', ub, bbr)
        bi = jnp.einsum('blgc,gnc->blgn', ub, bbi)
        ar_c, ai_c, br_c, bi_c = lax.associative_scan(_complex_affine_combine, (a_r, a_i, br, bi), axis=1)
        hr = br_c + ar_c * hr0[:, None] - ai_c * hi0[:, None]
        hi = bi_c + ar_c * hi0[:, None] + ai_c * hr0[:, None]
        y = jnp.einsum('blgn,gcn->blgc', hr, c_re) - jnp.einsum('blgn,gcn->blgc', hi, c_im)
        return (hr[:, -1], hi[:, -1]), y

    (hr, hi), ys = lax.scan(body, (h0r.astype(jnp.float32), h0i.astype(jnp.float32)), ug)
    y = ys.transpose(1, 0, 2, 3, 4).reshape(B, L, D_SSM) + d_skip.astype(jnp.float32) * u
    return y, hr, hi


def sink_attention(q, k, v, sinks, mask):
    scale = HEAD_DIM ** -0.5
    s = jnp.einsum('...qkgd,...skd->...kgqs', q, k.astype(jnp.float32)) * scale
    s = jnp.where(mask, s, NEG)
    sk = sinks.astype(jnp.float32).reshape(N_KV, Q_PER_KV)[:, :, None, None]
    m = jnp.maximum(jnp.max(s, axis=-1, keepdims=True), sk)
    p = jnp.exp(s - m)
    den = jnp.sum(p, axis=-1, keepdims=True) + jnp.exp(sk - m)
    return jnp.einsum('...kgqs,...skd->...qkgd', p / den, v.astype(jnp.float32))


def attn_prompt(q, k, v, sinks):
    B, L = q.shape[0], q.shape[1]
    nc = L // CHUNK
    S = (BAND + 1) * CHUNK
    qc = q.reshape(B, nc, CHUNK, N_KV, Q_PER_KV, HEAD_DIM)
    pad = ((0, 0), (BAND, 0), (0, 0), (0, 0), (0, 0))
    kp = jnp.pad(k.reshape(B, nc, CHUNK, N_KV, HEAD_DIM), pad)
    vp = jnp.pad(v.reshape(B, nc, CHUNK, N_KV, HEAD_DIM), pad)
    kb = jnp.concatenate([kp[:, i:i + nc] for i in range(BAND + 1)], axis=2)
    vb = jnp.concatenate([vp[:, i:i + nc] for i in range(BAND + 1)], axis=2)
    src_chunk = jnp.arange(nc)[:, None] + (jnp.arange(S) // CHUNK)[None, :] - BAND
    mask = (src_chunk >= 0)[:, None, None, None, :]
    o = sink_attention(qc, kb, vb, sinks, mask)
    return o.reshape(B, L, D_ATTN)


def trunk_layer(x, pos, k_cache, v_cache, h0r, h0i, p):
    (norm1, w_in, q_norm, k_norm, sinks, a_re, a_im, log_dt, b_re, b_im, c_re, c_im, d_skip,
     w_glu, w_br_ssm, w_br_attn, w_gate, w_out, norm2, w_fg, w_fu, w_fd) = p
    B, L, _ = x.shape
    xf = x.astype(jnp.float32)
    xn = rmsnorm(xf, norm1)
    z = xn @ w_in
    u = z[..., :D_SSM]
    q = z[..., D_SSM:D_SSM + D_ATTN]
    k = z[..., D_SSM + D_ATTN:D_SSM + D_ATTN + D_KV]
    v = z[..., D_SSM + D_ATTN + D_KV:]

    ys, hr, hi = ssm_mixer(u, h0r, h0i, a_re, a_im, log_dt, b_re, b_im, c_re, c_im, d_skip)
    ys = jax.nn.gelu(ys)
    ys = ys * jax.nn.sigmoid(ys @ w_glu)

    q = rope_partial(rmsnorm(q.reshape(B, L, N_HEADS, HEAD_DIM), q_norm), pos)
    k = rope_partial(rmsnorm(k.reshape(B, L, N_KV, HEAD_DIM), k_norm), pos)
    v = v.reshape(B, L, N_KV, HEAD_DIM)
    if k_cache is None:
        o = attn_prompt(q, k, v, sinks)
        win = min(WINDOW, L)
        k_all, v_all = k, v
    else:
        win = k_cache.shape[1]
        k_all = jnp.concatenate([k_cache.astype(jnp.float32), k], axis=1)
        v_all = jnp.concatenate([v_cache.astype(jnp.float32), v], axis=1)
        qs = q.reshape(B, L, N_KV, Q_PER_KV, HEAD_DIM)
        o = sink_attention(qs, k_all, v_all, sinks, True).reshape(B, L, D_ATTN)
    new_k = k_all[:, -win:].astype(x.dtype)
    new_v = v_all[:, -win:].astype(x.dtype)

    g = jax.nn.sigmoid(xn @ w_gate)
    mixed = g[..., :D_MODEL] * (ys @ w_br_ssm) + g[..., D_MODEL:] * (o @ w_br_attn)
    h = xf + mixed @ w_out

    hn = rmsnorm(h, norm2)
    ff = (jax.nn.silu(hn @ w_fg) * (hn @ w_fu)) @ w_fd
    out = (h + ff).astype(x.dtype)
    return out, new_k, new_v, hr.astype(x.dtype), hi.astype(x.dtype)


def setup_inputs(seed: int = 0) -> dict:
    key = jax.random.key(seed)
    ks = iter(jax.random.split(key, 32))
    f32 = jnp.float32

    def nrm(shape, scale):
        return jax.random.normal(next(ks), shape, f32) * scale

    win = min(WINDOW, PAST_LEN)
    x_prompt = nrm((BATCH, SEQ, D_MODEL), 1.0)
    x_sample = nrm((DEC_BATCH, DEC_SEQ, D_MODEL), 1.0)
    cache_k = nrm((DEPTH, DEC_BATCH, win, N_KV, HEAD_DIM), 1.0)
    cache_v = nrm((DEPTH, DEC_BATCH, win, N_KV, HEAD_DIM), 1.0)
    state_ssm_re = nrm((DEPTH, DEC_BATCH, N_GROUPS, STATE), 0.1)
    state_ssm_im = nrm((DEPTH, DEC_BATCH, N_GROUPS, STATE), 0.1)
    norm1 = 1.0 + nrm((DEPTH, D_MODEL), 0.02)
    w_in = nrm((DEPTH, D_MODEL, IN_COLS), D_MODEL ** -0.5)
    q_norm = 1.0 + nrm((DEPTH, HEAD_DIM), 0.02)
    k_norm = 1.0 + nrm((DEPTH, HEAD_DIM), 0.02)
    sinks = nrm((DEPTH, N_HEADS), 0.5)
    n_idx = jnp.arange(STATE, dtype=f32)
    ssm_a_re = -0.5 + nrm((DEPTH, N_GROUPS, STATE), 0.01)
    ssm_a_im = math.pi * n_idx + nrm((DEPTH, N_GROUPS, STATE), 0.01)
    ssm_log_dt = jax.random.uniform(next(ks), (DEPTH, N_GROUPS), f32, math.log(1e-3), math.log(1e-1))
    ssm_b_re = nrm((DEPTH, N_GROUPS, STATE, GROUP_CH), (2 * GROUP_CH) ** -0.5)
    ssm_b_im = nrm((DEPTH, N_GROUPS, STATE, GROUP_CH), (2 * GROUP_CH) ** -0.5)
    ssm_c_re = nrm((DEPTH, N_GROUPS, GROUP_CH, STATE), (2 * STATE) ** -0.5)
    ssm_c_im = nrm((DEPTH, N_GROUPS, GROUP_CH, STATE), (2 * STATE) ** -0.5)
    ssm_d = nrm((DEPTH, D_SSM), 1.0)
    w_glu = nrm((DEPTH, D_SSM, D_SSM), D_SSM ** -0.5)
    w_br_ssm = nrm((DEPTH, D_SSM, D_MODEL), D_SSM ** -0.5)
    w_br_attn = nrm((DEPTH, D_ATTN, D_MODEL), D_ATTN ** -0.5)
    w_gate = nrm((DEPTH, D_MODEL, 2 * D_MODEL), D_MODEL ** -0.5)
    w_out = nrm((DEPTH, D_MODEL, D_MODEL), D_MODEL ** -0.5)
    norm2 = 1.0 + nrm((DEPTH, D_MODEL), 0.02)
    w_ffn_gate = nrm((DEPTH, D_MODEL, D_FF), D_MODEL ** -0.5)
    w_ffn_up = nrm((DEPTH, D_MODEL, D_FF), D_MODEL ** -0.5)
    w_ffn_down = nrm((DEPTH, D_FF, D_MODEL), D_FF ** -0.5)
    return {'x_prompt': x_prompt, 'x_sample': x_sample,
            'cache_k': cache_k, 'cache_v': cache_v,
            'state_ssm_re': state_ssm_re, 'state_ssm_im': state_ssm_im,
            'norm1': norm1, 'w_in': w_in, 'q_norm': q_norm, 'k_norm': k_norm, 'sinks': sinks,
            'ssm_a_re': ssm_a_re, 'ssm_a_im': ssm_a_im, 'ssm_log_dt': ssm_log_dt,
            'ssm_b_re': ssm_b_re, 'ssm_b_im': ssm_b_im, 'ssm_c_re': ssm_c_re, 'ssm_c_im': ssm_c_im,
            'ssm_d': ssm_d, 'w_glu': w_glu, 'w_br_ssm': w_br_ssm, 'w_br_attn': w_br_attn,
            'w_gate': w_gate, 'w_out': w_out, 'norm2': norm2,
            'w_ffn_gate': w_ffn_gate, 'w_ffn_up': w_ffn_up, 'w_ffn_down': w_ffn_down}


def reference(x_prompt, x_sample, cache_k, cache_v, state_ssm_re, state_ssm_im,
              norm1, w_in, q_norm, k_norm, sinks,
              ssm_a_re, ssm_a_im, ssm_log_dt, ssm_b_re, ssm_b_im, ssm_c_re, ssm_c_im, ssm_d,
              w_glu, w_br_ssm, w_br_attn, w_gate, w_out, norm2,
              w_ffn_gate, w_ffn_up, w_ffn_down):
    Bp, Lp = x_prompt.shape[0], x_prompt.shape[1]
    Ls = x_sample.shape[1]
    pos_p = jnp.arange(Lp)
    pos_s = PAST_LEN + jnp.arange(Ls)
    h0p = jnp.zeros((Bp, N_GROUPS, STATE), jnp.float32)
    yp, ys = x_prompt, x_sample
    kp_l, vp_l, rp_l, ip_l, ks_l, vs_l, rs_l, is_l = [], [], [], [], [], [], [], []
    for l in range(DEPTH):
        p = (norm1[l], w_in[l], q_norm[l], k_norm[l], sinks[l],
             ssm_a_re[l], ssm_a_im[l], ssm_log_dt[l], ssm_b_re[l], ssm_b_im[l],
             ssm_c_re[l], ssm_c_im[l], ssm_d[l], w_glu[l], w_br_ssm[l], w_br_attn[l],
             w_gate[l], w_out[l], norm2[l], w_ffn_gate[l], w_ffn_up[l], w_ffn_down[l])
        yp, kp, vp, rp, ip = trunk_layer(yp, pos_p, None, None, h0p, h0p, p)
        ys, kS, vS, rS, iS = trunk_layer(ys, pos_s, cache_k[l], cache_v[l],
                                         state_ssm_re[l], state_ssm_im[l], p)
        kp_l.append(kp); vp_l.append(vp); rp_l.append(rp); ip_l.append(ip)
        ks_l.append(kS); vs_l.append(vS); rs_l.append(rS); is_l.append(iS)
    k_win_prompt = jnp.stack(kp_l)
    v_win_prompt = jnp.stack(vp_l)
    ssm_re_prompt = jnp.stack(rp_l)
    ssm_im_prompt = jnp.stack(ip_l)
    k_win_sample = jnp.stack(ks_l)
    v_win_sample = jnp.stack(vs_l)
    ssm_re_sample = jnp.stack(rs_l)
    ssm_im_sample = jnp.stack(is_l)
    return (yp, ys, k_win_prompt, v_win_prompt, ssm_re_prompt, ssm_im_prompt,
            k_win_sample, v_win_sample, ssm_re_sample, ssm_im_sample)
```

```python
import functools
import math

import jax
import jax.numpy as jnp
from jax import lax
from jax.experimental import pallas as pl
from jax.experimental.pallas import tpu as pltpu

D_MODEL = 2048
CHUNK = 64
D_SSM = 1024
GROUP_CH = 16
N_GROUPS = 64
STATE = 64
HEAD_DIM = 64
N_HEADS = 16
N_KV = 4
Q_PER_KV = 4
D_ATTN = 1024
D_KV = 256
WINDOW = 128
ROT_DIM = 16
ROPE_THETA = 500000.0
PAST_LEN = 1024
EPS = 1e-6
NEG = -1e30

BLK = 256
STEPS_PER_BLK = BLK // GROUP_CH
N_BLK = CHUNK // STEPS_PER_BLK
V7X_VMEM_LIMIT = 56 * 1024 * 1024

F32 = jnp.float32
BF16 = jnp.bfloat16


def _dot(a, b):
    return jnp.dot(a, b, preferred_element_type=F32)


def _dot_nt(a, b, precision=None):
    return lax.dot_general(a, b, (((1,), (1,)), ((), ())), preferred_element_type=F32,
                           precision=precision)


def _params(*sem):
    return pltpu.CompilerParams(dimension_semantics=sem, vmem_limit_bytes=V7X_VMEM_LIMIT)


def _ssm_prep_kernel(are_ref, aim_ref, ldt_ref, btr_ref, bti_ref, cr_ref, ci_ref, d_ref,
                     blk_ref, wst_ref, wout_ref, a2_ref,
                     lre_s, lim_s, zre_s, zim_s):
    hi = lax.Precision.HIGHEST
    a_re = are_ref[...]
    a_im = aim_ref[...]
    dt = jnp.exp(ldt_ref[...])
    lre = a_re * dt
    lim = a_im * dt

    def cpow(e):
        mag = jnp.exp(e * lre)
        ang = e * lim
        return mag * jnp.cos(ang), mag * jnp.sin(ang)

    abr, abi = cpow(jnp.ones((1, 1), F32))
    fr, fi = abr - 1.0, abi
    den = a_re * a_re + a_im * a_im
    cr = (fr * a_re + fi * a_im) / den
    ci = (fi * a_re - fr * a_im) / den
    btr = btr_ref[...]
    bti = bti_ref[...]
    bbr = cr * btr - ci * bti
    bbi = cr * bti + ci * btr
    c_r = cr_ref[...]
    c_i = ci_ref[...]

    def outer_rows(pr, pi, mr, mi, dst_r, dst_i, conj=False):
        for r in range(pr.shape[0]):
            er = pr[r:r + 1, :]
            ei = pi[r:r + 1, :]
            zr = er * mr - ei * mi
            zi = er * mi + ei * mr
            dst_r[r * GROUP_CH:(r + 1) * GROUP_CH, :] = zr
            dst_i[r * GROUP_CH:(r + 1) * GROUP_CH, :] = -zi if conj else zi

    col = lax.broadcasted_iota(jnp.int32, (CHUNK, 1), 0).astype(F32)

    e1r, e1i = cpow(float(STEPS_PER_BLK) - col[:STEPS_PER_BLK])
    outer_rows(e1r, e1i, bbr, bbi, lre_s, lim_s)
    l_re = lre_s[...]
    l_im = lim_s[...]

    for d in range(N_BLK - 1):
        e2r, e2i = cpow(col[:STEPS_PER_BLK] + float(STEPS_PER_BLK * d))
        outer_rows(e2r, e2i, c_r, c_i, zre_s.at[0:BLK], zim_s.at[0:BLK])
        z_re = zre_s[0:BLK, :]
        z_im = zim_s[0:BLK, :]
        off = _dot_nt(l_re, z_re, hi) - _dot_nt(l_im, z_im, hi)
        blk_ref[d + 1] = off.astype(BF16)
        if d == 0:
            kt = _dot_nt(bbr, z_re, hi) - _dot_nt(bbi, z_im, hi)
            rows = lax.broadcasted_iota(jnp.int32, (GROUP_CH, BLK), 0)
            lanes = lax.broadcasted_iota(jnp.int32, (GROUP_CH, BLK), 1)
            kt = kt + jnp.where(rows == lanes, d_ref[...], 0.0)
            for s0 in range(STEPS_PER_BLK):
                sh = GROUP_CH * s0
                piece = kt if s0 == 0 else jnp.where(lanes >= sh, pltpu.roll(kt, sh, axis=1), 0.0)
                blk_ref[0, s0 * GROUP_CH:(s0 + 1) * GROUP_CH, :] = piece.astype(BF16)

    e3r, e3i = cpow(float(CHUNK - 1) - col)
    outer_rows(e3r, e3i, bbr, bbi, zre_s, zim_s)
    wst_ref[:, 0:STATE] = zre_s[...].astype(BF16)
    wst_ref[:, STATE:2 * STATE] = zim_s[...].astype(BF16)

    e4r, e4i = cpow(col + 1.0)
    outer_rows(e4r, e4i, c_r, c_i, zre_s, zim_s, conj=True)
    wout_ref[:, 0:STATE] = zre_s[...].astype(BF16)
    wout_ref[:, STATE:2 * STATE] = zim_s[...].astype(BF16)

    a64r = e4r[CHUNK - 1:CHUNK, :]
    a64i = e4i[CHUNK - 1:CHUNK, :]
    a2_ref[0:1, :] = jnp.concatenate([a64r, a64r], axis=1)
    a2_ref[1:2, :] = jnp.concatenate([-a64i, a64i], axis=1)


def _ssm_prep(a_re, a_im, log_dt, b_re, b_im, c_re, c_im, d_skip):
    g = N_GROUPS
    row = lambda a: a.astype(F32).reshape(g, 1, -1)
    bt_re = jnp.swapaxes(b_re.astype(F32), 1, 2)
    bt_im = jnp.swapaxes(b_im.astype(F32), 1, 2)
    d_pad = jnp.pad(d_skip.astype(F32).reshape(g, 1, GROUP_CH), ((0, 0), (0, 0), (0, BLK - GROUP_CH)))
    vec = lambda n: pl.BlockSpec((None, 1, n), lambda i: (i, 0, 0))
    mat = pl.BlockSpec((None, GROUP_CH, STATE), lambda i: (i, 0, 0))
    return pl.pallas_call(
        _ssm_prep_kernel,
        grid=(g,),
        in_specs=[vec(STATE), vec(STATE), vec(1), mat, mat, mat, mat, vec(BLK)],
        out_specs=[pl.BlockSpec((None, N_BLK, BLK, BLK), lambda i: (i, 0, 0, 0)),
                   pl.BlockSpec((None, CHUNK * GROUP_CH, 2 * STATE), lambda i: (i, 0, 0)),
                   pl.BlockSpec((None, CHUNK * GROUP_CH, 2 * STATE), lambda i: (i, 0, 0)),
                   pl.BlockSpec((None, 2, 2 * STATE), lambda i: (i, 0, 0))],
        out_shape=[jax.ShapeDtypeStruct((g, N_BLK, BLK, BLK), BF16),
                   jax.ShapeDtypeStruct((g, CHUNK * GROUP_CH, 2 * STATE), BF16),
                   jax.ShapeDtypeStruct((g, CHUNK * GROUP_CH, 2 * STATE), BF16),
                   jax.ShapeDtypeStruct((g, 2, 2 * STATE), F32)],
        scratch_shapes=[pltpu.VMEM((BLK, STATE), F32), pltpu.VMEM((BLK, STATE), F32),
                        pltpu.VMEM((CHUNK * GROUP_CH, STATE), F32),
                        pltpu.VMEM((CHUNK * GROUP_CH, STATE), F32)],
        compiler_params=_params("arbitrary"),
        name="ssm_prep",
    )(row(a_re), row(a_im), row(log_dt), bt_re, bt_im, c_re.astype(F32), c_im.astype(F32), d_pad)


def _rms(x, g):
    return x * lax.rsqrt(jnp.mean(x * x, axis=-1, keepdims=True) + EPS) * g


def _in_proj_kernel(x_ref, n1_ref, w_ref, seg_ref, qg_ref, kg_ref, rc_ref, rm_ref, rp_ref,
                    u_ref, q_ref, k_ref, v_ref):
    xn = _rms(x_ref[...], n1_ref[...]).astype(BF16)
    u_ref[...] = _dot(xn, w_ref[:, 0:D_SSM]).astype(u_ref.dtype)
    v_ref[...] = _dot(xn, w_ref[:, D_SSM + D_ATTN + D_KV:])
    rc = rc_ref[...]
    rm = rm_ref[...]
    rp = rp_ref[...]

    def norm_rope(z, gain, width, scale):
        sq = (z * z).astype(BF16)
        outs = []
        for cb in range(width // 128):
            sl = slice(cb * 128, (cb + 1) * 128)
            ms = _dot(sq[:, sl], seg_ref[...]) * (1.0 / HEAD_DIM)
            zn = z[:, sl] * lax.rsqrt(ms + EPS) * gain
            rot = zn * rc + pltpu.roll(zn, 128 - ROT_DIM // 2, axis=1) * rm \
                + pltpu.roll(zn, ROT_DIM // 2, axis=1) * rp
            outs.append(rot * scale)
        return outs

    q = _dot(xn, w_ref[:, D_SSM:D_SSM + D_ATTN])
    for cb, blk in enumerate(norm_rope(q, qg_ref[...], D_ATTN, HEAD_DIM ** -0.5)):
        q_ref[:, cb * 128:(cb + 1) * 128] = blk.astype(q_ref.dtype)
    k = _dot(xn, w_ref[:, D_SSM + D_ATTN:D_SSM + D_ATTN + D_KV])
    for cb, blk in enumerate(norm_rope(k, kg_ref[...], D_KV, 1.0)):
        k_ref[:, cb * 128:(cb + 1) * 128] = blk


def _rope_tables(pos):
    half = ROT_DIM // 2
    inv = ROPE_THETA ** (-jnp.arange(half, dtype=F32) * 2.0 / ROT_DIM)
    ang = pos.astype(F32)[:, None] * inv[None, :]
    cos, sin = jnp.cos(ang), jnp.sin(ang)
    n = pos.shape[0]
    ones = jnp.ones((n, HEAD_DIM - ROT_DIM), F32)
    zeros = jnp.zeros((n, HEAD_DIM - ROT_DIM), F32)
    zh = jnp.zeros((n, half), F32)
    rc = jnp.concatenate([cos, cos, ones], axis=1)
    rm = jnp.concatenate([-sin, zh, zeros], axis=1)
    rp = jnp.concatenate([zh, sin, zeros], axis=1)
    two = lambda t: jnp.concatenate([t, t], axis=1)
    return two(rc), two(rm), two(rp)


def _in_proj(x2d, pos, tm, norm1, w_in_b, seg, qg, kg):
    t = x2d.shape[0]
    rc, rm, rp = _rope_tables(pos)
    nrope = pos.shape[0] // tm
    rope_spec = pl.BlockSpec((tm, 128), lambda i: (i % nrope, 0))
    const = lambda shape: pl.BlockSpec(shape, lambda i: (0, 0))
    tok = lambda w: pl.BlockSpec((tm, w), lambda i: (i, 0))
    return pl.pallas_call(
        _in_proj_kernel,
        grid=(t // tm,),
        in_specs=[tok(D_MODEL), const((1, D_MODEL)), const(w_in_b.shape), const((128, 128)),
                  const((1, 128)), const((1, 128)), rope_spec, rope_spec, rope_spec],
        out_specs=[tok(D_SSM), tok(D_ATTN), tok(D_KV), tok(D_KV)],
        out_shape=[jax.ShapeDtypeStruct((t, D_SSM), BF16), jax.ShapeDtypeStruct((t, D_ATTN), BF16),
                   jax.ShapeDtypeStruct((t, D_KV), F32), jax.ShapeDtypeStruct((t, D_KV), F32)],
        compiler_params=_params("arbitrary"),
        name="in_proj",
    )(x2d, norm1, w_in_b, seg, qg, kg, rc, rm, rp)


def _ssm_state_kernel(x_ref, wst_ref, sp_ref, ss_ref, *, mp):
    s = _dot(x_ref[...], wst_ref[...])
    sp_ref[...] = s[:mp]
    ss_ref[...] = s[mp:]


def _cmul_add(h, ar2, ai2, s):
    return h * ar2 + pltpu.roll(h, STATE, axis=1) * ai2 + s


def _ssm_scan_kernel(sp_ref, ss_ref, arp_ref, aip_ref, ars_ref, ais_ref, h0s_ref,
                     hin_ref, hfp_ref, hfs_ref, *, nchunk, rows):
    ar = arp_ref[...]
    ai = aip_ref[...]
    h = jnp.zeros((rows, 2 * STATE), F32)
    for k in range(nchunk):
        hin_ref[pl.ds(k, rows, stride=nchunk), :] = h
        h = _cmul_add(h, ar, ai, sp_ref[pl.ds(k, rows, stride=nchunk), :])
    hfp_ref[...] = h
    hfs_ref[...] = _cmul_add(h0s_ref[...], ars_ref[...], ais_ref[...], ss_ref[...])


def _ssm_out_kernel(x_ref, blk_ref, wout_ref, hinp_ref, h0s_ref, y_ref):
    hin = jnp.concatenate([hinp_ref[...], h0s_ref[...]], axis=0).astype(BF16)
    for tb in range(N_BLK):
        acc = _dot_nt(hin, wout_ref[tb * BLK:(tb + 1) * BLK, :])
        for sb in range(tb + 1):
            acc = acc + _dot(x_ref[:, sb * BLK:(sb + 1) * BLK], blk_ref[tb - sb])
        y_ref[:, tb * BLK:(tb + 1) * BLK] = acc


def _ssm(xg, blk, wst, wout, a2, h0s, bp, nchunk, bs):
    g = N_GROUPS
    mp = bp * nchunk
    m = mp + bs
    kdim = CHUNK * GROUP_CH
    sp, ss = pl.pallas_call(
        functools.partial(_ssm_state_kernel, mp=mp),
        grid=(g,),
        in_specs=[pl.BlockSpec((None, m, kdim), lambda i: (i, 0, 0)),
                  pl.BlockSpec((None, kdim, 2 * STATE), lambda i: (i, 0, 0))],
        out_specs=[pl.BlockSpec((mp, 2 * STATE), lambda i: (i, 0)),
                   pl.BlockSpec((bs, 2 * STATE), lambda i: (i, 0))],
        out_shape=[jax.ShapeDtypeStruct((g * mp, 2 * STATE), F32),
                   jax.ShapeDtypeStruct((g * bs, 2 * STATE), F32)],
        compiler_params=_params("arbitrary"),
        name="ssm_state",
    )(xg, wst)

    arp = jnp.repeat(a2[:, 0, :], bp, axis=0)
    aip = jnp.repeat(a2[:, 1, :], bp, axis=0)
    ars = jnp.repeat(a2[:, 0, :], bs, axis=0)
    ais = jnp.repeat(a2[:, 1, :], bs, axis=0)
    full = lambda r: pl.BlockSpec((r, 2 * STATE), lambda i: (0, 0))
    hin, hfp, hfs = pl.pallas_call(
        functools.partial(_ssm_scan_kernel, nchunk=nchunk, rows=g * bp),
        grid=(1,),
        in_specs=[full(g * mp), full(g * bs), full(g * bp), full(g * bp), full(g * bs), full(g * bs),
                  full(g * bs)],
        out_specs=[full(g * mp), full(g * bp), full(g * bs)],
        out_shape=[jax.ShapeDtypeStruct((g * mp, 2 * STATE), F32),
                   jax.ShapeDtypeStruct((g * bp, 2 * STATE), F32),
                   jax.ShapeDtypeStruct((g * bs, 2 * STATE), F32)],
        compiler_params=_params("arbitrary"),
        name="ssm_scan",
    )(sp, ss, arp, aip, ars, ais, h0s)

    y = pl.pallas_call(
        _ssm_out_kernel,
        grid=(g,),
        in_specs=[pl.BlockSpec((None, m, kdim), lambda i: (i, 0, 0)),
                  pl.BlockSpec((None, N_BLK, BLK, BLK), lambda i: (i, 0, 0, 0)),
                  pl.BlockSpec((None, kdim, 2 * STATE), lambda i: (i, 0, 0)),
                  pl.BlockSpec((mp, 2 * STATE), lambda i: (i, 0)),
                  pl.BlockSpec((bs, 2 * STATE), lambda i: (i, 0))],
        out_specs=pl.BlockSpec((None, m, kdim), lambda i: (i, 0, 0)),
        out_shape=jax.ShapeDtypeStruct((g, m, kdim), F32),
        compiler_params=_params("arbitrary"),
        name="ssm_out",
    )(xg, blk, wout, hin, h0s)
    return y, hfp, hfs


def _attn_kernel(sinks_ref, q_ref, kp_ref, kc_ref, vp_ref, vc_ref, o_ref, *, qt, masked):
    i = pl.program_id(1)
    n_cl = qt // CHUNK
    span = WINDOW + CHUNK
    kall = jnp.concatenate([kp_ref[...], kc_ref[...]], axis=0).astype(BF16)
    vall = jnp.concatenate([vp_ref[...], vc_ref[...]], axis=0).astype(BF16)
    for cl in range(n_cl):
        rows = slice(cl * CHUNK, (cl + 1) * CHUNK)
        for kv in range(N_KV):
            heads = [kv * Q_PER_KV + g for g in range(Q_PER_KV)]
            qs = jnp.concatenate([q_ref[rows, h * HEAD_DIM:(h + 1) * HEAD_DIM] for h in heads], axis=0)
            kk = kall[cl * CHUNK:cl * CHUNK + span, kv * HEAD_DIM:(kv + 1) * HEAD_DIM]
            vv = vall[cl * CHUNK:cl * CHUNK + span, kv * HEAD_DIM:(kv + 1) * HEAD_DIM]
            s = _dot_nt(qs, kk)
            if masked:
                first = (WINDOW // CHUNK - (i * n_cl + cl)) * CHUNK
                col = lax.broadcasted_iota(jnp.int32, s.shape, 1)
                s = jnp.where(col >= first, s, NEG)
            sink = jnp.concatenate([jnp.full((CHUNK, 1), sinks_ref[h], F32) for h in heads], axis=0)
            m = jnp.maximum(jnp.max(s, axis=-1, keepdims=True), sink)
            p = jnp.exp(s - m)
            den = jnp.sum(p, axis=-1, keepdims=True) + jnp.exp(sink - m)
            o = _dot(p.astype(BF16), vv) * (1.0 / den)
            for g, h in enumerate(heads):
                o_ref[rows, h * HEAD_DIM:(h + 1) * HEAD_DIM] = \
                    o[g * CHUNK:(g + 1) * CHUNK].astype(o_ref.dtype)


def _attention(sinks, q, k_prev, k_cur, v_prev, v_cur, qt, masked):
    b, l, _ = q.shape
    per = qt // WINDOW
    if masked:
        prev_map = lambda bi, i: (bi, jnp.maximum(i * per - 1, 0), 0)
    else:
        prev_map = lambda bi, i: (bi, 0, 0)
    cur = lambda w: pl.BlockSpec((None, qt, w), lambda bi, i: (bi, i, 0))
    prev = pl.BlockSpec((None, WINDOW, D_KV), prev_map)
    return pl.pallas_call(
        functools.partial(_attn_kernel, qt=qt, masked=masked),
        grid=(b, l // qt),
        in_specs=[pl.BlockSpec(memory_space=pltpu.SMEM), cur(D_ATTN), prev, cur(D_KV), prev, cur(D_KV)],
        out_specs=cur(D_ATTN),
        out_shape=jax.ShapeDtypeStruct((b, l, D_ATTN), BF16),
        compiler_params=_params("arbitrary", "arbitrary"),
        name="attention",
    )(sinks, q, k_prev, k_cur, v_prev, v_cur)


def _mix_kernel(x_ref, y_ref, o_ref, n1_ref, wglu_ref, wga_ref, wgb_ref, wbs_ref, wba_ref, wout_ref,
                h_ref, xn_s, ys_s):
    j = pl.program_id(1)

    @pl.when(j == 0)
    def _():
        x = x_ref[...]
        xn_s[...] = _rms(x, n1_ref[...]).astype(BF16)
        ya = jax.nn.gelu(y_ref[...])
        ys_s[...] = (ya * jax.nn.sigmoid(_dot(ya.astype(BF16), wglu_ref[...]))).astype(BF16)
        h_ref[...] = x

    xn = xn_s[...]
    ga = jax.nn.sigmoid(_dot(xn, wga_ref[...]))
    gb = jax.nn.sigmoid(_dot(xn, wgb_ref[...]))
    mixed = ga * _dot(ys_s[...], wbs_ref[...]) + gb * _dot(o_ref[...], wba_ref[...])
    h_ref[...] += _dot(mixed.astype(BF16), wout_ref[...])


def _mix(x2d, y2d, o2d, norm1, wglu, wgate, wbs, wba, wout, tm, tn):
    t = x2d.shape[0]
    nj = D_MODEL // tn
    tok = lambda w: pl.BlockSpec((tm, w), lambda i, j: (i, 0))
    return pl.pallas_call(
        _mix_kernel,
        grid=(t // tm, nj),
        in_specs=[tok(D_MODEL), tok(D_SSM), tok(D_ATTN),
                  pl.BlockSpec((1, D_MODEL), lambda i, j: (0, 0)),
                  pl.BlockSpec((D_SSM, D_SSM), lambda i, j: (0, 0)),
                  pl.BlockSpec((D_MODEL, tn), lambda i, j: (0, j)),
                  pl.BlockSpec((D_MODEL, tn), lambda i, j: (0, nj + j)),
                  pl.BlockSpec((D_SSM, tn), lambda i, j: (0, j)),
                  pl.BlockSpec((D_ATTN, tn), lambda i, j: (0, j)),
                  pl.BlockSpec((tn, D_MODEL), lambda i, j: (j, 0))],
        out_specs=tok(D_MODEL),
        out_shape=jax.ShapeDtypeStruct((t, D_MODEL), F32),
        scratch_shapes=[pltpu.VMEM((tm, D_MODEL), BF16), pltpu.VMEM((tm, D_SSM), BF16)],
        compiler_params=_params("arbitrary", "arbitrary"),
        name="mix",
    )(x2d, y2d, o2d, norm1, wglu, wgate, wgate, wbs, wba, wout)


def _ffn_kernel(h_ref, n2_ref, wg_ref, wu_ref, wd_ref, out_ref, hn_s):
    j = pl.program_id(1)

    @pl.when(j == 0)
    def _():
        h = h_ref[...]
        hn_s[...] = _rms(h, n2_ref[...]).astype(BF16)
        out_ref[...] = h

    hn = hn_s[...]
    act = jax.nn.silu(_dot(hn, wg_ref[...])) * _dot(hn, wu_ref[...])
    out_ref[...] += _dot(act.astype(BF16), wd_ref[...])


def _ffn(h2d, norm2, wg, wu, wd, tm, tf):
    t = h2d.shape[0]
    d_ff = wg.shape[1]
    tok = pl.BlockSpec((tm, D_MODEL), lambda i, j: (i, 0))
    return pl.pallas_call(
        _ffn_kernel,
        grid=(t // tm, d_ff // tf),
        in_specs=[tok, pl.BlockSpec((1, D_MODEL), lambda i, j: (0, 0)),
                  pl.BlockSpec((D_MODEL, tf), lambda i, j: (0, j)),
                  pl.BlockSpec((D_MODEL, tf), lambda i, j: (0, j)),
                  pl.BlockSpec((tf, D_MODEL), lambda i, j: (j, 0))],
        out_specs=tok,
        out_shape=jax.ShapeDtypeStruct((t, D_MODEL), F32),
        scratch_shapes=[pltpu.VMEM((tm, D_MODEL), BF16)],
        compiler_params=_params("arbitrary", "arbitrary"),
        name="ffn",
    )(h2d, norm2, wg, wu, wd)


def _to_groups(u2d, rows):
    t = u2d.shape[0]
    return u2d.reshape(t * GROUP_CH, N_GROUPS).T.reshape(N_GROUPS, rows, CHUNK * GROUP_CH)


def _from_groups(yg):
    g, rows, kdim = yg.shape
    return yg.reshape(g, rows * kdim).T.reshape(rows * CHUNK, D_SSM)


def kernel(x_prompt, x_sample, cache_k, cache_v, state_ssm_re, state_ssm_im, norm1, w_in, q_norm, k_norm,
           sinks, ssm_a_re, ssm_a_im, ssm_log_dt, ssm_b_re, ssm_b_im, ssm_c_re, ssm_c_im, ssm_d, w_glu,
           w_br_ssm, w_br_attn, w_gate, w_out, norm2, w_ffn_gate, w_ffn_up, w_ffn_down):
    bp, lp, _ = x_prompt.shape
    bs, ls, _ = x_sample.shape
    ncp = lp // CHUNK
    assert ls == CHUNK and cache_k.shape[2] == WINDOW and norm1.shape[0] == 1
    l = 0

    perm = lambda a, axis: jnp.swapaxes(
        a.reshape(a.shape[:axis] + (N_GROUPS, GROUP_CH) + a.shape[axis + 1:]), axis, axis + 1
    ).reshape(a.shape)
    w_in_l = w_in[l]
    w_in_b = jnp.concatenate([perm(w_in_l[:, :D_SSM], 1), w_in_l[:, D_SSM:]], axis=1).astype(BF16)
    wglu_b = perm(perm(w_glu[l], 0), 1).astype(BF16)
    wbs_b = perm(w_br_ssm[l], 0).astype(BF16)
    wba_b = w_br_attn[l].astype(BF16)
    wgate_b = w_gate[l].astype(BF16)
    wout_b = w_out[l].astype(BF16)
    wfg_b = w_ffn_gate[l].astype(BF16)
    wfu_b = w_ffn_up[l].astype(BF16)
    wfd_b = w_ffn_down[l].astype(BF16)
    n1 = norm1[l].astype(F32).reshape(1, D_MODEL)
    n2 = norm2[l].astype(F32).reshape(1, D_MODEL)
    qg = jnp.tile(q_norm[l].astype(F32), 2).reshape(1, 128)
    kg = jnp.tile(k_norm[l].astype(F32), 2).reshape(1, 128)
    lane_head = jnp.arange(128) // HEAD_DIM
    seg = (lane_head[:, None] == lane_head[None, :]).astype(BF16)
    sink = sinks[l].astype(F32)

    blk, wst, wout_t, a2 = _ssm_prep(ssm_a_re[l], ssm_a_im[l], ssm_log_dt[l], ssm_b_re[l], ssm_b_im[l],
                                     ssm_c_re[l], ssm_c_im[l], ssm_d[l])

    xp = x_prompt.astype(F32).reshape(bp * lp, D_MODEL)
    xs = x_sample.astype(F32).reshape(bs * ls, D_MODEL)
    tm = 512
    up, qp, kp, vp = _in_proj(xp, jnp.arange(lp), tm, n1, w_in_b, seg, qg, kg)
    us, qs, ks, vs = _in_proj(xs, jnp.tile(PAST_LEN + jnp.arange(ls), tm // ls), tm, n1, w_in_b, seg, qg, kg)

    xg = jnp.concatenate([_to_groups(up, bp * ncp), _to_groups(us, bs)], axis=1)
    h0s = jnp.concatenate([state_ssm_re[l], state_ssm_im[l]], axis=-1).astype(F32)
    h0s = jnp.swapaxes(h0s, 0, 1).reshape(N_GROUPS * bs, 2 * STATE)
    yg, hfp, hfs = _ssm(xg, blk, wst, wout_t, a2, h0s, bp, ncp, bs)
    y_p = _from_groups(yg[:, :bp * ncp])
    y_s = _from_groups(yg[:, bp * ncp:])

    def states(hf, b):
        hf = jnp.swapaxes(hf.reshape(N_GROUPS, b, 2 * STATE), 0, 1)
        return hf[None, :, :, :STATE], hf[None, :, :, STATE:]

    kp3, vp3 = kp.reshape(bp, lp, D_KV), vp.reshape(bp, lp, D_KV)
    o_p = _attention(sink, qp.reshape(bp, lp, D_ATTN), kp3, kp3, vp3, vp3, 256, True)
    ck = cache_k[l].astype(F32).reshape(bs, WINDOW, D_KV)
    cv = cache_v[l].astype(F32).reshape(bs, WINDOW, D_KV)
    ks3, vs3 = ks.reshape(bs, ls, D_KV), vs.reshape(bs, ls, D_KV)
    o_s = _attention(sink, qs.reshape(bs, ls, D_ATTN), ck, ks3, cv, vs3, ls, False)

    outs = []
    for x2d, y2d, o3d in ((xp, y_p, o_p), (xs, y_s, o_s)):
        h = _mix(x2d, y2d, o3d.reshape(-1, D_ATTN), n1, wglu_b, wgate_b, wbs_b, wba_b, wout_b, 512, 512)
        outs.append(_ffn(h, n2, wfg_b, wfu_b, wfd_b, 1024, 512))

    win = lambda a, b, n: a.reshape(b, n, N_KV, HEAD_DIM)[None]
    k_win_p = win(kp3[:, lp - WINDOW:], bp, WINDOW)
    v_win_p = win(vp3[:, lp - WINDOW:], bp, WINDOW)
    k_win_s = win(jnp.concatenate([ck, ks3], axis=1)[:, ls:], bs, WINDOW)
    v_win_s = win(jnp.concatenate([cv, vs3], axis=1)[:, ls:], bs, WINDOW)
    re_p, im_p = states(hfp, bp)
    re_s, im_s = states(hfs, bs)
    dt = x_prompt.dtype
    return (outs[0].reshape(bp, lp, D_MODEL).astype(dt), outs[1].reshape(bs, ls, D_MODEL).astype(dt),
            k_win_p.astype(dt), v_win_p.astype(dt), re_p.astype(dt), im_p.astype(dt),
            k_win_s.astype(dt), v_win_s.astype(dt), re_s.astype(dt), im_s.astype(dt))
```

```python
import functools
import math

import jax
import jax.numpy as jnp
from jax import lax
from jax.experimental import pallas as pl
from jax.experimental.pallas import tpu as pltpu

D_MODEL = 2048
CHUNK = 64
D_SSM = 1024
GROUP_CH = 16
N_GROUPS = 64
STATE = 64
HEAD_DIM = 64
N_HEADS = 16
N_KV = 4
Q_PER_KV = 4
D_ATTN = 1024
D_KV = 256
WINDOW = 128
ROT_DIM = 16
ROPE_THETA = 500000.0
PAST_LEN = 1024
EPS = 1e-6
NEG = -1e30

BLK = 256
STEPS_PER_BLK = BLK // GROUP_CH
N_BLK = CHUNK // STEPS_PER_BLK
V7X_VMEM_LIMIT = 56 * 1024 * 1024

F32 = jnp.float32
BF16 = jnp.bfloat16


def _dot(a, b):
    return jnp.dot(a, b, preferred_element_type=F32)


def _dot_nt(a, b, precision=None):
    return lax.dot_general(a, b, (((1,), (1,)), ((), ())), preferred_element_type=F32,
                           precision=precision)


def _params(*sem):
    return pltpu.CompilerParams(dimension_semantics=sem, vmem_limit_bytes=V7X_VMEM_LIMIT)


def _ssm_prep_kernel(are_ref, aim_ref, ldt_ref, bt_ref, bts_ref, c_ref, cs_ref, d_ref,
                     blk_ref, wst_ref, wout_ref, a2_ref, l_s, z_s, w_s):
    lane = lax.broadcasted_iota(jnp.int32, (1, 2 * STATE), 1)
    sg = jnp.where(lane < STATE, 1.0, -1.0).astype(F32)
    a_re = are_ref[...]
    a_im = aim_ref[...]
    dt = jnp.exp(ldt_ref[...])
    lre = a_re * dt
    lim = a_im * dt

    def cpow(e):
        mag = jnp.exp(e * lre)
        ang = e * lim
        return mag * jnp.cos(ang), mag * jnp.sin(ang)

    def cmul(ar, ai, br, bi):
        return ar * br - ai * bi, ar * bi + ai * br

    col = lax.broadcasted_iota(jnp.int32, (STEPS_PER_BLK, 1), 0).astype(F32)
    t_r, t_i = cpow(col)
    r_r, r_i = cpow(float(STEPS_PER_BLK - 1) - col)
    h_r, h_i = cpow(float(STEPS_PER_BLK) * col[:8])
    ab_r, ab_i = t_r[1:2], t_i[1:2]

    fr, fi = ab_r - 1.0, ab_i
    den = a_re * a_re + a_im * a_im
    cr = (fr * a_re + fi * a_im) / den
    ci = (fi * a_re - fr * a_im) / den
    bb = cr * bt_ref[...] - (ci * sg) * bts_ref[...]
    bbs = cr * bts_ref[...] + (ci * sg) * bt_ref[...]
    cc = c_ref[...]
    ccs = cs_ref[...]

    def outer_rows(pr, pi, m, ms, dst, row0, conj):
        pa = pr * sg if conj else pr
        pb = -pi if conj else -(pi * sg)
        for r in range(pr.shape[0]):
            z = pa[r:r + 1, :] * m + pb[r:r + 1, :] * ms
            dst[row0 + r * GROUP_CH:row0 + (r + 1) * GROUP_CH, :] = z

    e_r, e_i = cmul(ab_r, ab_i, r_r, r_i)
    outer_rows(e_r, e_i, bb, bbs, l_s, 0, False)
    l2 = l_s[...].astype(BF16)

    for d in range(N_BLK - 1):
        e_r, e_i = cmul(h_r[d:d + 1], h_i[d:d + 1], t_r, t_i)
        outer_rows(e_r, e_i, cc, ccs, z_s, 0, True)
        z2 = z_s[...].astype(BF16)
        blk_ref[d + 1] = _dot_nt(l2, z2).astype(BF16)
        if d == 0:
            kt = _dot_nt(bb.astype(BF16), z2)
            rows = lax.broadcasted_iota(jnp.int32, (GROUP_CH, BLK), 0)
            lanes = lax.broadcasted_iota(jnp.int32, (GROUP_CH, BLK), 1)
            kt = kt + jnp.where(rows == lanes, d_ref[...], 0.0)
            for s0 in range(STEPS_PER_BLK):
                sh = GROUP_CH * s0
                piece = kt if s0 == 0 else jnp.where(lanes >= sh, pltpu.roll(kt, sh, axis=1), 0.0)
                blk_ref[0, s0 * GROUP_CH:(s0 + 1) * GROUP_CH, :] = piece.astype(BF16)

    for sb in range(N_BLK):
        m = N_BLK - 1 - sb
        e_r, e_i = cmul(h_r[m:m + 1], h_i[m:m + 1], r_r, r_i)
        outer_rows(e_r, e_i, bb, bbs, w_s, sb * BLK, False)
    wst_ref[...] = w_s[...].astype(BF16)

    for tb in range(N_BLK):
        e_r, e_i = cmul(*cmul(ab_r, ab_i, h_r[tb:tb + 1], h_i[tb:tb + 1]), t_r, t_i)
        outer_rows(e_r, e_i, cc, ccs, w_s, tb * BLK, True)
    wout_ref[...] = w_s[...].astype(BF16)

    a2_ref[0:1, :] = h_r[N_BLK:N_BLK + 1]
    a2_ref[1:2, :] = -(h_i[N_BLK:N_BLK + 1] * sg)


def _ssm_prep(a_re, a_im, log_dt, b_re, b_im, c_re, c_im, d_skip):
    g = N_GROUPS
    f = lambda a: a.astype(F32)
    dup = lambda a: jnp.concatenate([f(a), f(a)], axis=-1).reshape(g, 1, 2 * STATE)
    bt_re = jnp.swapaxes(f(b_re), 1, 2)
    bt_im = jnp.swapaxes(f(b_im), 1, 2)
    bt = jnp.concatenate([bt_re, bt_im], axis=-1)
    bts = jnp.concatenate([bt_im, bt_re], axis=-1)
    cc = jnp.concatenate([f(c_re), f(c_im)], axis=-1)
    ccs = jnp.concatenate([f(c_im), f(c_re)], axis=-1)
    d_pad = jnp.pad(f(d_skip).reshape(g, 1, GROUP_CH), ((0, 0), (0, 0), (0, BLK - GROUP_CH)))
    vec = lambda n: pl.BlockSpec((None, 1, n), lambda i: (i, 0, 0))
    mat = pl.BlockSpec((None, GROUP_CH, 2 * STATE), lambda i: (i, 0, 0))
    kdim = CHUNK * GROUP_CH
    return pl.pallas_call(
        _ssm_prep_kernel,
        grid=(g,),
        in_specs=[vec(2 * STATE), vec(2 * STATE), vec(1), mat, mat, mat, mat, vec(BLK)],
        out_specs=[pl.BlockSpec((None, N_BLK, BLK, BLK), lambda i: (i, 0, 0, 0)),
                   pl.BlockSpec((None, kdim, 2 * STATE), lambda i: (i, 0, 0)),
                   pl.BlockSpec((None, kdim, 2 * STATE), lambda i: (i, 0, 0)),
                   pl.BlockSpec((None, 2, 2 * STATE), lambda i: (i, 0, 0))],
        out_shape=[jax.ShapeDtypeStruct((g, N_BLK, BLK, BLK), BF16),
                   jax.ShapeDtypeStruct((g, kdim, 2 * STATE), BF16),
                   jax.ShapeDtypeStruct((g, kdim, 2 * STATE), BF16),
                   jax.ShapeDtypeStruct((g, 2, 2 * STATE), F32)],
        scratch_shapes=[pltpu.VMEM((BLK, 2 * STATE), F32), pltpu.VMEM((BLK, 2 * STATE), F32),
                        pltpu.VMEM((kdim, 2 * STATE), F32)],
        compiler_params=_params("arbitrary"),
        name="ssm_prep",
    )(dup(a_re), dup(a_im), f(log_dt).reshape(g, 1, 1), bt, bts, cc, ccs, d_pad)


def _rms(x, g):
    return x * lax.rsqrt(jnp.mean(x * x, axis=-1, keepdims=True) + EPS) * g


def _in_proj_kernel(x_ref, n1_ref, w_ref, seg_ref, qg_ref, kg_ref, rc_ref, rm_ref, rp_ref,
                    u_ref, q_ref, k_ref, v_ref):
    xn = _rms(x_ref[...], n1_ref[...]).astype(BF16)
    u_ref[...] = _dot(xn, w_ref[:, 0:D_SSM]).astype(u_ref.dtype)
    v_ref[...] = _dot(xn, w_ref[:, D_SSM + D_ATTN + D_KV:])
    rc = rc_ref[...]
    rm = rm_ref[...]
    rp = rp_ref[...]

    def norm_rope(z, gain, width, scale):
        sq = (z * z).astype(BF16)
        outs = []
        for cb in range(width // 128):
            sl = slice(cb * 128, (cb + 1) * 128)
            ms = _dot(sq[:, sl], seg_ref[...]) * (1.0 / HEAD_DIM)
            zn = z[:, sl] * lax.rsqrt(ms + EPS) * gain
            rot = zn * rc + pltpu.roll(zn, 128 - ROT_DIM // 2, axis=1) * rm \
                + pltpu.roll(zn, ROT_DIM // 2, axis=1) * rp
            outs.append(rot * scale)
        return outs

    q = _dot(xn, w_ref[:, D_SSM:D_SSM + D_ATTN])
    for cb, blk in enumerate(norm_rope(q, qg_ref[...], D_ATTN, HEAD_DIM ** -0.5)):
        q_ref[:, cb * 128:(cb + 1) * 128] = blk.astype(q_ref.dtype)
    k = _dot(xn, w_ref[:, D_SSM + D_ATTN:D_SSM + D_ATTN + D_KV])
    for cb, blk in enumerate(norm_rope(k, kg_ref[...], D_KV, 1.0)):
        k_ref[:, cb * 128:(cb + 1) * 128] = blk


def _rope_tables(pos):
    half = ROT_DIM // 2
    inv = ROPE_THETA ** (-jnp.arange(half, dtype=F32) * 2.0 / ROT_DIM)
    ang = pos.astype(F32)[:, None] * inv[None, :]
    cos, sin = jnp.cos(ang), jnp.sin(ang)
    n = pos.shape[0]
    ones = jnp.ones((n, HEAD_DIM - ROT_DIM), F32)
    zeros = jnp.zeros((n, HEAD_DIM - ROT_DIM), F32)
    zh = jnp.zeros((n, half), F32)
    rc = jnp.concatenate([cos, cos, ones], axis=1)
    rm = jnp.concatenate([-sin, zh, zeros], axis=1)
    rp = jnp.concatenate([zh, sin, zeros], axis=1)
    two = lambda t: jnp.concatenate([t, t], axis=1)
    return two(rc), two(rm), two(rp)


def _in_proj(x2d, pos, tm, norm1, w_in_b, seg, qg, kg):
    t = x2d.shape[0]
    rc, rm, rp = _rope_tables(pos)
    nrope = pos.shape[0] // tm
    rope_spec = pl.BlockSpec((tm, 128), lambda i: (i % nrope, 0))
    const = lambda shape: pl.BlockSpec(shape, lambda i: (0, 0))
    tok = lambda w: pl.BlockSpec((tm, w), lambda i: (i, 0))
    return pl.pallas_call(
        _in_proj_kernel,
        grid=(t // tm,),
        in_specs=[tok(D_MODEL), const((1, D_MODEL)), const(w_in_b.shape), const((128, 128)),
                  const((1, 128)), const((1, 128)), rope_spec, rope_spec, rope_spec],
        out_specs=[tok(D_SSM), tok(D_ATTN), tok(D_KV), tok(D_KV)],
        out_shape=[jax.ShapeDtypeStruct((t, D_SSM), BF16), jax.ShapeDtypeStruct((t, D_ATTN), BF16),
                   jax.ShapeDtypeStruct((t, D_KV), F32), jax.ShapeDtypeStruct((t, D_KV), F32)],
        compiler_params=_params("arbitrary"),
        name="in_proj",
    )(x2d, norm1, w_in_b, seg, qg, kg, rc, rm, rp)


def _ssm_state_kernel(xp_ref, xs_ref, wst_ref, sp_ref, ss_ref):
    sp_ref[...] = _dot(xp_ref[...], wst_ref[...])
    ss_ref[...] = _dot(xs_ref[...], wst_ref[...])


def _cmul_add(h, ar2, ai2, s):
    return h * ar2 + pltpu.roll(h, STATE, axis=1) * ai2 + s


def _ssm_scan_kernel(sp_ref, ss_ref, arp_ref, aip_ref, ars_ref, ais_ref, h0s_ref,
                     hin_ref, hfp_ref, hfs_ref, *, nchunk, rows):
    ar = arp_ref[...]
    ai = aip_ref[...]
    h = jnp.zeros((rows, 2 * STATE), F32)
    for k in range(nchunk):
        hin_ref[pl.ds(k, rows, stride=nchunk), :] = h
        h = _cmul_add(h, ar, ai, sp_ref[pl.ds(k, rows, stride=nchunk), :])
    hfp_ref[...] = h
    hfs_ref[...] = _cmul_add(h0s_ref[...], ars_ref[...], ais_ref[...], ss_ref[...])


def _ssm_out_kernel(xp_ref, xs_ref, blk_ref, wout_ref, hinp_ref, h0s_ref, yp_ref, ys_ref, x_s):
    mp = xp_ref.shape[0]
    x_s[:mp] = xp_ref[...]
    x_s[mp:] = xs_ref[...]
    hin = jnp.concatenate([hinp_ref[...], h0s_ref[...]], axis=0).astype(BF16)
    for tb in range(N_BLK):
        acc = _dot_nt(hin, wout_ref[tb * BLK:(tb + 1) * BLK, :])
        for sb in range(tb + 1):
            acc = acc + _dot(x_s[:, sb * BLK:(sb + 1) * BLK], blk_ref[tb - sb])
        yp_ref[:, tb * BLK:(tb + 1) * BLK] = acc[:mp].astype(yp_ref.dtype)
        ys_ref[:, tb * BLK:(tb + 1) * BLK] = acc[mp:].astype(ys_ref.dtype)


def _ssm(xgp, xgs, blk, wst, wout, a2, h0s, bp, nchunk, bs):
    g = N_GROUPS
    mp = bp * nchunk
    m = mp + bs
    kdim = CHUNK * GROUP_CH
    xspec = lambda r: pl.BlockSpec((None, r, kdim), lambda i: (i, 0, 0))
    sp, ss = pl.pallas_call(
        _ssm_state_kernel,
        grid=(g,),
        in_specs=[xspec(mp), xspec(bs),
                  pl.BlockSpec((None, kdim, 2 * STATE), lambda i: (i, 0, 0))],
        out_specs=[pl.BlockSpec((mp, 2 * STATE), lambda i: (i, 0)),
                   pl.BlockSpec((bs, 2 * STATE), lambda i: (i, 0))],
        out_shape=[jax.ShapeDtypeStruct((g * mp, 2 * STATE), F32),
                   jax.ShapeDtypeStruct((g * bs, 2 * STATE), F32)],
        compiler_params=_params("arbitrary"),
        name="ssm_state",
    )(xgp, xgs, wst)

    arp = jnp.repeat(a2[:, 0, :], bp, axis=0)
    aip = jnp.repeat(a2[:, 1, :], bp, axis=0)
    ars = jnp.repeat(a2[:, 0, :], bs, axis=0)
    ais = jnp.repeat(a2[:, 1, :], bs, axis=0)
    full = lambda r: pl.BlockSpec((r, 2 * STATE), lambda i: (0, 0))
    hin, hfp, hfs = pl.pallas_call(
        functools.partial(_ssm_scan_kernel, nchunk=nchunk, rows=g * bp),
        grid=(1,),
        in_specs=[full(g * mp), full(g * bs), full(g * bp), full(g * bp), full(g * bs), full(g * bs),
                  full(g * bs)],
        out_specs=[full(g * mp), full(g * bp), full(g * bs)],
        out_shape=[jax.ShapeDtypeStruct((g * mp, 2 * STATE), F32),
                   jax.ShapeDtypeStruct((g * bp, 2 * STATE), F32),
                   jax.ShapeDtypeStruct((g * bs, 2 * STATE), F32)],
        compiler_params=_params("arbitrary"),
        name="ssm_scan",
    )(sp, ss, arp, aip, ars, ais, h0s)

    yp, ys = pl.pallas_call(
        _ssm_out_kernel,
        grid=(g,),
        in_specs=[xspec(mp), xspec(bs),
                  pl.BlockSpec((None, N_BLK, BLK, BLK), lambda i: (i, 0, 0, 0)),
                  pl.BlockSpec((None, kdim, 2 * STATE), lambda i: (i, 0, 0)),
                  pl.BlockSpec((mp, 2 * STATE), lambda i: (i, 0)),
                  pl.BlockSpec((bs, 2 * STATE), lambda i: (i, 0))],
        out_specs=[xspec(mp), xspec(bs)],
        out_shape=[jax.ShapeDtypeStruct((g, mp, kdim), BF16), jax.ShapeDtypeStruct((g, bs, kdim), BF16)],
        scratch_shapes=[pltpu.VMEM((m, kdim), BF16)],
        compiler_params=_params("arbitrary"),
        name="ssm_out",
    )(xgp, xgs, blk, wout, hin, h0s)
    return yp, ys, hfp, hfs


def _attn_kernel(sinks_ref, q_ref, kp_ref, kc_ref, vp_ref, vc_ref, o_ref, *, qp, kw, masked):
    i = pl.program_id(1)
    n_units = q_ref.shape[0] // qp
    gw = Q_PER_KV * HEAD_DIM
    half = HEAD_DIM
    nl = Q_PER_KV * qp

    def duplicated(prev_ref, cur_ref):
        a = jnp.concatenate([prev_ref[...], cur_ref[...]], axis=0)
        lo = lax.broadcasted_iota(jnp.int32, (a.shape[0], 2 * half), 1) < half
        out = []
        for c in range(N_KV // 2):
            col = a[:, c * 2 * half:(c + 1) * 2 * half]
            rot = pltpu.roll(col, half, axis=1)
            out.append(jnp.where(lo, col, rot).astype(BF16))
            out.append(jnp.where(lo, rot, col).astype(BF16))
        return out

    kdup = duplicated(kp_ref, kc_ref)
    vdup = duplicated(vp_ref, vc_ref)
    pair_lo = lax.broadcasted_iota(jnp.int32, (qp, 2 * half), 1) < half
    lane_g = lax.broadcasted_iota(jnp.int32, (1, nl), 1) // qp

    for u in range(n_units):
        rows = slice(u * qp, (u + 1) * qp)
        win = slice(u * qp, u * qp + kw)
        if masked:
            kc = lax.broadcasted_iota(jnp.int32, (kw, nl), 0) // CHUNK
            qc = (lax.broadcasted_iota(jnp.int32, (kw, nl), 1) % qp) // CHUNK
            first = WINDOW // CHUNK - (i * n_units + u) * (qp // CHUNK)
            mask = (kc >= qc) & (kc <= qc + WINDOW // CHUNK) & (kc >= first)
        for kv in range(N_KV):
            qm = []
            for g in range(Q_PER_KV):
                pair = q_ref[rows, kv * gw + (g // 2) * 2 * half:kv * gw + (g // 2 + 1) * 2 * half]
                keep = pair_lo if g % 2 == 0 else jnp.logical_not(pair_lo)
                qm.append(jnp.where(keep, pair, jnp.zeros_like(pair)))
            s = _dot_nt(kdup[kv][win], jnp.concatenate(qm, axis=0))
            if masked:
                s = jnp.where(mask, s, NEG)
            sink = jnp.zeros((1, nl), F32)
            for g in range(Q_PER_KV):
                sink = jnp.where(lane_g == g, sinks_ref[kv * Q_PER_KV + g], sink)
            m = jnp.maximum(jnp.max(s, axis=0, keepdims=True), sink)
            p = jnp.exp(s - m)
            den = jnp.sum(p, axis=0, keepdims=True) + jnp.exp(sink - m)
            pn = (p * (1.0 / den)).astype(BF16)
            o = lax.dot_general(pn, vdup[kv][win], (((0,), (0,)), ((), ())),
                                preferred_element_type=F32)
            for h in range(Q_PER_KV // 2):
                both = jnp.where(pair_lo, o[2 * h * qp:(2 * h + 1) * qp], o[(2 * h + 1) * qp:(2 * h + 2) * qp])
                o_ref[rows, kv * gw + h * 2 * half:kv * gw + (h + 1) * 2 * half] = both.astype(o_ref.dtype)


def _attention(sinks, q, k_prev, k_cur, v_prev, v_cur, qt, qp, masked):
    b, l, _ = q.shape
    per = qt // WINDOW
    kw = WINDOW + qp
    if masked:
        prev_map = lambda bi, i: (bi, jnp.maximum(i * per - 1, 0), 0)
    else:
        prev_map = lambda bi, i: (bi, 0, 0)
    cur = lambda w: pl.BlockSpec((None, qt, w), lambda bi, i: (bi, i, 0))
    prev = pl.BlockSpec((None, WINDOW, D_KV), prev_map)
    return pl.pallas_call(
        functools.partial(_attn_kernel, qp=qp, kw=kw, masked=masked),
        grid=(b, l // qt),
        in_specs=[pl.BlockSpec(memory_space=pltpu.SMEM), cur(D_ATTN), prev, cur(D_KV), prev, cur(D_KV)],
        out_specs=cur(D_ATTN),
        out_shape=jax.ShapeDtypeStruct((b, l, D_ATTN), BF16),
        compiler_params=_params("arbitrary", "arbitrary"),
        name="attention",
    )(sinks, q, k_prev, k_cur, v_prev, v_cur)


def _mix_kernel(x_ref, y_ref, o_ref, n1_ref, wglu_ref, wga_ref, wgb_ref, wbs_ref, wba_ref, wout_ref,
                h_ref, xn_s, ys_s):
    j = pl.program_id(1)

    @pl.when(j == 0)
    def _():
        x = x_ref[...]
        xn_s[...] = _rms(x, n1_ref[...]).astype(BF16)
        ya = jax.nn.gelu(y_ref[...].astype(F32))
        ys_s[...] = (ya * jax.nn.sigmoid(_dot(ya.astype(BF16), wglu_ref[...]))).astype(BF16)
        h_ref[...] = x

    xn = xn_s[...]
    ga = jax.nn.sigmoid(_dot(xn, wga_ref[...]))
    gb = jax.nn.sigmoid(_dot(xn, wgb_ref[...]))
    mixed = ga * _dot(ys_s[...], wbs_ref[...]) + gb * _dot(o_ref[...], wba_ref[...])
    h_ref[...] += _dot(mixed.astype(BF16), wout_ref[...])


def _mix(x2d, y2d, o2d, norm1, wglu, wgate, wbs, wba, wout, tm, tn):
    t = x2d.shape[0]
    nj = D_MODEL // tn
    tok = lambda w: pl.BlockSpec((tm, w), lambda i, j: (i, 0))
    return pl.pallas_call(
        _mix_kernel,
        grid=(t // tm, nj),
        in_specs=[tok(D_MODEL), tok(D_SSM), tok(D_ATTN),
                  pl.BlockSpec((1, D_MODEL), lambda i, j: (0, 0)),
                  pl.BlockSpec((D_SSM, D_SSM), lambda i, j: (0, 0)),
                  pl.BlockSpec((D_MODEL, tn), lambda i, j: (0, j)),
                  pl.BlockSpec((D_MODEL, tn), lambda i, j: (0, nj + j)),
                  pl.BlockSpec((D_SSM, tn), lambda i, j: (0, j)),
                  pl.BlockSpec((D_ATTN, tn), lambda i, j: (0, j)),
                  pl.BlockSpec((tn, D_MODEL), lambda i, j: (j, 0))],
        out_specs=tok(D_MODEL),
        out_shape=jax.ShapeDtypeStruct((t, D_MODEL), F32),
        scratch_shapes=[pltpu.VMEM((tm, D_MODEL), BF16), pltpu.VMEM((tm, D_SSM), BF16)],
        compiler_params=_params("arbitrary", "arbitrary"),
        name="mix",
    )(x2d, y2d, o2d, norm1, wglu, wgate, wgate, wbs, wba, wout)


def _ffn_kernel(h_ref, n2_ref, wg_ref, wu_ref, wd_ref, out_ref, hn_s):
    j = pl.program_id(1)

    @pl.when(j == 0)
    def _():
        h = h_ref[...]
        hn_s[...] = _rms(h, n2_ref[...]).astype(BF16)
        out_ref[...] = h

    hn = hn_s[...]
    act = jax.nn.silu(_dot(hn, wg_ref[...])) * _dot(hn, wu_ref[...])
    out_ref[...] += _dot(act.astype(BF16), wd_ref[...])


def _ffn(h2d, norm2, wg, wu, wd, tm, tf):
    t = h2d.shape[0]
    d_ff = wg.shape[1]
    tok = pl.BlockSpec((tm, D_MODEL), lambda i, j: (i, 0))
    return pl.pallas_call(
        _ffn_kernel,
        grid=(t // tm, d_ff // tf),
        in_specs=[tok, pl.BlockSpec((1, D_MODEL), lambda i, j: (0, 0)),
                  pl.BlockSpec((D_MODEL, tf), lambda i, j: (0, j)),
                  pl.BlockSpec((D_MODEL, tf), lambda i, j: (0, j)),
                  pl.BlockSpec((tf, D_MODEL), lambda i, j: (j, 0))],
        out_specs=tok,
        out_shape=jax.ShapeDtypeStruct((t, D_MODEL), F32),
        scratch_shapes=[pltpu.VMEM((tm, D_MODEL), BF16)],
        compiler_params=_params("arbitrary", "arbitrary"),
        name="ffn",
    )(h2d, norm2, wg, wu, wd)


def _to_groups(u2d, rows):
    t = u2d.shape[0]
    return u2d.reshape(t * GROUP_CH, N_GROUPS).T.reshape(N_GROUPS, rows, CHUNK * GROUP_CH)


def _from_groups(yg):
    g, rows, kdim = yg.shape
    return yg.reshape(g, rows * kdim).T.reshape(rows * CHUNK, D_SSM)


def kernel(x_prompt, x_sample, cache_k, cache_v, state_ssm_re, state_ssm_im, norm1, w_in, q_norm, k_norm,
           sinks, ssm_a_re, ssm_a_im, ssm_log_dt, ssm_b_re, ssm_b_im, ssm_c_re, ssm_c_im, ssm_d, w_glu,
           w_br_ssm, w_br_attn, w_gate, w_out, norm2, w_ffn_gate, w_ffn_up, w_ffn_down):
    bp, lp, _ = x_prompt.shape
    bs, ls, _ = x_sample.shape
    ncp = lp // CHUNK
    assert ls == CHUNK and cache_k.shape[2] == WINDOW and norm1.shape[0] == 1
    l = 0

    perm = lambda a, axis: jnp.swapaxes(
        a.reshape(a.shape[:axis] + (N_GROUPS, GROUP_CH) + a.shape[axis + 1:]), axis, axis + 1
    ).reshape(a.shape)
    w_in_l = w_in[l]
    w_in_b = jnp.concatenate([perm(w_in_l[:, :D_SSM], 1), w_in_l[:, D_SSM:]], axis=1).astype(BF16)
    wglu_b = perm(perm(w_glu[l], 0), 1).astype(BF16)
    wbs_b = perm(w_br_ssm[l], 0).astype(BF16)
    wba_b = w_br_attn[l].astype(BF16)
    wgate_b = w_gate[l].astype(BF16)
    wout_b = w_out[l].astype(BF16)
    wfg_b = w_ffn_gate[l].astype(BF16)
    wfu_b = w_ffn_up[l].astype(BF16)
    wfd_b = w_ffn_down[l].astype(BF16)
    n1 = norm1[l].astype(F32).reshape(1, D_MODEL)
    n2 = norm2[l].astype(F32).reshape(1, D_MODEL)
    qg = jnp.tile(q_norm[l].astype(F32), 2).reshape(1, 128)
    kg = jnp.tile(k_norm[l].astype(F32), 2).reshape(1, 128)
    lane_head = jnp.arange(128) // HEAD_DIM
    seg = (lane_head[:, None] == lane_head[None, :]).astype(BF16)
    sink = sinks[l].astype(F32)

    blk, wst, wout_t, a2 = _ssm_prep(ssm_a_re[l], ssm_a_im[l], ssm_log_dt[l], ssm_b_re[l], ssm_b_im[l],
                                     ssm_c_re[l], ssm_c_im[l], ssm_d[l])

    xp = x_prompt.astype(F32).reshape(bp * lp, D_MODEL)
    xs = x_sample.astype(F32).reshape(bs * ls, D_MODEL)
    tm = 512
    up, qp, kp, vp = _in_proj(xp, jnp.arange(lp), tm, n1, w_in_b, seg, qg, kg)
    us, qs, ks, vs = _in_proj(xs, jnp.tile(PAST_LEN + jnp.arange(ls), tm // ls), tm, n1, w_in_b, seg, qg, kg)

    h0s = jnp.concatenate([state_ssm_re[l], state_ssm_im[l]], axis=-1).astype(F32)
    h0s = jnp.swapaxes(h0s, 0, 1).reshape(N_GROUPS * bs, 2 * STATE)
    ygp, ygs, hfp, hfs = _ssm(_to_groups(up, bp * ncp), _to_groups(us, bs), blk, wst, wout_t, a2, h0s,
                              bp, ncp, bs)
    y_p = _from_groups(ygp)
    y_s = _from_groups(ygs)

    def states(hf, b):
        hf = jnp.swapaxes(hf.reshape(N_GROUPS, b, 2 * STATE), 0, 1)
        return hf[None, :, :, :STATE], hf[None, :, :, STATE:]

    kp3, vp3 = kp.reshape(bp, lp, D_KV), vp.reshape(bp, lp, D_KV)
    o_p = _attention(sink, qp.reshape(bp, lp, D_ATTN), kp3, kp3, vp3, vp3, 256, 2 * CHUNK, True)
    ck = cache_k[l].astype(F32).reshape(bs, WINDOW, D_KV)
    cv = cache_v[l].astype(F32).reshape(bs, WINDOW, D_KV)
    ks3, vs3 = ks.reshape(bs, ls, D_KV), vs.reshape(bs, ls, D_KV)
    o_s = _attention(sink, qs.reshape(bs, ls, D_ATTN), ck, ks3, cv, vs3, ls, ls, False)

    outs = []
    for x2d, y2d, o3d in ((xp, y_p, o_p), (xs, y_s, o_s)):
        h = _mix(x2d, y2d, o3d.reshape(-1, D_ATTN), n1, wglu_b, wgate_b, wbs_b, wba_b, wout_b, 512, 512)
        outs.append(_ffn(h, n2, wfg_b, wfu_b, wfd_b, 1024, 512))

    win = lambda a, b, n: a.reshape(b, n, N_KV, HEAD_DIM)[None]
    k_win_p = win(kp3[:, lp - WINDOW:], bp, WINDOW)
    v_win_p = win(vp3[:, lp - WINDOW:], bp, WINDOW)
    k_win_s = win(jnp.concatenate([ck, ks3], axis=1)[:, ls:], bs, WINDOW)
    v_win_s = win(jnp.concatenate([cv, vs3], axis=1)[:, ls:], bs, WINDOW)
    re_p, im_p = states(hfp, bp)
    re_s, im_s = states(hfs, bs)
    dt = x_prompt.dtype
    return (outs[0].reshape(bp, lp, D_MODEL).astype(dt), outs[1].reshape(bs, ls, D_MODEL).astype(dt),
            k_win_p.astype(dt), v_win_p.astype(dt), re_p.astype(dt), im_p.astype(dt),
            k_win_s.astype(dt), v_win_s.astype(dt), re_s.astype(dt), im_s.astype(dt))
```

```python
import functools
import math

import jax
import jax.numpy as jnp
import numpy as np
from jax import lax
from jax.experimental import pallas as pl
from jax.experimental.pallas import tpu as pltpu

D_MODEL = 2048
CHUNK = 64
D_SSM = 1024
GROUP_CH = 16
N_GROUPS = 64
STATE = 64
HEAD_DIM = 64
N_HEADS = 16
N_KV = 4
Q_PER_KV = 4
D_ATTN = 1024
D_KV = 256
WINDOW = 128
ROT_DIM = 16
ROPE_THETA = 500000.0
PAST_LEN = 1024
EPS = 1e-6
NEG = -1e30

BLK = 256
PITCH = CHUNK + 8
STEPS_PER_BLK = BLK // GROUP_CH
N_BLK = CHUNK // STEPS_PER_BLK
V7X_VMEM_LIMIT = 56 * 1024 * 1024

F32 = jnp.float32
BF16 = jnp.bfloat16


def _dot(a, b):
    return jnp.dot(a, b, preferred_element_type=F32)


def _dot_nt(a, b, precision=None):
    return lax.dot_general(a, b, (((1,), (1,)), ((), ())), preferred_element_type=F32,
                           precision=precision)


def _params(*sem):
    return pltpu.CompilerParams(dimension_semantics=sem, vmem_limit_bytes=V7X_VMEM_LIMIT)


def _ssm_prep_kernel(are_ref, aim_ref, ldt_ref, bt_ref, bts_ref, c_ref, cs_ref, d_ref,
                     blk_ref, wst_ref, wout_ref, a2_ref, l_s, z_s, w_s):
    lane = lax.broadcasted_iota(jnp.int32, (1, 2 * STATE), 1)
    sg = jnp.where(lane < STATE, 1.0, -1.0).astype(F32)
    a_re = are_ref[...]
    a_im = aim_ref[...]
    dt = jnp.exp(ldt_ref[...])
    lre = a_re * dt
    lim = a_im * dt

    def cpow(e):
        mag = jnp.exp(e * lre)
        ang = e * lim
        return mag * jnp.cos(ang), mag * jnp.sin(ang)

    def cmul(ar, ai, br, bi):
        return ar * br - ai * bi, ar * bi + ai * br

    col = lax.broadcasted_iota(jnp.int32, (STEPS_PER_BLK, 1), 0).astype(F32)
    t_r, t_i = cpow(col)
    r_r, r_i = cpow(float(STEPS_PER_BLK - 1) - col)
    h_r, h_i = cpow(float(STEPS_PER_BLK) * col[:8])
    ab_r, ab_i = t_r[1:2], t_i[1:2]

    fr, fi = ab_r - 1.0, ab_i
    den = a_re * a_re + a_im * a_im
    cr = (fr * a_re + fi * a_im) / den
    ci = (fi * a_re - fr * a_im) / den
    bb = cr * bt_ref[...] - (ci * sg) * bts_ref[...]
    bbs = cr * bts_ref[...] + (ci * sg) * bt_ref[...]
    cc = c_ref[...]
    ccs = cs_ref[...]

    def outer_rows(pr, pi, m, ms, dst, row0, conj):
        pa = pr * sg if conj else pr
        pb = -pi if conj else -(pi * sg)
        for r in range(pr.shape[0]):
            z = pa[r:r + 1, :] * m + pb[r:r + 1, :] * ms
            dst[row0 + r * GROUP_CH:row0 + (r + 1) * GROUP_CH, :] = z

    e_r, e_i = cmul(ab_r, ab_i, r_r, r_i)
    outer_rows(e_r, e_i, bb, bbs, l_s, 0, False)
    l2 = l_s[...].astype(BF16)

    for d in range(N_BLK - 1):
        e_r, e_i = cmul(h_r[d:d + 1], h_i[d:d + 1], t_r, t_i)
        outer_rows(e_r, e_i, cc, ccs, z_s, 0, True)
        blk_ref[d + 1] = _dot_nt(z_s[...].astype(BF16), l2).astype(BF16)

    outer_rows(r_r, r_i, bb, bbs, z_s, 0, False)
    kr = _dot_nt((cc * sg).astype(BF16), z_s[...].astype(BF16))
    rows = lax.broadcasted_iota(jnp.int32, (GROUP_CH, BLK), 0)
    lanes = lax.broadcasted_iota(jnp.int32, (GROUP_CH, BLK), 1)
    kr = kr + jnp.where(lanes == rows + (BLK - GROUP_CH), d_ref[...], 0.0)
    for t0 in range(STEPS_PER_BLK):
        sh = GROUP_CH * (STEPS_PER_BLK - 1 - t0)
        piece = kr if sh == 0 else jnp.where(lanes < BLK - sh, pltpu.roll(kr, BLK - sh, axis=1), 0.0)
        blk_ref[0, t0 * GROUP_CH:(t0 + 1) * GROUP_CH, :] = piece.astype(BF16)

    for sb in range(N_BLK):
        m = N_BLK - 1 - sb
        e_r, e_i = cmul(h_r[m:m + 1], h_i[m:m + 1], r_r, r_i)
        outer_rows(e_r, e_i, bb, bbs, w_s, sb * BLK, False)
    wst_ref[...] = w_s[...].astype(BF16)

    for tb in range(N_BLK):
        e_r, e_i = cmul(*cmul(ab_r, ab_i, h_r[tb:tb + 1], h_i[tb:tb + 1]), t_r, t_i)
        outer_rows(e_r, e_i, cc, ccs, w_s, tb * BLK, True)
    wout_ref[...] = w_s[...].astype(BF16)

    a2_ref[0:1, :] = h_r[N_BLK:N_BLK + 1]
    a2_ref[1:2, :] = -(h_i[N_BLK:N_BLK + 1] * sg)


def _ssm_prep(a_re, a_im, log_dt, b_re, b_im, c_re, c_im, d_skip):
    g = N_GROUPS
    f = lambda a: a.astype(F32)
    dup = lambda a: jnp.concatenate([f(a), f(a)], axis=-1).reshape(g, 1, 2 * STATE)
    bt_re = jnp.swapaxes(f(b_re), 1, 2)
    bt_im = jnp.swapaxes(f(b_im), 1, 2)
    bt = jnp.concatenate([bt_re, bt_im], axis=-1)
    bts = jnp.concatenate([bt_im, bt_re], axis=-1)
    cc = jnp.concatenate([f(c_re), f(c_im)], axis=-1)
    ccs = jnp.concatenate([f(c_im), f(c_re)], axis=-1)
    d_pad = jnp.pad(f(d_skip).reshape(g, 1, GROUP_CH), ((0, 0), (0, 0), (BLK - GROUP_CH, 0)))
    vec = lambda n: pl.BlockSpec((None, 1, n), lambda i: (i, 0, 0))
    mat = pl.BlockSpec((None, GROUP_CH, 2 * STATE), lambda i: (i, 0, 0))
    kdim = CHUNK * GROUP_CH
    return pl.pallas_call(
        _ssm_prep_kernel,
        grid=(g,),
        in_specs=[vec(2 * STATE), vec(2 * STATE), vec(1), mat, mat, mat, mat, vec(BLK)],
        out_specs=[pl.BlockSpec((None, N_BLK, BLK, BLK), lambda i: (i, 0, 0, 0)),
                   pl.BlockSpec((None, kdim, 2 * STATE), lambda i: (i, 0, 0)),
                   pl.BlockSpec((None, kdim, 2 * STATE), lambda i: (i, 0, 0)),
                   pl.BlockSpec((None, 2, 2 * STATE), lambda i: (i, 0, 0))],
        out_shape=[jax.ShapeDtypeStruct((g, N_BLK, BLK, BLK), BF16),
                   jax.ShapeDtypeStruct((g, kdim, 2 * STATE), BF16),
                   jax.ShapeDtypeStruct((g, kdim, 2 * STATE), BF16),
                   jax.ShapeDtypeStruct((g, 2, 2 * STATE), F32)],
        scratch_shapes=[pltpu.VMEM((BLK, 2 * STATE), F32), pltpu.VMEM((BLK, 2 * STATE), F32),
                        pltpu.VMEM((kdim, 2 * STATE), F32)],
        compiler_params=_params("arbitrary"),
        name="ssm_prep",
    )(dup(a_re), dup(a_im), f(log_dt).reshape(g, 1, 1), bt, bts, cc, ccs, d_pad)


def _rms(x, g):
    return x * lax.rsqrt(jnp.mean(x * x, axis=-1, keepdims=True) + EPS) * g


def _in_proj_kernel(x_ref, n1_ref, w_ref, seg_ref, qg_ref, kg_ref, rc_ref, rm_ref, rp_ref,
                    u_ref, q_ref, k_ref, v_ref):
    xn = _rms(x_ref[...], n1_ref[...]).astype(BF16)
    u = _dot(xn, w_ref[:, 0:D_SSM])
    for c in range(u.shape[0] // CHUNK):
        u_ref[c * PITCH:c * PITCH + CHUNK, :] = u[c * CHUNK:(c + 1) * CHUNK]
        u_ref[c * PITCH + CHUNK:(c + 1) * PITCH, :] = jnp.zeros((PITCH - CHUNK, D_SSM), F32)
    v_ref[...] = _dot(xn, w_ref[:, D_SSM + D_ATTN + D_KV:])
    rc = rc_ref[...]
    rm = rm_ref[...]
    rp = rp_ref[...]

    def norm_rope(z, gain, width, scale):
        sq = (z * z).astype(BF16)
        outs = []
        for cb in range(width // 128):
            sl = slice(cb * 128, (cb + 1) * 128)
            ms = _dot(sq[:, sl], seg_ref[...]) * (1.0 / HEAD_DIM)
            zn = z[:, sl] * lax.rsqrt(ms + EPS) * gain
            rot = zn * rc + pltpu.roll(zn, 128 - ROT_DIM // 2, axis=1) * rm \
                + pltpu.roll(zn, ROT_DIM // 2, axis=1) * rp
            outs.append(rot * scale)
        return outs

    q = _dot(xn, w_ref[:, D_SSM:D_SSM + D_ATTN])
    for cb, blk in enumerate(norm_rope(q, qg_ref[...], D_ATTN, HEAD_DIM ** -0.5)):
        q_ref[:, cb * 128:(cb + 1) * 128] = blk.astype(q_ref.dtype)
    k = _dot(xn, w_ref[:, D_SSM + D_ATTN:D_SSM + D_ATTN + D_KV])
    for cb, blk in enumerate(norm_rope(k, kg_ref[...], D_KV, 1.0)):
        k_ref[:, cb * 128:(cb + 1) * 128] = blk


def _rope_tables(pos):
    half = ROT_DIM // 2
    inv = ROPE_THETA ** (-np.arange(half, dtype=np.float64) * 2.0 / ROT_DIM)
    ang = pos.astype(np.float64)[:, None] * inv[None, :]
    cos, sin = np.cos(ang), np.sin(ang)
    n = pos.shape[0]
    ones = np.ones((n, HEAD_DIM - ROT_DIM))
    zeros = np.zeros((n, HEAD_DIM - ROT_DIM))
    zh = np.zeros((n, half))
    rc = np.concatenate([cos, cos, ones], axis=1)
    rm = np.concatenate([-sin, zh, zeros], axis=1)
    rp = np.concatenate([zh, sin, zeros], axis=1)
    two = lambda t: jnp.asarray(np.concatenate([t, t], axis=1), F32)
    return two(rc), two(rm), two(rp)


def _in_proj(x2d, pos, tm, norm1, w_in_b, seg, qg, kg):
    t = x2d.shape[0]
    rc, rm, rp = _rope_tables(pos)
    nrope = pos.shape[0] // tm
    rope_spec = pl.BlockSpec((tm, 128), lambda i: (i % nrope, 0))
    const = lambda shape: pl.BlockSpec(shape, lambda i: (0, 0))
    tok = lambda w: pl.BlockSpec((tm, w), lambda i: (i, 0))
    return pl.pallas_call(
        _in_proj_kernel,
        grid=(t // tm,),
        in_specs=[tok(D_MODEL), const((1, D_MODEL)), const(w_in_b.shape), const((128, 128)),
                  const((1, 128)), const((1, 128)), rope_spec, rope_spec, rope_spec],
        out_specs=[pl.BlockSpec((tm // CHUNK * PITCH, D_SSM), lambda i: (i, 0)), tok(D_ATTN), tok(D_KV), tok(D_KV)],
        out_shape=[jax.ShapeDtypeStruct((t // CHUNK * PITCH, D_SSM), F32), jax.ShapeDtypeStruct((t, D_ATTN), BF16),
                   jax.ShapeDtypeStruct((t, D_KV), F32), jax.ShapeDtypeStruct((t, D_KV), F32)],
        compiler_params=_params("arbitrary"),
        name="in_proj",
    )(x2d, norm1, w_in_b, seg, qg, kg, rc, rm, rp)


def _ssm_state_kernel(up_ref, us_ref, wst_ref, xtp_ref, xts_ref, sp_ref, ss_ref, *, mp, bs):
    ng = xtp_ref.shape[0]
    pad = jnp.zeros((128 - bs, 128), F32)
    for s in range(CHUNK):
        vp = up_ref[pl.ds(s, mp, stride=PITCH), :].T.astype(BF16)
        vs = jnp.concatenate([us_ref[pl.ds(s, bs, stride=PITCH), :], pad], axis=0).T.astype(BF16)
        for g in range(ng):
            xtp_ref[g, s * GROUP_CH:(s + 1) * GROUP_CH, :] = vp[g * GROUP_CH:(g + 1) * GROUP_CH, :]
            xts_ref[g, s * GROUP_CH:(s + 1) * GROUP_CH, :] = vs[g * GROUP_CH:(g + 1) * GROUP_CH, :]
    tn = (((0,), (0,)), ((), ()))
    for g in range(ng):
        st = lax.dot_general(wst_ref[g], xtp_ref[g], tn, preferred_element_type=F32)
        sp_ref[g * mp:(g + 1) * mp, :] = st.T
        st = lax.dot_general(wst_ref[g], xts_ref[g], tn, preferred_element_type=F32)
        ss_ref[g * bs:(g + 1) * bs, :] = st.T[:bs]


def _cmul_add(h, ar2, ai2, s):
    return h * ar2 + pltpu.roll(h, STATE, axis=1) * ai2 + s


def _ssm_scan_kernel(sp_ref, ss_ref, arp_ref, aip_ref, ars_ref, ais_ref, h0s_ref,
                     hin_ref, hfp_ref, hfs_ref, *, nchunk, rows):
    ar = arp_ref[...]
    ai = aip_ref[...]
    h = jnp.zeros((rows, 2 * STATE), F32)
    for k in range(nchunk):
        hin_ref[pl.ds(k, rows, stride=nchunk), :] = h
        h = _cmul_add(h, ar, ai, sp_ref[pl.ds(k, rows, stride=nchunk), :])
    hfp_ref[...] = h
    hfs_ref[...] = _cmul_add(h0s_ref[...], ars_ref[...], ais_ref[...], ss_ref[...])


def _ssm_out_kernel(xtp_ref, xts_ref, blk_ref, wout_ref, hinp_ref, h0s_ref, yp_ref, ys_ref,
                    stp_s, sts_s, *, mp, bs):
    ng = xtp_ref.shape[0]
    pad = jnp.zeros((128 - bs, 2 * STATE), F32)
    hp = [hinp_ref[g * mp:(g + 1) * mp, :].astype(BF16) for g in range(ng)]
    hs = [jnp.concatenate([h0s_ref[g * bs:(g + 1) * bs, :], pad], axis=0).astype(BF16) for g in range(ng)]
    for tb in range(N_BLK):
        for g in range(ng):
            w = wout_ref[g, tb * BLK:(tb + 1) * BLK, :]
            accp = _dot_nt(w, hp[g])
            accs = _dot_nt(w, hs[g])
            for sb in range(tb + 1):
                b = blk_ref[g, tb - sb]
                accp = accp + _dot(b, xtp_ref[g, sb * BLK:(sb + 1) * BLK, :])
                accs = accs + _dot(b, xts_ref[g, sb * BLK:(sb + 1) * BLK, :])
            for t0 in range(STEPS_PER_BLK):
                stp_s[t0, g * GROUP_CH:(g + 1) * GROUP_CH, :] = accp[t0 * GROUP_CH:(t0 + 1) * GROUP_CH, :]
                sts_s[t0, g * GROUP_CH:(g + 1) * GROUP_CH, :] = accs[t0 * GROUP_CH:(t0 + 1) * GROUP_CH, :]
        for t0 in range(STEPS_PER_BLK):
            t = tb * STEPS_PER_BLK + t0
            yp_ref[pl.ds(t, mp, stride=PITCH), :] = stp_s[t0].T
            ys_ref[pl.ds(t, bs, stride=PITCH), :] = sts_s[t0].T[:bs]
    for r in range(CHUNK, PITCH):
        yp_ref[pl.ds(r, mp, stride=PITCH), :] = jnp.zeros((mp, 128), F32)
        ys_ref[pl.ds(r, bs, stride=PITCH), :] = jnp.zeros((bs, 128), F32)


def _ssm(up, us, blk, wst, wout, a2, h0s, bp, nchunk, bs):
    g = N_GROUPS
    ng = 128 // GROUP_CH
    mp = bp * nchunk
    kdim = CHUNK * GROUP_CH
    tok = lambda a: pl.BlockSpec((a.shape[0], 128), lambda i: (0, i))
    grp = lambda *tail: pl.BlockSpec((ng,) + tail, lambda i: (i,) + (0,) * len(tail))
    rows = lambda r: pl.BlockSpec((ng * r, 2 * STATE), lambda i: (i, 0))
    xtp, xts, sp, ss = pl.pallas_call(
        functools.partial(_ssm_state_kernel, mp=mp, bs=bs),
        grid=(g // ng,),
        in_specs=[tok(up), tok(us), grp(kdim, 2 * STATE)],
        out_specs=[grp(kdim, mp), grp(kdim, 128), rows(mp), rows(bs)],
        out_shape=[jax.ShapeDtypeStruct((g, kdim, mp), BF16), jax.ShapeDtypeStruct((g, kdim, 128), BF16),
                   jax.ShapeDtypeStruct((g * mp, 2 * STATE), F32),
                   jax.ShapeDtypeStruct((g * bs, 2 * STATE), F32)],
        compiler_params=_params("arbitrary"),
        name="ssm_state",
    )(up, us, wst)

    arp = jnp.repeat(a2[:, 0, :], bp, axis=0)
    aip = jnp.repeat(a2[:, 1, :], bp, axis=0)
    ars = jnp.repeat(a2[:, 0, :], bs, axis=0)
    ais = jnp.repeat(a2[:, 1, :], bs, axis=0)
    full = lambda r: pl.BlockSpec((r, 2 * STATE), lambda i: (0, 0))
    hin, hfp, hfs = pl.pallas_call(
        functools.partial(_ssm_scan_kernel, nchunk=nchunk, rows=g * bp),
        grid=(1,),
        in_specs=[full(g * mp), full(g * bs), full(g * bp), full(g * bp), full(g * bs), full(g * bs),
                  full(g * bs)],
        out_specs=[full(g * mp), full(g * bp), full(g * bs)],
        out_shape=[jax.ShapeDtypeStruct((g * mp, 2 * STATE), F32),
                   jax.ShapeDtypeStruct((g * bp, 2 * STATE), F32),
                   jax.ShapeDtypeStruct((g * bs, 2 * STATE), F32)],
        compiler_params=_params("arbitrary"),
        name="ssm_scan",
    )(sp, ss, arp, aip, ars, ais, h0s)

    yp, ys = pl.pallas_call(
        functools.partial(_ssm_out_kernel, mp=mp, bs=bs),
        grid=(g // ng,),
        in_specs=[grp(kdim, mp), grp(kdim, 128), grp(N_BLK, BLK, BLK), grp(kdim, 2 * STATE), rows(mp), rows(bs)],
        out_specs=[tok(up), tok(us)],
        out_shape=[jax.ShapeDtypeStruct(up.shape, F32), jax.ShapeDtypeStruct(us.shape, F32)],
        scratch_shapes=[pltpu.VMEM((STEPS_PER_BLK, 128, mp), F32), pltpu.VMEM((STEPS_PER_BLK, 128, 128), F32)],
        compiler_params=_params("arbitrary"),
        name="ssm_out",
    )(xtp, xts, blk, wout, hin, h0s)
    return yp, ys, hfp, hfs


def _attn_kernel(sinks_ref, q_ref, kp_ref, kc_ref, vp_ref, vc_ref, o_ref, *, qp, kw, masked):
    i = pl.program_id(1)
    n_units = q_ref.shape[0] // qp
    gw = Q_PER_KV * HEAD_DIM
    half = HEAD_DIM
    nl = Q_PER_KV * qp

    def duplicated(prev_ref, cur_ref):
        a = jnp.concatenate([prev_ref[...], cur_ref[...]], axis=0)
        lo = lax.broadcasted_iota(jnp.int32, (a.shape[0], 2 * half), 1) < half
        out = []
        for c in range(N_KV // 2):
            col = a[:, c * 2 * half:(c + 1) * 2 * half]
            rot = pltpu.roll(col, half, axis=1)
            out.append(jnp.where(lo, col, rot).astype(BF16))
            out.append(jnp.where(lo, rot, col).astype(BF16))
        return out

    kdup = duplicated(kp_ref, kc_ref)
    vdup = duplicated(vp_ref, vc_ref)
    pair_lo = lax.broadcasted_iota(jnp.int32, (qp, 2 * half), 1) < half
    lane_g = lax.broadcasted_iota(jnp.int32, (1, nl), 1) // qp

    for u in range(n_units):
        rows = slice(u * qp, (u + 1) * qp)
        win = slice(u * qp, u * qp + kw)
        if masked:
            kc = lax.broadcasted_iota(jnp.int32, (kw, nl), 0) // CHUNK
            qc = (lax.broadcasted_iota(jnp.int32, (kw, nl), 1) % qp) // CHUNK
            first = WINDOW // CHUNK - (i * n_units + u) * (qp // CHUNK)
            mask = (kc >= qc) & (kc <= qc + WINDOW // CHUNK) & (kc >= first)
        for kv in range(N_KV):
            qm = []
            for g in range(Q_PER_KV):
                pair = q_ref[rows, kv * gw + (g // 2) * 2 * half:kv * gw + (g // 2 + 1) * 2 * half]
                keep = pair_lo if g % 2 == 0 else jnp.logical_not(pair_lo)
                qm.append(jnp.where(keep, pair, jnp.zeros_like(pair)))
            s = _dot_nt(kdup[kv][win], jnp.concatenate(qm, axis=0))
            if masked:
                s = jnp.where(mask, s, NEG)
            sink = jnp.zeros((1, nl), F32)
            for g in range(Q_PER_KV):
                sink = jnp.where(lane_g == g, sinks_ref[kv * Q_PER_KV + g], sink)
            m = jnp.maximum(jnp.max(s, axis=0, keepdims=True), sink)
            p = jnp.exp(s - m)
            den = jnp.sum(p, axis=0, keepdims=True) + jnp.exp(sink - m)
            pn = (p * (1.0 / den)).astype(BF16)
            o = lax.dot_general(pn, vdup[kv][win], (((0,), (0,)), ((), ())),
                                preferred_element_type=F32)
            for h in range(Q_PER_KV // 2):
                both = jnp.where(pair_lo, o[2 * h * qp:(2 * h + 1) * qp], o[(2 * h + 1) * qp:(2 * h + 2) * qp])
                o_ref[rows, kv * gw + h * 2 * half:kv * gw + (h + 1) * 2 * half] = both.astype(o_ref.dtype)


def _attention(sinks, q, k_prev, k_cur, v_prev, v_cur, qt, qp, masked):
    b, l, _ = q.shape
    per = qt // WINDOW
    kw = WINDOW + qp
    if masked:
        prev_map = lambda bi, i: (bi, jnp.maximum(i * per - 1, 0), 0)
    else:
        prev_map = lambda bi, i: (bi, 0, 0)
    cur = lambda w: pl.BlockSpec((None, qt, w), lambda bi, i: (bi, i, 0))
    prev = pl.BlockSpec((None, WINDOW, D_KV), prev_map)
    return pl.pallas_call(
        functools.partial(_attn_kernel, qp=qp, kw=kw, masked=masked),
        grid=(b, l // qt),
        in_specs=[pl.BlockSpec(memory_space=pltpu.SMEM), cur(D_ATTN), prev, cur(D_KV), prev, cur(D_KV)],
        out_specs=cur(D_ATTN),
        out_shape=jax.ShapeDtypeStruct((b, l, D_ATTN), BF16),
        compiler_params=_params("arbitrary", "arbitrary"),
        name="attention",
    )(sinks, q, k_prev, k_cur, v_prev, v_cur)


def _mix_kernel(x_ref, y_ref, o_ref, n1_ref, wglu_ref, wga_ref, wgb_ref, wbs_ref, wba_ref, wout_ref,
                h_ref, xn_s, ys_s):
    j = pl.program_id(1)

    @pl.when(j == 0)
    def _():
        x = x_ref[...]
        xn_s[...] = _rms(x, n1_ref[...]).astype(BF16)
        y = jnp.concatenate([y_ref[c * PITCH:c * PITCH + CHUNK, :] for c in range(x.shape[0] // CHUNK)], axis=0)
        ya = jax.nn.gelu(y)
        ys_s[...] = (ya * jax.nn.sigmoid(_dot(ya.astype(BF16), wglu_ref[...]))).astype(BF16)
        h_ref[...] = x

    xn = xn_s[...]
    ga = jax.nn.sigmoid(_dot(xn, wga_ref[...]))
    gb = jax.nn.sigmoid(_dot(xn, wgb_ref[...]))
    mixed = ga * _dot(ys_s[...], wbs_ref[...]) + gb * _dot(o_ref[...], wba_ref[...])
    h_ref[...] += _dot(mixed.astype(BF16), wout_ref[...])


def _mix(x2d, y2d, o2d, norm1, wglu, wgate, wbs, wba, wout, tm, tn):
    t = x2d.shape[0]
    nj = D_MODEL // tn
    tok = lambda w: pl.BlockSpec((tm, w), lambda i, j: (i, 0))
    return pl.pallas_call(
        _mix_kernel,
        grid=(t // tm, nj),
        in_specs=[tok(D_MODEL), pl.BlockSpec((tm // CHUNK * PITCH, D_SSM), lambda i, j: (i, 0)), tok(D_ATTN),
                  pl.BlockSpec((1, D_MODEL), lambda i, j: (0, 0)),
                  pl.BlockSpec((D_SSM, D_SSM), lambda i, j: (0, 0)),
                  pl.BlockSpec((D_MODEL, tn), lambda i, j: (0, j)),
                  pl.BlockSpec((D_MODEL, tn), lambda i, j: (0, nj + j)),
                  pl.BlockSpec((D_SSM, tn), lambda i, j: (0, j)),
                  pl.BlockSpec((D_ATTN, tn), lambda i, j: (0, j)),
                  pl.BlockSpec((tn, D_MODEL), lambda i, j: (j, 0))],
        out_specs=tok(D_MODEL),
        out_shape=jax.ShapeDtypeStruct((t, D_MODEL), F32),
        scratch_shapes=[pltpu.VMEM((tm, D_MODEL), BF16), pltpu.VMEM((tm, D_SSM), BF16)],
        compiler_params=_params("arbitrary", "arbitrary"),
        name="mix",
    )(x2d, y2d, o2d, norm1, wglu, wgate, wgate, wbs, wba, wout)


def _ffn_kernel(h_ref, n2_ref, wg_ref, wu_ref, wd_ref, out_ref, hn_s):
    j = pl.program_id(1)

    @pl.when(j == 0)
    def _():
        h = h_ref[...]
        hn_s[...] = _rms(h, n2_ref[...]).astype(BF16)
        out_ref[...] = h

    hn = hn_s[...]
    act = jax.nn.silu(_dot(hn, wg_ref[...])) * _dot(hn, wu_ref[...])
    out_ref[...] += _dot(act.astype(BF16), wd_ref[...])


def _ffn(h2d, norm2, wg, wu, wd, tm, tf):
    t = h2d.shape[0]
    d_ff = wg.shape[1]
    tok = pl.BlockSpec((tm, D_MODEL), lambda i, j: (i, 0))
    return pl.pallas_call(
        _ffn_kernel,
        grid=(t // tm, d_ff // tf),
        in_specs=[tok, pl.BlockSpec((1, D_MODEL), lambda i, j: (0, 0)),
                  pl.BlockSpec((D_MODEL, tf), lambda i, j: (0, j)),
                  pl.BlockSpec((D_MODEL, tf), lambda i, j: (0, j)),
                  pl.BlockSpec((tf, D_MODEL), lambda i, j: (j, 0))],
        out_specs=tok,
        out_shape=jax.ShapeDtypeStruct((t, D_MODEL), F32),
        scratch_shapes=[pltpu.VMEM((tm, D_MODEL), BF16)],
        compiler_params=_params("arbitrary", "arbitrary"),
        name="ffn",
    )(h2d, norm2, wg, wu, wd)


def kernel(x_prompt, x_sample, cache_k, cache_v, state_ssm_re, state_ssm_im, norm1, w_in, q_norm, k_norm,
           sinks, ssm_a_re, ssm_a_im, ssm_log_dt, ssm_b_re, ssm_b_im, ssm_c_re, ssm_c_im, ssm_d, w_glu,
           w_br_ssm, w_br_attn, w_gate, w_out, norm2, w_ffn_gate, w_ffn_up, w_ffn_down):
    bp, lp, _ = x_prompt.shape
    bs, ls, _ = x_sample.shape
    ncp = lp // CHUNK
    assert ls == CHUNK and cache_k.shape[2] == WINDOW and norm1.shape[0] == 1
    l = 0

    w_in_b = w_in[l].astype(BF16)
    wglu_b = w_glu[l].astype(BF16)
    wbs_b = w_br_ssm[l].astype(BF16)
    wba_b = w_br_attn[l].astype(BF16)
    wgate_b = w_gate[l].astype(BF16)
    wout_b = w_out[l].astype(BF16)
    wfg_b = w_ffn_gate[l].astype(BF16)
    wfu_b = w_ffn_up[l].astype(BF16)
    wfd_b = w_ffn_down[l].astype(BF16)
    n1 = norm1[l].astype(F32).reshape(1, D_MODEL)
    n2 = norm2[l].astype(F32).reshape(1, D_MODEL)
    qg = jnp.tile(q_norm[l].astype(F32), 2).reshape(1, 128)
    kg = jnp.tile(k_norm[l].astype(F32), 2).reshape(1, 128)
    lane_head = jnp.arange(128) // HEAD_DIM
    seg = (lane_head[:, None] == lane_head[None, :]).astype(BF16)
    sink = sinks[l].astype(F32)

    blk, wst, wout_t, a2 = _ssm_prep(ssm_a_re[l], ssm_a_im[l], ssm_log_dt[l], ssm_b_re[l], ssm_b_im[l],
                                     ssm_c_re[l], ssm_c_im[l], ssm_d[l])

    xp = x_prompt.astype(F32).reshape(bp * lp, D_MODEL)
    xs = x_sample.astype(F32).reshape(bs * ls, D_MODEL)
    tm = 512
    up, qp, kp, vp = _in_proj(xp, np.arange(lp), tm, n1, w_in_b, seg, qg, kg)
    us, qs, ks, vs = _in_proj(xs, np.tile(PAST_LEN + np.arange(ls), tm // ls), tm, n1, w_in_b, seg, qg, kg)

    h0s = jnp.concatenate([state_ssm_re[l], state_ssm_im[l]], axis=-1).astype(F32)
    h0s = jnp.swapaxes(h0s, 0, 1).reshape(N_GROUPS * bs, 2 * STATE)
    y_p, y_s, hfp, hfs = _ssm(up, us, blk, wst, wout_t, a2, h0s, bp, ncp, bs)

    def states(hf, b):
        hf = jnp.swapaxes(hf.reshape(N_GROUPS, b, 2 * STATE), 0, 1)
        return hf[None, :, :, :STATE], hf[None, :, :, STATE:]

    kp3, vp3 = kp.reshape(bp, lp, D_KV), vp.reshape(bp, lp, D_KV)
    o_p = _attention(sink, qp.reshape(bp, lp, D_ATTN), kp3, kp3, vp3, vp3, 256, 2 * CHUNK, True)
    ck = cache_k[l].astype(F32).reshape(bs, WINDOW, D_KV)
    cv = cache_v[l].astype(F32).reshape(bs, WINDOW, D_KV)
    ks3, vs3 = ks.reshape(bs, ls, D_KV), vs.reshape(bs, ls, D_KV)
    o_s = _attention(sink, qs.reshape(bs, ls, D_ATTN), ck, ks3, cv, vs3, ls, ls, False)

    outs = []
    for x2d, y2d, o3d in ((xp, y_p, o_p), (xs, y_s, o_s)):
        h = _mix(x2d, y2d, o3d.reshape(-1, D_ATTN), n1, wglu_b, wgate_b, wbs_b, wba_b, wout_b, 512, 512)
        outs.append(_ffn(h, n2, wfg_b, wfu_b, wfd_b, 1024, 512))

    win = lambda a, b, n: a.reshape(b, n, N_KV, HEAD_DIM)[None]
    k_win_p = win(kp3[:, lp - WINDOW:], bp, WINDOW)
    v_win_p = win(vp3[:, lp - WINDOW:], bp, WINDOW)
    k_win_s = win(jnp.concatenate([ck, ks3], axis=1)[:, ls:], bs, WINDOW)
    v_win_s = win(jnp.concatenate([cv, vs3], axis=1)[:, ls:], bs, WINDOW)
    re_p, im_p = states(hfp, bp)
    re_s, im_s = states(hfs, bs)
    dt = x_prompt.dtype
    return (outs[0].reshape(bp, lp, D_MODEL).astype(dt), outs[1].reshape(bs, ls, D_MODEL).astype(dt),
            k_win_p.astype(dt), v_win_p.astype(dt), re_p.astype(dt), im_p.astype(dt),
            k_win_s.astype(dt), v_win_s.astype(dt), re_s.astype(dt), im_s.astype(dt))
```

```python
import functools

import jax
import jax.numpy as jnp
import numpy as np
from jax import lax
from jax.experimental import pallas as pl
from jax.experimental.pallas import tpu as pltpu

D_MODEL = 2048
CHUNK = 64
D_SSM = 1024
GROUP_CH = 16
N_GROUPS = 64
STATE = 64
HEAD_DIM = 64
N_HEADS = 16
N_KV = 4
Q_PER_KV = 4
D_ATTN = 1024
D_KV = 256
WINDOW = 128
ROT_DIM = 16
ROPE_THETA = 500000.0
PAST_LEN = 1024
EPS = 1e-6
NEG = -1e30

BLK = 256
PITCH = CHUNK + 8
STEPS_PER_BLK = BLK // GROUP_CH
N_BLK = CHUNK // STEPS_PER_BLK
V7X_VMEM_LIMIT = 56 * 1024 * 1024

F32 = jnp.float32
BF16 = jnp.bfloat16


def _dot(a, b):
    return jnp.dot(a, b, preferred_element_type=F32)


def _dot_nt(a, b):
    return lax.dot_general(a, b, (((1,), (1,)), ((), ())), preferred_element_type=F32)


def _params(*sem):
    return pltpu.CompilerParams(dimension_semantics=sem, vmem_limit_bytes=V7X_VMEM_LIMIT)


def _cast_riders(weights, nsteps, step_of):
    in_specs, out_specs, out_shape = [], [], []
    for w in weights:
        rows, cols = w.shape
        slab = rows // nsteps
        assert slab * nsteps == rows and slab % 16 == 0, (w.shape, nsteps)
        spec = pl.BlockSpec((slab, cols), lambda *ids: (step_of(*ids), 0))
        in_specs.append(spec)
        out_specs.append(spec)
        out_shape.append(jax.ShapeDtypeStruct(w.shape, BF16))
    return in_specs, out_specs, out_shape


def _run_riders(src_refs, dst_refs):
    for s, d in zip(src_refs, dst_refs):
        d[...] = s[...].astype(d.dtype)


def _ssm_prep_kernel(are_ref, aim_ref, ldt_ref, bt_ref, bts_ref, c_ref, cs_ref, d_ref, wf_ref,
                     blk_ref, wst_ref, wout_ref, a2_ref, wb_ref, l_s, z_s, w_s):
    _run_riders([wf_ref], [wb_ref])
    lane = lax.broadcasted_iota(jnp.int32, (1, 2 * STATE), 1)
    sg = jnp.where(lane < STATE, 1.0, -1.0).astype(F32)
    a_re = are_ref[...]
    a_im = aim_ref[...]
    dt = jnp.exp(ldt_ref[...])
    lre = a_re * dt
    lim = a_im * dt

    def cpow(e):
        mag = jnp.exp(e * lre)
        ang = e * lim
        return mag * jnp.cos(ang), mag * jnp.sin(ang)

    def cmul(ar, ai, br, bi):
        return ar * br - ai * bi, ar * bi + ai * br

    col = lax.broadcasted_iota(jnp.int32, (STEPS_PER_BLK, 1), 0).astype(F32)
    t_r, t_i = cpow(col)
    r_r, r_i = cpow(float(STEPS_PER_BLK - 1) - col)
    h_r, h_i = cpow(float(STEPS_PER_BLK) * col[:8])
    ab_r, ab_i = t_r[1:2], t_i[1:2]

    fr, fi = ab_r - 1.0, ab_i
    den = a_re * a_re + a_im * a_im
    cr = (fr * a_re + fi * a_im) / den
    ci = (fi * a_re - fr * a_im) / den
    bb = cr * bt_ref[...] - (ci * sg) * bts_ref[...]
    bbs = cr * bts_ref[...] + (ci * sg) * bt_ref[...]
    cc = c_ref[...]
    ccs = cs_ref[...]

    def outer_rows(pr, pi, m, ms, dst, row0, conj):
        pa = pr * sg if conj else pr
        pb = -pi if conj else -(pi * sg)
        for r in range(pr.shape[0]):
            z = pa[r:r + 1, :] * m + pb[r:r + 1, :] * ms
            dst[row0 + r * GROUP_CH:row0 + (r + 1) * GROUP_CH, :] = z

    e_r, e_i = cmul(ab_r, ab_i, r_r, r_i)
    outer_rows(e_r, e_i, bb, bbs, l_s, 0, False)
    l2 = l_s[...].astype(BF16)

    for d in range(N_BLK - 1):
        e_r, e_i = cmul(h_r[d:d + 1], h_i[d:d + 1], t_r, t_i)
        outer_rows(e_r, e_i, cc, ccs, z_s, 0, True)
        blk_ref[d + 1] = _dot_nt(z_s[...].astype(BF16), l2).astype(BF16)

    outer_rows(r_r, r_i, bb, bbs, z_s, 0, False)
    kr = _dot_nt((cc * sg).astype(BF16), z_s[...].astype(BF16))
    rows = lax.broadcasted_iota(jnp.int32, (GROUP_CH, BLK), 0)
    lanes = lax.broadcasted_iota(jnp.int32, (GROUP_CH, BLK), 1)
    kr = kr + jnp.where(lanes == rows + (BLK - GROUP_CH), d_ref[...], 0.0)
    for t0 in range(STEPS_PER_BLK):
        sh = GROUP_CH * (STEPS_PER_BLK - 1 - t0)
        piece = kr if sh == 0 else jnp.where(lanes < BLK - sh, pltpu.roll(kr, BLK - sh, axis=1), 0.0)
        blk_ref[0, t0 * GROUP_CH:(t0 + 1) * GROUP_CH, :] = piece.astype(BF16)

    for sb in range(N_BLK):
        m = N_BLK - 1 - sb
        e_r, e_i = cmul(h_r[m:m + 1], h_i[m:m + 1], r_r, r_i)
        outer_rows(e_r, e_i, bb, bbs, w_s, sb * BLK, False)
    wst_ref[...] = w_s[...].astype(BF16)

    for tb in range(N_BLK):
        e_r, e_i = cmul(*cmul(ab_r, ab_i, h_r[tb:tb + 1], h_i[tb:tb + 1]), t_r, t_i)
        outer_rows(e_r, e_i, cc, ccs, w_s, tb * BLK, True)
    wout_ref[...] = w_s[...].astype(BF16)

    a2_ref[0:1, :] = h_r[N_BLK:N_BLK + 1]
    a2_ref[1:2, :] = -(h_i[N_BLK:N_BLK + 1] * sg)


def _ssm_prep(a_re, a_im, log_dt, b_re, b_im, c_re, c_im, d_skip, ride):
    g = N_GROUPS
    r_in, r_out, r_shape = _cast_riders([ride], g, lambda i: i)
    f = lambda a: a.astype(F32)
    dup = lambda a: jnp.concatenate([f(a), f(a)], axis=-1).reshape(g, 1, 2 * STATE)
    bt_re = jnp.swapaxes(f(b_re), 1, 2)
    bt_im = jnp.swapaxes(f(b_im), 1, 2)
    bt = jnp.concatenate([bt_re, bt_im], axis=-1)
    bts = jnp.concatenate([bt_im, bt_re], axis=-1)
    cc = jnp.concatenate([f(c_re), f(c_im)], axis=-1)
    ccs = jnp.concatenate([f(c_im), f(c_re)], axis=-1)
    d_pad = jnp.pad(f(d_skip).reshape(g, 1, GROUP_CH), ((0, 0), (0, 0), (BLK - GROUP_CH, 0)))
    vec = lambda n: pl.BlockSpec((None, 1, n), lambda i: (i, 0, 0))
    mat = pl.BlockSpec((None, GROUP_CH, 2 * STATE), lambda i: (i, 0, 0))
    kdim = CHUNK * GROUP_CH
    return pl.pallas_call(
        _ssm_prep_kernel,
        grid=(g,),
        in_specs=[vec(2 * STATE), vec(2 * STATE), vec(1), mat, mat, mat, mat, vec(BLK)] + r_in,
        out_specs=[pl.BlockSpec((None, N_BLK, BLK, BLK), lambda i: (i, 0, 0, 0)),
                   pl.BlockSpec((None, kdim, 2 * STATE), lambda i: (i, 0, 0)),
                   pl.BlockSpec((None, kdim, 2 * STATE), lambda i: (i, 0, 0)),
                   pl.BlockSpec((None, 2, 2 * STATE), lambda i: (i, 0, 0))] + r_out,
        out_shape=[jax.ShapeDtypeStruct((g, N_BLK, BLK, BLK), BF16),
                   jax.ShapeDtypeStruct((g, kdim, 2 * STATE), BF16),
                   jax.ShapeDtypeStruct((g, kdim, 2 * STATE), BF16),
                   jax.ShapeDtypeStruct((g, 2, 2 * STATE), F32)] + r_shape,
        scratch_shapes=[pltpu.VMEM((BLK, 2 * STATE), F32), pltpu.VMEM((BLK, 2 * STATE), F32),
                        pltpu.VMEM((kdim, 2 * STATE), F32)],
        compiler_params=_params("arbitrary"),
        name="ssm_prep",
    )(dup(a_re), dup(a_im), f(log_dt).reshape(g, 1, 1), bt, bts, cc, ccs, d_pad, ride)


def _rms(x, g):
    return x * lax.rsqrt(jnp.mean(x * x, axis=-1, keepdims=True) + EPS) * g


def _in_proj_kernel(x_ref, n1_ref, w_ref, qg_ref, kg_ref, rc_ref, rm_ref, rp_ref, *rest):
    n_ride = (len(rest) - 4) // 2
    u_ref, q_ref, k_ref, v_ref = rest[n_ride:n_ride + 4]
    _run_riders(rest[:n_ride], rest[n_ride + 4:])
    hr = x_ref.shape[0] // 2
    lo = lax.broadcasted_iota(jnp.int32, (hr, 128), 1) < HEAD_DIM

    for r in range(2):
        rows = slice(r * hr, (r + 1) * hr)
        xn = _rms(x_ref[rows, :], n1_ref[...]).astype(BF16)
        u = _dot(xn, w_ref[:, 0:D_SSM])
        for c in range(hr // CHUNK):
            base = (r * (hr // CHUNK) + c) * PITCH
            u_ref[base:base + CHUNK, :] = u[c * CHUNK:(c + 1) * CHUNK]
            u_ref[base + CHUNK:base + PITCH, :] = jnp.zeros((PITCH - CHUNK, D_SSM), F32)
        v_ref[rows, :] = _dot(xn, w_ref[:, D_SSM + D_ATTN + D_KV:])
        rc = rc_ref[rows, :]
        rm = rm_ref[rows, :]
        rp = rp_ref[rows, :]

        def norm_rope(z, gain, width, scale):
            outs = []
            for cb in range(width // 128):
                zb = z[:, cb * 128:(cb + 1) * 128]
                sq = zb * zb
                s_lo = jnp.sum(jnp.where(lo, sq, 0.0), axis=-1, keepdims=True)
                s_hi = jnp.sum(jnp.where(lo, 0.0, sq), axis=-1, keepdims=True)
                ms = jnp.where(lo, s_lo, s_hi) * (1.0 / HEAD_DIM)
                zn = zb * lax.rsqrt(ms + EPS) * gain
                rot = zn * rc + pltpu.roll(zn, 128 - ROT_DIM // 2, axis=1) * rm \
                    + pltpu.roll(zn, ROT_DIM // 2, axis=1) * rp
                outs.append(rot * scale)
            return outs

        q = _dot(xn, w_ref[:, D_SSM:D_SSM + D_ATTN])
        for cb, blk in enumerate(norm_rope(q, qg_ref[...], D_ATTN, HEAD_DIM ** -0.5)):
            q_ref[rows, cb * 128:(cb + 1) * 128] = blk.astype(q_ref.dtype)
        k = _dot(xn, w_ref[:, D_SSM + D_ATTN:D_SSM + D_ATTN + D_KV])
        for cb, blk in enumerate(norm_rope(k, kg_ref[...], D_KV, 1.0)):
            k_ref[rows, cb * 128:(cb + 1) * 128] = blk


def _rope_tables(pos):
    half = ROT_DIM // 2
    inv = ROPE_THETA ** (-np.arange(half, dtype=np.float64) * 2.0 / ROT_DIM)
    ang = pos.astype(np.float64)[:, None] * inv[None, :]
    cos, sin = np.cos(ang), np.sin(ang)
    n = pos.shape[0]
    ones = np.ones((n, HEAD_DIM - ROT_DIM))
    zeros = np.zeros((n, HEAD_DIM - ROT_DIM))
    zh = np.zeros((n, half))
    rc = np.concatenate([cos, cos, ones], axis=1)
    rm = np.concatenate([-sin, zh, zeros], axis=1)
    rp = np.concatenate([zh, sin, zeros], axis=1)
    two = lambda t: jnp.asarray(np.concatenate([t, t], axis=1), F32)
    return two(rc), two(rm), two(rp)


def _in_proj(x2d, pos, tm, norm1, w_in_b, qg, kg, ride=()):
    t = x2d.shape[0]
    r_in, r_out, r_shape = _cast_riders(ride, t // tm, lambda i: i)
    rc, rm, rp = _rope_tables(pos)
    nrope = pos.shape[0] // tm
    rope_spec = pl.BlockSpec((tm, 128), lambda i: (i % nrope, 0))
    const = lambda shape: pl.BlockSpec(shape, lambda i: (0, 0))
    tok = lambda w: pl.BlockSpec((tm, w), lambda i: (i, 0))
    return pl.pallas_call(
        _in_proj_kernel,
        grid=(t // tm,),
        in_specs=[tok(D_MODEL), const((1, D_MODEL)), const(w_in_b.shape),
                  const((1, 128)), const((1, 128)), rope_spec, rope_spec, rope_spec] + r_in,
        out_specs=[pl.BlockSpec((tm // CHUNK * PITCH, D_SSM), lambda i: (i, 0)), tok(D_ATTN), tok(D_KV),
                   tok(D_KV)] + r_out,
        out_shape=[jax.ShapeDtypeStruct((t // CHUNK * PITCH, D_SSM), F32), jax.ShapeDtypeStruct((t, D_ATTN), BF16),
                   jax.ShapeDtypeStruct((t, D_KV), F32), jax.ShapeDtypeStruct((t, D_KV), F32)] + r_shape,
        compiler_params=_params("arbitrary"),
        name="in_proj",
    )(x2d, norm1, w_in_b, qg, kg, rc, rm, rp, *ride)


def _ssm_state_kernel(up_ref, us_ref, wst_ref, xtp_ref, xts_ref, sp_ref, ss_ref, *, mp, bs):
    ng = xtp_ref.shape[0]
    pad = jnp.zeros((128 - bs, 128), F32)
    for s in range(CHUNK):
        vp = up_ref[pl.ds(s, mp, stride=PITCH), :].T.astype(BF16)
        vs = jnp.concatenate([us_ref[pl.ds(s, bs, stride=PITCH), :], pad], axis=0).T.astype(BF16)
        for g in range(ng):
            xtp_ref[g, s * GROUP_CH:(s + 1) * GROUP_CH, :] = vp[g * GROUP_CH:(g + 1) * GROUP_CH, :]
            xts_ref[g, s * GROUP_CH:(s + 1) * GROUP_CH, :] = vs[g * GROUP_CH:(g + 1) * GROUP_CH, :]
    tn = (((0,), (0,)), ((), ()))
    for g in range(ng):
        st = lax.dot_general(wst_ref[g], xtp_ref[g], tn, preferred_element_type=F32)
        sp_ref[g * mp:(g + 1) * mp, :] = st.T
        st = lax.dot_general(wst_ref[g], xts_ref[g], tn, preferred_element_type=F32)
        ss_ref[g * bs:(g + 1) * bs, :] = st.T[:bs]


def _cmul_add(h, ar2, ai2, s):
    return h * ar2 + pltpu.roll(h, STATE, axis=1) * ai2 + s


def _ssm_scan_kernel(sp_ref, ss_ref, arp_ref, aip_ref, ars_ref, ais_ref, h0s_ref,
                     hin_ref, hfp_ref, hfs_ref, *, nchunk, rows):
    ar = arp_ref[...]
    ai = aip_ref[...]
    h = jnp.zeros((rows, 2 * STATE), F32)
    for k in range(nchunk):
        hin_ref[pl.ds(k, rows, stride=nchunk), :] = h
        h = _cmul_add(h, ar, ai, sp_ref[pl.ds(k, rows, stride=nchunk), :])
    hfp_ref[...] = h
    hfs_ref[...] = _cmul_add(h0s_ref[...], ars_ref[...], ais_ref[...], ss_ref[...])


def _ssm_out_kernel(xtp_ref, xts_ref, blk_ref, wout_ref, hinp_ref, h0s_ref, yp_ref, ys_ref,
                    stp_s, sts_s, *, mp, bs):
    ng = xtp_ref.shape[0]
    pad = jnp.zeros((128 - bs, 2 * STATE), F32)
    hp = [hinp_ref[g * mp:(g + 1) * mp, :].astype(BF16) for g in range(ng)]
    hs = [jnp.concatenate([h0s_ref[g * bs:(g + 1) * bs, :], pad], axis=0).astype(BF16) for g in range(ng)]
    for tb in range(N_BLK):
        for g in range(ng):
            w = wout_ref[g, tb * BLK:(tb + 1) * BLK, :]
            accp = _dot_nt(w, hp[g])
            accs = _dot_nt(w, hs[g])
            for sb in range(tb + 1):
                b = blk_ref[g, tb - sb]
                accp = accp + _dot(b, xtp_ref[g, sb * BLK:(sb + 1) * BLK, :])
                accs = accs + _dot(b, xts_ref[g, sb * BLK:(sb + 1) * BLK, :])
            for t0 in range(STEPS_PER_BLK):
                stp_s[t0, g * GROUP_CH:(g + 1) * GROUP_CH, :] = accp[t0 * GROUP_CH:(t0 + 1) * GROUP_CH, :]
                sts_s[t0, g * GROUP_CH:(g + 1) * GROUP_CH, :] = accs[t0 * GROUP_CH:(t0 + 1) * GROUP_CH, :]
        for t0 in range(STEPS_PER_BLK):
            t = tb * STEPS_PER_BLK + t0
            yp_ref[pl.ds(t, mp, stride=PITCH), :] = stp_s[t0].T
            ys_ref[pl.ds(t, bs, stride=PITCH), :] = sts_s[t0].T[:bs]
    for r in range(CHUNK, PITCH):
        yp_ref[pl.ds(r, mp, stride=PITCH), :] = jnp.zeros((mp, 128), F32)
        ys_ref[pl.ds(r, bs, stride=PITCH), :] = jnp.zeros((bs, 128), F32)


def _ssm(up, us, blk, wst, wout, a2, h0s, bp, nchunk, bs):
    g = N_GROUPS
    ng = 128 // GROUP_CH
    mp = bp * nchunk
    kdim = CHUNK * GROUP_CH
    tok = lambda a: pl.BlockSpec((a.shape[0], 128), lambda i: (0, i))
    grp = lambda *tail: pl.BlockSpec((ng,) + tail, lambda i: (i,) + (0,) * len(tail))
    rows = lambda r: pl.BlockSpec((ng * r, 2 * STATE), lambda i: (i, 0))
    xtp, xts, sp, ss = pl.pallas_call(
        functools.partial(_ssm_state_kernel, mp=mp, bs=bs),
        grid=(g // ng,),
        in_specs=[tok(up), tok(us), grp(kdim, 2 * STATE)],
        out_specs=[grp(kdim, mp), grp(kdim, 128), rows(mp), rows(bs)],
        out_shape=[jax.ShapeDtypeStruct((g, kdim, mp), BF16), jax.ShapeDtypeStruct((g, kdim, 128), BF16),
                   jax.ShapeDtypeStruct((g * mp, 2 * STATE), F32),
                   jax.ShapeDtypeStruct((g * bs, 2 * STATE), F32)],
        compiler_params=_params("arbitrary"),
        name="ssm_state",
    )(up, us, wst)

    arp = jnp.repeat(a2[:, 0, :], bp, axis=0)
    aip = jnp.repeat(a2[:, 1, :], bp, axis=0)
    ars = jnp.repeat(a2[:, 0, :], bs, axis=0)
    ais = jnp.repeat(a2[:, 1, :], bs, axis=0)
    full = lambda r: pl.BlockSpec((r, 2 * STATE), lambda i: (0, 0))
    hin, hfp, hfs = pl.pallas_call(
        functools.partial(_ssm_scan_kernel, nchunk=nchunk, rows=g * bp),
        grid=(1,),
        in_specs=[full(g * mp), full(g * bs), full(g * bp), full(g * bp), full(g * bs), full(g * bs),
                  full(g * bs)],
        out_specs=[full(g * mp), full(g * bp), full(g * bs)],
        out_shape=[jax.ShapeDtypeStruct((g * mp, 2 * STATE), F32),
                   jax.ShapeDtypeStruct((g * bp, 2 * STATE), F32),
                   jax.ShapeDtypeStruct((g * bs, 2 * STATE), F32)],
        compiler_params=_params("arbitrary"),
        name="ssm_scan",
    )(sp, ss, arp, aip, ars, ais, h0s)

    yp, ys = pl.pallas_call(
        functools.partial(_ssm_out_kernel, mp=mp, bs=bs),
        grid=(g // ng,),
        in_specs=[grp(kdim, mp), grp(kdim, 128), grp(N_BLK, BLK, BLK), grp(kdim, 2 * STATE), rows(mp), rows(bs)],
        out_specs=[tok(up), tok(us)],
        out_shape=[jax.ShapeDtypeStruct(up.shape, F32), jax.ShapeDtypeStruct(us.shape, F32)],
        scratch_shapes=[pltpu.VMEM((STEPS_PER_BLK, 128, mp), F32), pltpu.VMEM((STEPS_PER_BLK, 128, 128), F32)],
        compiler_params=_params("arbitrary"),
        name="ssm_out",
    )(xtp, xts, blk, wout, hin, h0s)
    return yp, ys, hfp, hfs


def _attn_kernel(sinks_ref, q_ref, kp_ref, kc_ref, vp_ref, vc_ref, *rest, qp, kw, masked):
    n_ride = (len(rest) - 1) // 2
    o_ref = rest[n_ride]
    _run_riders(rest[:n_ride], rest[n_ride + 1:])
    i = pl.program_id(1)
    n_units = q_ref.shape[0] // qp
    gw = Q_PER_KV * HEAD_DIM
    half = HEAD_DIM
    nl = Q_PER_KV * qp

    def duplicated(prev_ref, cur_ref):
        a = jnp.concatenate([prev_ref[...], cur_ref[...]], axis=0)
        lo = lax.broadcasted_iota(jnp.int32, (a.shape[0], 2 * half), 1) < half
        out = []
        for c in range(N_KV // 2):
            col = a[:, c * 2 * half:(c + 1) * 2 * half]
            rot = pltpu.roll(col, half, axis=1)
            out.append(jnp.where(lo, col, rot).astype(BF16))
            out.append(jnp.where(lo, rot, col).astype(BF16))
        return out

    kdup = duplicated(kp_ref, kc_ref)
    vdup = duplicated(vp_ref, vc_ref)
    pair_lo = lax.broadcasted_iota(jnp.int32, (qp, 2 * half), 1) < half
    lane_g = lax.broadcasted_iota(jnp.int32, (1, nl), 1) // qp

    for u in range(n_units):
        rows = slice(u * qp, (u + 1) * qp)
        win = slice(u * qp, u * qp + kw)
        if masked:
            kc = lax.broadcasted_iota(jnp.int32, (kw, nl), 0) // CHUNK
            qc = (lax.broadcasted_iota(jnp.int32, (kw, nl), 1) % qp) // CHUNK
            first = WINDOW // CHUNK - (i * n_units + u) * (qp // CHUNK)
            mask = (kc >= qc) & (kc <= qc + WINDOW // CHUNK) & (kc >= first)
        for kv in range(N_KV):
            qm = []
            for g in range(Q_PER_KV):
                pair = q_ref[rows, kv * gw + (g // 2) * 2 * half:kv * gw + (g // 2 + 1) * 2 * half]
                keep = pair_lo if g % 2 == 0 else jnp.logical_not(pair_lo)
                qm.append(jnp.where(keep, pair, jnp.zeros_like(pair)))
            s = _dot_nt(kdup[kv][win], jnp.concatenate(qm, axis=0))
            if masked:
                s = jnp.where(mask, s, NEG)
            sink = jnp.zeros((1, nl), F32)
            for g in range(Q_PER_KV):
                sink = jnp.where(lane_g == g, sinks_ref[kv * Q_PER_KV + g], sink)
            m = jnp.maximum(jnp.max(s, axis=0, keepdims=True), sink)
            p = jnp.exp(s - m)
            den = jnp.sum(p, axis=0, keepdims=True) + jnp.exp(sink - m)
            pn = (p * (1.0 / den)).astype(BF16)
            o = lax.dot_general(pn, vdup[kv][win], (((0,), (0,)), ((), ())),
                                preferred_element_type=F32)
            for h in range(Q_PER_KV // 2):
                both = jnp.where(pair_lo, o[2 * h * qp:(2 * h + 1) * qp], o[(2 * h + 1) * qp:(2 * h + 2) * qp])
                o_ref[rows, kv * gw + h * 2 * half:kv * gw + (h + 1) * 2 * half] = both.astype(o_ref.dtype)


def _attention(sinks, q, k_prev, k_cur, v_prev, v_cur, qt, qp, masked, ride=()):
    b, l, _ = q.shape
    nt = l // qt
    r_in, r_out, r_shape = _cast_riders(ride, b * nt, lambda bi, i: bi * nt + i)
    per = qt // WINDOW
    kw = WINDOW + qp
    if masked:
        prev_map = lambda bi, i: (bi, jnp.maximum(i * per - 1, 0), 0)
    else:
        prev_map = lambda bi, i: (bi, 0, 0)
    cur = lambda w: pl.BlockSpec((None, qt, w), lambda bi, i: (bi, i, 0))
    prev = pl.BlockSpec((None, WINDOW, D_KV), prev_map)
    return pl.pallas_call(
        functools.partial(_attn_kernel, qp=qp, kw=kw, masked=masked),
        grid=(b, l // qt),
        in_specs=[pl.BlockSpec(memory_space=pltpu.SMEM), cur(D_ATTN), prev, cur(D_KV), prev, cur(D_KV)] + r_in,
        out_specs=[cur(D_ATTN)] + r_out,
        out_shape=[jax.ShapeDtypeStruct((b, l, D_ATTN), BF16)] + r_shape,
        compiler_params=_params("arbitrary", "arbitrary"),
        name="attention",
    )(sinks, q, k_prev, k_cur, v_prev, v_cur, *ride)


def _mix_kernel(x_ref, y_ref, o_ref, n1_ref, wglu_ref, wga_ref, wgb_ref, wbs_ref, wba_ref, wout_ref,
                h_ref, xn_s, ys_s):
    j = pl.program_id(1)

    @pl.when(j == 0)
    def _():
        x = x_ref[...]
        xn_s[...] = _rms(x, n1_ref[...]).astype(BF16)
        y = jnp.concatenate([y_ref[c * PITCH:c * PITCH + CHUNK, :] for c in range(x.shape[0] // CHUNK)], axis=0)
        ya = jax.nn.gelu(y)
        ys_s[...] = (ya * jax.nn.sigmoid(_dot(ya.astype(BF16), wglu_ref[...]))).astype(BF16)
        h_ref[...] = x

    xn = xn_s[...]
    ga = jax.nn.sigmoid(_dot(xn, wga_ref[...]))
    gb = jax.nn.sigmoid(_dot(xn, wgb_ref[...]))
    mixed = ga * _dot(ys_s[...], wbs_ref[...]) + gb * _dot(o_ref[...], wba_ref[...])
    h_ref[...] += _dot(mixed.astype(BF16), wout_ref[...])


def _mix(x2d, y2d, o2d, norm1, wglu, wgate, wbs, wba, wout, tm, tn):
    t = x2d.shape[0]
    nj = D_MODEL // tn
    tok = lambda w: pl.BlockSpec((tm, w), lambda i, j: (i, 0))
    return pl.pallas_call(
        _mix_kernel,
        grid=(t // tm, nj),
        in_specs=[tok(D_MODEL), pl.BlockSpec((tm // CHUNK * PITCH, D_SSM), lambda i, j: (i, 0)), tok(D_ATTN),
                  pl.BlockSpec((1, D_MODEL), lambda i, j: (0, 0)),
                  pl.BlockSpec((D_SSM, D_SSM), lambda i, j: (0, 0)),
                  pl.BlockSpec((D_MODEL, tn), lambda i, j: (0, j)),
                  pl.BlockSpec((D_MODEL, tn), lambda i, j: (0, nj + j)),
                  pl.BlockSpec((D_SSM, tn), lambda i, j: (0, j)),
                  pl.BlockSpec((D_ATTN, tn), lambda i, j: (0, j)),
                  pl.BlockSpec((tn, D_MODEL), lambda i, j: (j, 0))],
        out_specs=tok(D_MODEL),
        out_shape=jax.ShapeDtypeStruct((t, D_MODEL), F32),
        scratch_shapes=[pltpu.VMEM((tm, D_MODEL), BF16), pltpu.VMEM((tm, D_SSM), BF16)],
        compiler_params=_params("arbitrary", "arbitrary"),
        name="mix",
    )(x2d, y2d, o2d, norm1, wglu, wgate, wgate, wbs, wba, wout)


def _ffn_kernel(h_ref, n2_ref, wg_ref, wu_ref, wd_ref, out_ref, hn_s):
    j = pl.program_id(1)

    @pl.when(j == 0)
    def _():
        h = h_ref[...]
        hn_s[...] = _rms(h, n2_ref[...]).astype(BF16)
        out_ref[...] = h

    hn = hn_s[...]
    act = jax.nn.silu(_dot(hn, wg_ref[...])) * _dot(hn, wu_ref[...])
    out_ref[...] += _dot(act.astype(BF16), wd_ref[...])


def _ffn(h2d, norm2, wg, wu, wd, tm, tf):
    t = h2d.shape[0]
    d_ff = wg.shape[1]
    tok = pl.BlockSpec((tm, D_MODEL), lambda i, j: (i, 0))
    return pl.pallas_call(
        _ffn_kernel,
        grid=(t // tm, d_ff // tf),
        in_specs=[tok, pl.BlockSpec((1, D_MODEL), lambda i, j: (0, 0)),
                  pl.BlockSpec((D_MODEL, tf), lambda i, j: (0, j)),
                  pl.BlockSpec((D_MODEL, tf), lambda i, j: (0, j)),
                  pl.BlockSpec((tf, D_MODEL), lambda i, j: (j, 0))],
        out_specs=tok,
        out_shape=jax.ShapeDtypeStruct((t, D_MODEL), F32),
        scratch_shapes=[pltpu.VMEM((tm, D_MODEL), BF16)],
        compiler_params=_params("arbitrary", "arbitrary"),
        name="ffn",
    )(h2d, norm2, wg, wu, wd)


def kernel(x_prompt, x_sample, cache_k, cache_v, state_ssm_re, state_ssm_im, norm1, w_in, q_norm, k_norm,
           sinks, ssm_a_re, ssm_a_im, ssm_log_dt, ssm_b_re, ssm_b_im, ssm_c_re, ssm_c_im, ssm_d, w_glu,
           w_br_ssm, w_br_attn, w_gate, w_out, norm2, w_ffn_gate, w_ffn_up, w_ffn_down):
    bp, lp, _ = x_prompt.shape
    bs, ls, _ = x_sample.shape
    ncp = lp // CHUNK
    assert ls == CHUNK and cache_k.shape[2] == WINDOW and norm1.shape[0] == 1
    l = 0

    f32 = lambda a: a[l].astype(F32)
    n1 = norm1[l].astype(F32).reshape(1, D_MODEL)
    n2 = norm2[l].astype(F32).reshape(1, D_MODEL)
    qg = jnp.tile(q_norm[l].astype(F32), 2).reshape(1, 128)
    kg = jnp.tile(k_norm[l].astype(F32), 2).reshape(1, 128)
    sink = sinks[l].astype(F32)

    blk, wst, wout_t, a2, w_in_b = _ssm_prep(ssm_a_re[l], ssm_a_im[l], ssm_log_dt[l], ssm_b_re[l], ssm_b_im[l],
                                             ssm_c_re[l], ssm_c_im[l], ssm_d[l], f32(w_in))

    xp = x_prompt.astype(F32).reshape(bp * lp, D_MODEL)
    xs = x_sample.astype(F32).reshape(bs * ls, D_MODEL)
    tm = 512
    up, qp, kp, vp, wfg_b, wfu_b, wfd_b = _in_proj(xp, np.arange(lp), tm, n1, w_in_b, qg, kg,
                                                   ride=(f32(w_ffn_gate), f32(w_ffn_up), f32(w_ffn_down)))
    us, qs, ks, vs = _in_proj(xs, np.tile(PAST_LEN + np.arange(ls), tm // ls), tm, n1, w_in_b, qg, kg)

    h0s = jnp.concatenate([state_ssm_re[l], state_ssm_im[l]], axis=-1).astype(F32)
    h0s = jnp.swapaxes(h0s, 0, 1).reshape(N_GROUPS * bs, 2 * STATE)
    y_p, y_s, hfp, hfs = _ssm(up, us, blk, wst, wout_t, a2, h0s, bp, ncp, bs)

    def states(hf, b):
        hf = jnp.swapaxes(hf.reshape(N_GROUPS, b, 2 * STATE), 0, 1)
        return hf[None, :, :, :STATE], hf[None, :, :, STATE:]

    kp3, vp3 = kp.reshape(bp, lp, D_KV), vp.reshape(bp, lp, D_KV)
    o_p, wgate_b, wout_b, wbs_b, wba_b, wglu_b = _attention(
        sink, qp.reshape(bp, lp, D_ATTN), kp3, kp3, vp3, vp3, 256, 2 * CHUNK, True,
        ride=(f32(w_gate), f32(w_out), f32(w_br_ssm), f32(w_br_attn), f32(w_glu)))
    ck = cache_k[l].astype(F32).reshape(bs, WINDOW, D_KV)
    cv = cache_v[l].astype(F32).reshape(bs, WINDOW, D_KV)
    ks3, vs3 = ks.reshape(bs, ls, D_KV), vs.reshape(bs, ls, D_KV)
    o_s, = _attention(sink, qs.reshape(bs, ls, D_ATTN), ck, ks3, cv, vs3, ls, ls, False)

    outs = []
    for x2d, y2d, o3d in ((xp, y_p, o_p), (xs, y_s, o_s)):
        h = _mix(x2d, y2d, o3d.reshape(-1, D_ATTN), n1, wglu_b, wgate_b, wbs_b, wba_b, wout_b, 512, 512)
        outs.append(_ffn(h, n2, wfg_b, wfu_b, wfd_b, 1024, 512))

    win = lambda a, b, n: a.reshape(b, n, N_KV, HEAD_DIM)[None]
    k_win_p = win(kp3[:, lp - WINDOW:], bp, WINDOW)
    v_win_p = win(vp3[:, lp - WINDOW:], bp, WINDOW)
    k_win_s = win(jnp.concatenate([ck, ks3], axis=1)[:, ls:], bs, WINDOW)
    v_win_s = win(jnp.concatenate([cv, vs3], axis=1)[:, ls:], bs, WINDOW)
    re_p, im_p = states(hfp, bp)
    re_s, im_s = states(hfs, bs)
    dt = x_prompt.dtype
    return (outs[0].reshape(bp, lp, D_MODEL).astype(dt), outs[1].reshape(bs, ls, D_MODEL).astype(dt),
            k_win_p.astype(dt), v_win_p.astype(dt), re_p.astype(dt), im_p.astype(dt),
            k_win_s.astype(dt), v_win_s.astype(dt), re_s.astype(dt), im_s.astype(dt))
```

```python
import functools

import jax
import jax.numpy as jnp
import numpy as np
from jax import lax
from jax.experimental import pallas as pl
from jax.experimental.pallas import tpu as pltpu

D_MODEL = 2048
CHUNK = 64
D_SSM = 1024
GROUP_CH = 16
N_GROUPS = 64
STATE = 64
HEAD_DIM = 64
N_HEADS = 16
N_KV = 4
Q_PER_KV = 4
D_ATTN = 1024
D_KV = 256
WINDOW = 128
ROT_DIM = 16
ROPE_THETA = 500000.0
PAST_LEN = 1024
EPS = 1e-6
NEG = -1e30
LOG2E = 1.4426950408889634

BLK = 256
PITCH = CHUNK + 8
STEPS_PER_BLK = BLK // GROUP_CH
N_BLK = CHUNK // STEPS_PER_BLK
V7X_VMEM_LIMIT = 56 * 1024 * 1024

F32 = jnp.float32
BF16 = jnp.bfloat16


def _dot(a, b):
    return jnp.dot(a, b, preferred_element_type=F32)


def _dot_nt(a, b):
    return lax.dot_general(a, b, (((1,), (1,)), ((), ())), preferred_element_type=F32)


def _params(*sem):
    return pltpu.CompilerParams(dimension_semantics=sem, vmem_limit_bytes=V7X_VMEM_LIMIT)


def _cast_riders(weights, nsteps, step_of):
    in_specs, out_specs, out_shape = [], [], []
    for w in weights:
        rows, cols = w.shape
        slab = rows // nsteps
        assert slab * nsteps == rows and slab % 16 == 0, (w.shape, nsteps)
        spec = pl.BlockSpec((slab, cols), lambda *ids: (step_of(*ids), 0))
        in_specs.append(spec)
        out_specs.append(spec)
        out_shape.append(jax.ShapeDtypeStruct(w.shape, BF16))
    return in_specs, out_specs, out_shape


def _run_riders(src_refs, dst_refs):
    for s, d in zip(src_refs, dst_refs):
        d[...] = s[...].astype(d.dtype)


def _ssm_prep_kernel(*refs):
    ins, wf_ref, outs, wb_ref, scratch = refs[:8], refs[8], refs[9:13], refs[13], refs[14:]
    _run_riders([wf_ref], [wb_ref])
    for g in range(ins[0].shape[0]):
        _ssm_prep_group(*[r.at[g] for r in ins + outs + scratch])


def _ssm_prep_group(are_ref, aim_ref, ldt_ref, bt_ref, bts_ref, c_ref, cs_ref, d_ref,
                    blk_ref, wst_ref, wout_ref, a2_ref, l_s, z_s, w_s):
    lane = lax.broadcasted_iota(jnp.int32, (1, 2 * STATE), 1)
    sg = jnp.where(lane < STATE, 1.0, -1.0).astype(F32)
    a_re = are_ref[...]
    a_im = aim_ref[...]
    dt = jnp.exp(ldt_ref[...])
    lre = a_re * dt
    lim = a_im * dt

    def cpow(e):
        mag = jnp.exp(e * lre)
        ang = e * lim
        return mag * jnp.cos(ang), mag * jnp.sin(ang)

    def cmul(ar, ai, br, bi):
        return ar * br - ai * bi, ar * bi + ai * br

    col = lax.broadcasted_iota(jnp.int32, (STEPS_PER_BLK, 1), 0).astype(F32)
    t_r, t_i = cpow(col)
    r_r, r_i = cpow(float(STEPS_PER_BLK - 1) - col)
    h_r, h_i = cpow(float(STEPS_PER_BLK) * col[:8])
    ab_r, ab_i = t_r[1:2], t_i[1:2]

    fr, fi = ab_r - 1.0, ab_i
    den = a_re * a_re + a_im * a_im
    cr = (fr * a_re + fi * a_im) / den
    ci = (fi * a_re - fr * a_im) / den
    bb = cr * bt_ref[...] - (ci * sg) * bts_ref[...]
    bbs = cr * bts_ref[...] + (ci * sg) * bt_ref[...]
    cc = c_ref[...]
    ccs = cs_ref[...]

    def outer_rows(pr, pi, m, ms, dst, row0, conj):
        pa = pr * sg if conj else pr
        pb = -pi if conj else -(pi * sg)
        for r in range(pr.shape[0]):
            z = pa[r:r + 1, :] * m + pb[r:r + 1, :] * ms
            dst[row0 + r * GROUP_CH:row0 + (r + 1) * GROUP_CH, :] = z

    e_r, e_i = cmul(ab_r, ab_i, r_r, r_i)
    outer_rows(e_r, e_i, bb, bbs, l_s, 0, False)
    l2 = l_s[...].astype(BF16)

    for d in range(N_BLK - 1):
        e_r, e_i = cmul(h_r[d:d + 1], h_i[d:d + 1], t_r, t_i)
        outer_rows(e_r, e_i, cc, ccs, z_s, 0, True)
        blk_ref[d + 1] = _dot_nt(z_s[...].astype(BF16), l2).astype(BF16)

    outer_rows(r_r, r_i, bb, bbs, z_s, 0, False)
    kr = _dot_nt((cc * sg).astype(BF16), z_s[...].astype(BF16))
    rows = lax.broadcasted_iota(jnp.int32, (GROUP_CH, BLK), 0)
    lanes = lax.broadcasted_iota(jnp.int32, (GROUP_CH, BLK), 1)
    kr = kr + jnp.where(lanes == rows + (BLK - GROUP_CH), d_ref[...], 0.0)
    for t0 in range(STEPS_PER_BLK):
        sh = GROUP_CH * (STEPS_PER_BLK - 1 - t0)
        piece = kr if sh == 0 else jnp.where(lanes < BLK - sh, pltpu.roll(kr, BLK - sh, axis=1), 0.0)
        blk_ref[0, t0 * GROUP_CH:(t0 + 1) * GROUP_CH, :] = piece.astype(BF16)

    for sb in range(N_BLK):
        m = N_BLK - 1 - sb
        e_r, e_i = cmul(h_r[m:m + 1], h_i[m:m + 1], r_r, r_i)
        outer_rows(e_r, e_i, bb, bbs, w_s, sb * BLK, False)
    wst_ref[...] = w_s[...].astype(BF16)

    for tb in range(N_BLK):
        e_r, e_i = cmul(*cmul(ab_r, ab_i, h_r[tb:tb + 1], h_i[tb:tb + 1]), t_r, t_i)
        outer_rows(e_r, e_i, cc, ccs, w_s, tb * BLK, True)
    wout_ref[...] = w_s[...].astype(BF16)

    a2_ref[0:1, :] = h_r[N_BLK:N_BLK + 1]
    a2_ref[1:2, :] = -(h_i[N_BLK:N_BLK + 1] * sg)


def _ssm_prep(a_re, a_im, log_dt, b_re, b_im, c_re, c_im, d_skip, ride):
    g = N_GROUPS
    ng = 8
    r_in, r_out, r_shape = _cast_riders([ride], g // ng, lambda i: i)
    f = lambda a: a.astype(F32)
    dup = lambda a: jnp.concatenate([f(a), f(a)], axis=-1).reshape(g, 1, 2 * STATE)
    bt_re = jnp.swapaxes(f(b_re), 1, 2)
    bt_im = jnp.swapaxes(f(b_im), 1, 2)
    bt = jnp.concatenate([bt_re, bt_im], axis=-1)
    bts = jnp.concatenate([bt_im, bt_re], axis=-1)
    cc = jnp.concatenate([f(c_re), f(c_im)], axis=-1)
    ccs = jnp.concatenate([f(c_im), f(c_re)], axis=-1)
    d_pad = jnp.pad(f(d_skip).reshape(g, 1, GROUP_CH), ((0, 0), (0, 0), (BLK - GROUP_CH, 0)))
    vec = lambda n: pl.BlockSpec((ng, 1, n), lambda i: (i, 0, 0))
    mat = pl.BlockSpec((ng, GROUP_CH, 2 * STATE), lambda i: (i, 0, 0))
    kdim = CHUNK * GROUP_CH
    return pl.pallas_call(
        _ssm_prep_kernel,
        grid=(g // ng,),
        in_specs=[vec(2 * STATE), vec(2 * STATE), vec(1), mat, mat, mat, mat, vec(BLK)] + r_in,
        out_specs=[pl.BlockSpec((ng, N_BLK, BLK, BLK), lambda i: (i, 0, 0, 0)),
                   pl.BlockSpec((ng, kdim, 2 * STATE), lambda i: (i, 0, 0)),
                   pl.BlockSpec((ng, kdim, 2 * STATE), lambda i: (i, 0, 0)),
                   pl.BlockSpec((ng, 2, 2 * STATE), lambda i: (i, 0, 0))] + r_out,
        out_shape=[jax.ShapeDtypeStruct((g, N_BLK, BLK, BLK), BF16),
                   jax.ShapeDtypeStruct((g, kdim, 2 * STATE), BF16),
                   jax.ShapeDtypeStruct((g, kdim, 2 * STATE), BF16),
                   jax.ShapeDtypeStruct((g, 2, 2 * STATE), F32)] + r_shape,
        scratch_shapes=[pltpu.VMEM((ng, BLK, 2 * STATE), F32), pltpu.VMEM((ng, BLK, 2 * STATE), F32),
                        pltpu.VMEM((ng, kdim, 2 * STATE), F32)],
        compiler_params=_params("arbitrary"),
        name="ssm_prep",
    )(dup(a_re), dup(a_im), f(log_dt).reshape(g, 1, 1), bt, bts, cc, ccs, d_pad, ride)


def _rms(x, g):
    return x * lax.rsqrt(jnp.mean(x * x, axis=-1, keepdims=True) + EPS) * g


def _in_proj_kernel(x_ref, n1_ref, w_ref, qg_ref, kg_ref, rc_ref, rm_ref, rp_ref, *rest):
    n_ride = (len(rest) - 4) // 2
    u_ref, q_ref, k_ref, v_ref = rest[n_ride:n_ride + 4]
    _run_riders(rest[:n_ride], rest[n_ride + 4:])
    hr = x_ref.shape[0] // 2
    lo = lax.broadcasted_iota(jnp.int32, (hr, 128), 1) < HEAD_DIM

    for r in range(2):
        rows = slice(r * hr, (r + 1) * hr)
        xn = _rms(x_ref[rows, :], n1_ref[...]).astype(BF16)
        u = _dot(xn, w_ref[:, 0:D_SSM])
        for c in range(hr // CHUNK):
            base = (r * (hr // CHUNK) + c) * PITCH
            u_ref[base:base + CHUNK, :] = u[c * CHUNK:(c + 1) * CHUNK]
            u_ref[base + CHUNK:base + PITCH, :] = jnp.zeros((PITCH - CHUNK, D_SSM), F32)
        v_ref[rows, :] = _dot(xn, w_ref[:, D_SSM + D_ATTN + D_KV:])
        rc = rc_ref[rows, :]
        rm = rm_ref[rows, :]
        rp = rp_ref[rows, :]

        def norm_rope(z, gain, width, scale):
            outs = []
            for cb in range(width // 128):
                zb = z[:, cb * 128:(cb + 1) * 128]
                sq = zb * zb
                s_lo = jnp.sum(jnp.where(lo, sq, 0.0), axis=-1, keepdims=True)
                s_hi = jnp.sum(jnp.where(lo, 0.0, sq), axis=-1, keepdims=True)
                ms = jnp.where(lo, s_lo, s_hi) * (1.0 / HEAD_DIM)
                zn = zb * lax.rsqrt(ms + EPS) * gain
                rot = zn * rc + pltpu.roll(zn, 128 - ROT_DIM // 2, axis=1) * rm \
                    + pltpu.roll(zn, ROT_DIM // 2, axis=1) * rp
                outs.append(rot * scale)
            return outs

        q = _dot(xn, w_ref[:, D_SSM:D_SSM + D_ATTN])
        for cb, blk in enumerate(norm_rope(q, qg_ref[...], D_ATTN, HEAD_DIM ** -0.5 * LOG2E)):
            q_ref[rows, cb * 128:(cb + 1) * 128] = blk.astype(q_ref.dtype)
        k = _dot(xn, w_ref[:, D_SSM + D_ATTN:D_SSM + D_ATTN + D_KV])
        for cb, blk in enumerate(norm_rope(k, kg_ref[...], D_KV, 1.0)):
            k_ref[rows, cb * 128:(cb + 1) * 128] = blk


def _rope_tables(pos):
    half = ROT_DIM // 2
    inv = ROPE_THETA ** (-np.arange(half, dtype=np.float64) * 2.0 / ROT_DIM)
    ang = pos.astype(np.float64)[:, None] * inv[None, :]
    cos, sin = np.cos(ang), np.sin(ang)
    n = pos.shape[0]
    ones = np.ones((n, HEAD_DIM - ROT_DIM))
    zeros = np.zeros((n, HEAD_DIM - ROT_DIM))
    zh = np.zeros((n, half))
    rc = np.concatenate([cos, cos, ones], axis=1)
    rm = np.concatenate([-sin, zh, zeros], axis=1)
    rp = np.concatenate([zh, sin, zeros], axis=1)
    two = lambda t: jnp.asarray(np.concatenate([t, t], axis=1), F32)
    return two(rc), two(rm), two(rp)


def _in_proj(x2d, pos, tm, norm1, w_in_b, qg, kg, ride=()):
    t = x2d.shape[0]
    r_in, r_out, r_shape = _cast_riders(ride, t // tm, lambda i: i)
    rc, rm, rp = _rope_tables(pos)
    nrope = pos.shape[0] // tm
    rope_spec = pl.BlockSpec((tm, 128), lambda i: (i % nrope, 0))
    const = lambda shape: pl.BlockSpec(shape, lambda i: (0, 0))
    tok = lambda w: pl.BlockSpec((tm, w), lambda i: (i, 0))
    return pl.pallas_call(
        _in_proj_kernel,
        grid=(t // tm,),
        in_specs=[tok(D_MODEL), const((1, D_MODEL)), const(w_in_b.shape),
                  const((1, 128)), const((1, 128)), rope_spec, rope_spec, rope_spec] + r_in,
        out_specs=[pl.BlockSpec((tm // CHUNK * PITCH, D_SSM), lambda i: (i, 0)), tok(D_ATTN), tok(D_KV),
                   tok(D_KV)] + r_out,
        out_shape=[jax.ShapeDtypeStruct((t // CHUNK * PITCH, D_SSM), F32), jax.ShapeDtypeStruct((t, D_ATTN), BF16),
                   jax.ShapeDtypeStruct((t, D_KV), F32), jax.ShapeDtypeStruct((t, D_KV), F32)] + r_shape,
        compiler_params=_params("arbitrary"),
        name="in_proj",
    )(x2d, norm1, w_in_b, qg, kg, rc, rm, rp, *ride)


def _ssm_state_kernel(up_ref, us_ref, wst_ref, xtp_ref, xts_ref, sp_ref, ss_ref, *, mp, bs):
    ng = xtp_ref.shape[0]
    pad = jnp.zeros((128 - bs, 128), F32)
    for s in range(CHUNK):
        vp = up_ref[pl.ds(s, mp, stride=PITCH), :].T.astype(BF16)
        vs = jnp.concatenate([us_ref[pl.ds(s, bs, stride=PITCH), :], pad], axis=0).T.astype(BF16)
        for g in range(ng):
            xtp_ref[g, s * GROUP_CH:(s + 1) * GROUP_CH, :] = vp[g * GROUP_CH:(g + 1) * GROUP_CH, :]
            xts_ref[g, s * GROUP_CH:(s + 1) * GROUP_CH, :] = vs[g * GROUP_CH:(g + 1) * GROUP_CH, :]
    tn = (((0,), (0,)), ((), ()))
    for g in range(ng):
        st = lax.dot_general(wst_ref[g], xtp_ref[g], tn, preferred_element_type=F32)
        sp_ref[g * mp:(g + 1) * mp, :] = st.T
        st = lax.dot_general(wst_ref[g], xts_ref[g], tn, preferred_element_type=F32)
        ss_ref[g * bs:(g + 1) * bs, :] = st.T[:bs]


def _cmul_add(h, ar2, ai2, s):
    return h * ar2 + pltpu.roll(h, STATE, axis=1) * ai2 + s


def _ssm_scan_kernel(sp_ref, ss_ref, arp_ref, aip_ref, ars_ref, ais_ref, h0s_ref,
                     hin_ref, hfp_ref, hfs_ref, *, nchunk, rows):
    ar = arp_ref[...]
    ai = aip_ref[...]
    h = jnp.zeros((rows, 2 * STATE), F32)
    for k in range(nchunk):
        hin_ref[pl.ds(k, rows, stride=nchunk), :] = h
        h = _cmul_add(h, ar, ai, sp_ref[pl.ds(k, rows, stride=nchunk), :])
    hfp_ref[...] = h
    hfs_ref[...] = _cmul_add(h0s_ref[...], ars_ref[...], ais_ref[...], ss_ref[...])


def _ssm_out_kernel(xtp_ref, xts_ref, blk_ref, wout_ref, hinp_ref, h0s_ref, yp_ref, ys_ref,
                    stp_s, sts_s, *, mp, bs):
    ng = xtp_ref.shape[0]
    pad = jnp.zeros((128 - bs, 2 * STATE), F32)
    hp = [hinp_ref[g * mp:(g + 1) * mp, :].astype(BF16) for g in range(ng)]
    hs = [jnp.concatenate([h0s_ref[g * bs:(g + 1) * bs, :], pad], axis=0).astype(BF16) for g in range(ng)]
    for tb in range(N_BLK):
        for g in range(ng):
            w = wout_ref[g, tb * BLK:(tb + 1) * BLK, :]
            accp = _dot_nt(w, hp[g])
            accs = _dot_nt(w, hs[g])
            for sb in range(tb + 1):
                b = blk_ref[g, tb - sb]
                accp = accp + _dot(b, xtp_ref[g, sb * BLK:(sb + 1) * BLK, :])
                accs = accs + _dot(b, xts_ref[g, sb * BLK:(sb + 1) * BLK, :])
            for t0 in range(STEPS_PER_BLK):
                stp_s[t0, g * GROUP_CH:(g + 1) * GROUP_CH, :] = accp[t0 * GROUP_CH:(t0 + 1) * GROUP_CH, :]
                sts_s[t0, g * GROUP_CH:(g + 1) * GROUP_CH, :] = accs[t0 * GROUP_CH:(t0 + 1) * GROUP_CH, :]
        for t0 in range(STEPS_PER_BLK):
            t = tb * STEPS_PER_BLK + t0
            yp_ref[pl.ds(t, mp, stride=PITCH), :] = stp_s[t0].T
            ys_ref[pl.ds(t, bs, stride=PITCH), :] = sts_s[t0].T[:bs]
    for r in range(CHUNK, PITCH):
        yp_ref[pl.ds(r, mp, stride=PITCH), :] = jnp.zeros((mp, 128), F32)
        ys_ref[pl.ds(r, bs, stride=PITCH), :] = jnp.zeros((bs, 128), F32)


def _ssm(up, us, blk, wst, wout, a2, h0s, bp, nchunk, bs):
    g = N_GROUPS
    ng = 128 // GROUP_CH
    mp = bp * nchunk
    kdim = CHUNK * GROUP_CH
    tok = lambda a: pl.BlockSpec((a.shape[0], 128), lambda i: (0, i))
    grp = lambda *tail: pl.BlockSpec((ng,) + tail, lambda i: (i,) + (0,) * len(tail))
    rows = lambda r: pl.BlockSpec((ng * r, 2 * STATE), lambda i: (i, 0))
    xtp, xts, sp, ss = pl.pallas_call(
        functools.partial(_ssm_state_kernel, mp=mp, bs=bs),
        grid=(g // ng,),
        in_specs=[tok(up), tok(us), grp(kdim, 2 * STATE)],
        out_specs=[grp(kdim, mp), grp(kdim, 128), rows(mp), rows(bs)],
        out_shape=[jax.ShapeDtypeStruct((g, kdim, mp), BF16), jax.ShapeDtypeStruct((g, kdim, 128), BF16),
                   jax.ShapeDtypeStruct((g * mp, 2 * STATE), F32),
                   jax.ShapeDtypeStruct((g * bs, 2 * STATE), F32)],
        compiler_params=_params("arbitrary"),
        name="ssm_state",
    )(up, us, wst)

    arp = jnp.repeat(a2[:, 0, :], bp, axis=0)
    aip = jnp.repeat(a2[:, 1, :], bp, axis=0)
    ars = jnp.repeat(a2[:, 0, :], bs, axis=0)
    ais = jnp.repeat(a2[:, 1, :], bs, axis=0)
    full = lambda r: pl.BlockSpec((r, 2 * STATE), lambda i: (0, 0))
    hin, hfp, hfs = pl.pallas_call(
        functools.partial(_ssm_scan_kernel, nchunk=nchunk, rows=g * bp),
        grid=(1,),
        in_specs=[full(g * mp), full(g * bs), full(g * bp), full(g * bp), full(g * bs), full(g * bs),
                  full(g * bs)],
        out_specs=[full(g * mp), full(g * bp), full(g * bs)],
        out_shape=[jax.ShapeDtypeStruct((g * mp, 2 * STATE), F32),
                   jax.ShapeDtypeStruct((g * bp, 2 * STATE), F32),
                   jax.ShapeDtypeStruct((g * bs, 2 * STATE), F32)],
        compiler_params=_params("arbitrary"),
        name="ssm_scan",
    )(sp, ss, arp, aip, ars, ais, h0s)

    yp, ys = pl.pallas_call(
        functools.partial(_ssm_out_kernel, mp=mp, bs=bs),
        grid=(g // ng,),
        in_specs=[grp(kdim, mp), grp(kdim, 128), grp(N_BLK, BLK, BLK), grp(kdim, 2 * STATE), rows(mp), rows(bs)],
        out_specs=[tok(up), tok(us)],
        out_shape=[jax.ShapeDtypeStruct(up.shape, F32), jax.ShapeDtypeStruct(us.shape, F32)],
        scratch_shapes=[pltpu.VMEM((STEPS_PER_BLK, 128, mp), F32), pltpu.VMEM((STEPS_PER_BLK, 128, 128), F32)],
        compiler_params=_params("arbitrary"),
        name="ssm_out",
    )(xtp, xts, blk, wout, hin, h0s)
    return yp, ys, hfp, hfs


def _attn_kernel(sinks_ref, q_ref, kp_ref, kc_ref, vp_ref, vc_ref, *rest, qp, kw, masked):
    n_ride = (len(rest) - 1) // 2
    o_ref = rest[n_ride]
    _run_riders(rest[:n_ride], rest[n_ride + 1:])
    i = pl.program_id(1)
    n_units = q_ref.shape[0] // qp
    gw = Q_PER_KV * HEAD_DIM
    half = HEAD_DIM
    nl = Q_PER_KV * qp

    def duplicated(prev_ref, cur_ref):
        a = jnp.concatenate([prev_ref[...], cur_ref[...]], axis=0)
        lo = lax.broadcasted_iota(jnp.int32, (a.shape[0], 2 * half), 1) < half
        out = []
        for c in range(N_KV // 2):
            col = a[:, c * 2 * half:(c + 1) * 2 * half]
            rot = pltpu.roll(col, half, axis=1)
            out.append(jnp.where(lo, col, rot).astype(BF16))
            out.append(jnp.where(lo, rot, col).astype(BF16))
        return out

    kdup = duplicated(kp_ref, kc_ref)
    vdup = duplicated(vp_ref, vc_ref)
    pair_lo = lax.broadcasted_iota(jnp.int32, (qp, 2 * half), 1) < half
    lane_g = lax.broadcasted_iota(jnp.int32, (1, nl), 1) // qp

    for u in range(n_units):
        rows = slice(u * qp, (u + 1) * qp)
        win = slice(u * qp, u * qp + kw)
        if masked:
            kc = lax.broadcasted_iota(jnp.int32, (kw, nl), 0) // CHUNK
            first = WINDOW // CHUNK - (i * n_units + u) * (qp // CHUNK)
            valid = kc >= first
            if qp > CHUNK:
                qc = (lax.broadcasted_iota(jnp.int32, (kw, nl), 1) % qp) // CHUNK
                valid = valid & (kc >= qc) & (kc <= qc + WINDOW // CHUNK)
            bias = jnp.where(valid, 0.0, NEG)
        scores = []
        for kv in range(N_KV):
            qm = []
            for g in range(Q_PER_KV):
                pair = q_ref[rows, kv * gw + (g // 2) * 2 * half:kv * gw + (g // 2 + 1) * 2 * half]
                keep = pair_lo if g % 2 == 0 else jnp.logical_not(pair_lo)
                qm.append(jnp.where(keep, pair, jnp.zeros_like(pair)))
            scores.append(_dot_nt(kdup[kv][win], jnp.concatenate(qm, axis=0)))
        probs = []
        for kv in range(N_KV):
            s = scores[kv] + bias if masked else scores[kv]
            sink = jnp.zeros((1, nl), F32)
            for g in range(Q_PER_KV):
                sink = jnp.where(lane_g == g, sinks_ref[kv * Q_PER_KV + g], sink)
            m = jnp.maximum(jnp.max(s, axis=0, keepdims=True), sink)
            p = jnp.exp2(s - m)
            den = jnp.sum(p, axis=0, keepdims=True) + jnp.exp2(sink - m)
            probs.append((p * (1.0 / den)).astype(BF16))
        for kv in range(N_KV):
            o = lax.dot_general(probs[kv], vdup[kv][win], (((0,), (0,)), ((), ())),
                                preferred_element_type=F32)
            for h in range(Q_PER_KV // 2):
                both = jnp.where(pair_lo, o[2 * h * qp:(2 * h + 1) * qp], o[(2 * h + 1) * qp:(2 * h + 2) * qp])
                o_ref[rows, kv * gw + h * 2 * half:kv * gw + (h + 1) * 2 * half] = both.astype(o_ref.dtype)


def _attention(sinks, q, k_prev, k_cur, v_prev, v_cur, qt, qp, masked, ride=()):
    b, l, _ = q.shape
    nt = l // qt
    r_in, r_out, r_shape = _cast_riders(ride, b * nt, lambda bi, i: bi * nt + i)
    per = qt // WINDOW
    kw = WINDOW + qp
    if masked:
        prev_map = lambda bi, i: (bi, jnp.maximum(i * per - 1, 0), 0)
    else:
        prev_map = lambda bi, i: (bi, 0, 0)
    cur = lambda w: pl.BlockSpec((None, qt, w), lambda bi, i: (bi, i, 0))
    prev = pl.BlockSpec((None, WINDOW, D_KV), prev_map)
    return pl.pallas_call(
        functools.partial(_attn_kernel, qp=qp, kw=kw, masked=masked),
        grid=(b, l // qt),
        in_specs=[pl.BlockSpec(memory_space=pltpu.SMEM), cur(D_ATTN), prev, cur(D_KV), prev, cur(D_KV)] + r_in,
        out_specs=[cur(D_ATTN)] + r_out,
        out_shape=[jax.ShapeDtypeStruct((b, l, D_ATTN), BF16)] + r_shape,
        compiler_params=_params("arbitrary", "arbitrary"),
        name="attention",
    )(sinks, q, k_prev, k_cur, v_prev, v_cur, *ride)


def _mix_kernel(x_ref, y_ref, o_ref, n1_ref, wglu_ref, wga_ref, wgb_ref, wbs_ref, wba_ref, wout_ref,
                h_ref, xn_s, ys_s):
    j = pl.program_id(1)

    @pl.when(j == 0)
    def _():
        x = x_ref[...]
        xn_s[...] = _rms(x, n1_ref[...]).astype(BF16)
        y = jnp.concatenate([y_ref[c * PITCH:c * PITCH + CHUNK, :] for c in range(x.shape[0] // CHUNK)], axis=0)
        ya = jax.nn.gelu(y)
        ys_s[...] = (ya * jax.nn.sigmoid(_dot(ya.astype(BF16), wglu_ref[...]))).astype(BF16)
        h_ref[...] = x

    xn = xn_s[...]
    ga = jax.nn.sigmoid(_dot(xn, wga_ref[...]))
    gb = jax.nn.sigmoid(_dot(xn, wgb_ref[...]))
    mixed = ga * _dot(ys_s[...], wbs_ref[...]) + gb * _dot(o_ref[...], wba_ref[...])
    h_ref[...] += _dot(mixed.astype(BF16), wout_ref[...])


def _mix(x2d, y2d, o2d, norm1, wglu, wgate, wbs, wba, wout, tm, tn):
    t = x2d.shape[0]
    nj = D_MODEL // tn
    tok = lambda w: pl.BlockSpec((tm, w), lambda i, j: (i, 0))
    return pl.pallas_call(
        _mix_kernel,
        grid=(t // tm, nj),
        in_specs=[tok(D_MODEL), pl.BlockSpec((tm // CHUNK * PITCH, D_SSM), lambda i, j: (i, 0)), tok(D_ATTN),
                  pl.BlockSpec((1, D_MODEL), lambda i, j: (0, 0)),
                  pl.BlockSpec((D_SSM, D_SSM), lambda i, j: (0, 0)),
                  pl.BlockSpec((D_MODEL, tn), lambda i, j: (0, j)),
                  pl.BlockSpec((D_MODEL, tn), lambda i, j: (0, nj + j)),
                  pl.BlockSpec((D_SSM, tn), lambda i, j: (0, j)),
                  pl.BlockSpec((D_ATTN, tn), lambda i, j: (0, j)),
                  pl.BlockSpec((tn, D_MODEL), lambda i, j: (j, 0))],
        out_specs=tok(D_MODEL),
        out_shape=jax.ShapeDtypeStruct((t, D_MODEL), F32),
        scratch_shapes=[pltpu.VMEM((tm, D_MODEL), BF16), pltpu.VMEM((tm, D_SSM), BF16)],
        compiler_params=_params("arbitrary", "arbitrary"),
        name="mix",
    )(x2d, y2d, o2d, norm1, wglu, wgate, wgate, wbs, wba, wout)


def _ffn_kernel(h_ref, n2_ref, wg_ref, wu_ref, wd_ref, out_ref, hn_s):
    j = pl.program_id(1)

    @pl.when(j == 0)
    def _():
        h = h_ref[...]
        hn_s[...] = _rms(h, n2_ref[...]).astype(BF16)
        out_ref[...] = h

    hn = hn_s[...]
    act = jax.nn.silu(_dot(hn, wg_ref[...])) * _dot(hn, wu_ref[...])
    out_ref[...] += _dot(act.astype(BF16), wd_ref[...])


def _ffn(h2d, norm2, wg, wu, wd, tm, tf):
    t = h2d.shape[0]
    d_ff = wg.shape[1]
    tok = pl.BlockSpec((tm, D_MODEL), lambda i, j: (i, 0))
    return pl.pallas_call(
        _ffn_kernel,
        grid=(t // tm, d_ff // tf),
        in_specs=[tok, pl.BlockSpec((1, D_MODEL), lambda i, j: (0, 0)),
                  pl.BlockSpec((D_MODEL, tf), lambda i, j: (0, j)),
                  pl.BlockSpec((D_MODEL, tf), lambda i, j: (0, j)),
                  pl.BlockSpec((tf, D_MODEL), lambda i, j: (j, 0))],
        out_specs=tok,
        out_shape=jax.ShapeDtypeStruct((t, D_MODEL), F32),
        scratch_shapes=[pltpu.VMEM((tm, D_MODEL), BF16)],
        compiler_params=_params("arbitrary", "arbitrary"),
        name="ffn",
    )(h2d, norm2, wg, wu, wd)


def kernel(x_prompt, x_sample, cache_k, cache_v, state_ssm_re, state_ssm_im, norm1, w_in, q_norm, k_norm,
           sinks, ssm_a_re, ssm_a_im, ssm_log_dt, ssm_b_re, ssm_b_im, ssm_c_re, ssm_c_im, ssm_d, w_glu,
           w_br_ssm, w_br_attn, w_gate, w_out, norm2, w_ffn_gate, w_ffn_up, w_ffn_down):
    bp, lp, _ = x_prompt.shape
    bs, ls, _ = x_sample.shape
    ncp = lp // CHUNK
    assert ls == CHUNK and cache_k.shape[2] == WINDOW and norm1.shape[0] == 1
    l = 0

    f32 = lambda a: a[l].astype(F32)
    n1 = norm1[l].astype(F32).reshape(1, D_MODEL)
    n2 = norm2[l].astype(F32).reshape(1, D_MODEL)
    qg = jnp.tile(q_norm[l].astype(F32), 2).reshape(1, 128)
    kg = jnp.tile(k_norm[l].astype(F32), 2).reshape(1, 128)
    sink = sinks[l].astype(F32) * LOG2E

    blk, wst, wout_t, a2, w_in_b = _ssm_prep(ssm_a_re[l], ssm_a_im[l], ssm_log_dt[l], ssm_b_re[l], ssm_b_im[l],
                                             ssm_c_re[l], ssm_c_im[l], ssm_d[l], f32(w_in))

    xp = x_prompt.astype(F32).reshape(bp * lp, D_MODEL)
    xs = x_sample.astype(F32).reshape(bs * ls, D_MODEL)
    tm = 512
    up, qp, kp, vp, wfg_b, wfu_b, wfd_b = _in_proj(xp, np.arange(lp), tm, n1, w_in_b, qg, kg,
                                                   ride=(f32(w_ffn_gate), f32(w_ffn_up), f32(w_ffn_down)))
    us, qs, ks, vs = _in_proj(xs, np.tile(PAST_LEN + np.arange(ls), tm // ls), tm, n1, w_in_b, qg, kg)

    h0s = jnp.concatenate([state_ssm_re[l], state_ssm_im[l]], axis=-1).astype(F32)
    h0s = jnp.swapaxes(h0s, 0, 1).reshape(N_GROUPS * bs, 2 * STATE)
    y_p, y_s, hfp, hfs = _ssm(up, us, blk, wst, wout_t, a2, h0s, bp, ncp, bs)

    def states(hf, b):
        hf = jnp.swapaxes(hf.reshape(N_GROUPS, b, 2 * STATE), 0, 1)
        return hf[None, :, :, :STATE], hf[None, :, :, STATE:]

    kp3, vp3 = kp.reshape(bp, lp, D_KV), vp.reshape(bp, lp, D_KV)
    o_p, wgate_b, wout_b, wbs_b, wba_b, wglu_b = _attention(
        sink, qp.reshape(bp, lp, D_ATTN), kp3, kp3, vp3, vp3, 256, 2 * CHUNK, True,
        ride=(f32(w_gate), f32(w_out), f32(w_br_ssm), f32(w_br_attn), f32(w_glu)))
    ck = cache_k[l].astype(F32).reshape(bs, WINDOW, D_KV)
    cv = cache_v[l].astype(F32).reshape(bs, WINDOW, D_KV)
    ks3, vs3 = ks.reshape(bs, ls, D_KV), vs.reshape(bs, ls, D_KV)
    o_s, = _attention(sink, qs.reshape(bs, ls, D_ATTN), ck, ks3, cv, vs3, ls, ls, False)

    outs = []
    for x2d, y2d, o3d in ((xp, y_p, o_p), (xs, y_s, o_s)):
        h = _mix(x2d, y2d, o3d.reshape(-1, D_ATTN), n1, wglu_b, wgate_b, wbs_b, wba_b, wout_b, 512, 512)
        outs.append(_ffn(h, n2, wfg_b, wfu_b, wfd_b, 1024, 512))

    win = lambda a, b, n: a.reshape(b, n, N_KV, HEAD_DIM)[None]
    k_win_p = win(kp3[:, lp - WINDOW:], bp, WINDOW)
    v_win_p = win(vp3[:, lp - WINDOW:], bp, WINDOW)
    k_win_s = win(jnp.concatenate([ck, ks3], axis=1)[:, ls:], bs, WINDOW)
    v_win_s = win(jnp.concatenate([cv, vs3], axis=1)[:, ls:], bs, WINDOW)
    re_p, im_p = states(hfp, bp)
    re_s, im_s = states(hfs, bs)
    dt = x_prompt.dtype
    return (outs[0].reshape(bp, lp, D_MODEL).astype(dt), outs[1].reshape(bs, ls, D_MODEL).astype(dt),
            k_win_p.astype(dt), v_win_p.astype(dt), re_p.astype(dt), im_p.astype(dt),
            k_win_s.astype(dt), v_win_s.astype(dt), re_s.astype(dt), im_s.astype(dt))
```

```python
import functools

import jax
import jax.numpy as jnp
import numpy as np
from jax import lax
from jax.experimental import pallas as pl
from jax.experimental.pallas import tpu as pltpu

D_MODEL = 2048
CHUNK = 64
D_SSM = 1024
GROUP_CH = 16
N_GROUPS = 64
STATE = 64
HEAD_DIM = 64
N_HEADS = 16
N_KV = 4
Q_PER_KV = 4
D_ATTN = 1024
D_KV = 256
WINDOW = 128
ROT_DIM = 16
ROPE_THETA = 500000.0
PAST_LEN = 1024
EPS = 1e-6
NEG = -1e30
LOG2E = 1.4426950408889634

BLK = 256
PITCH = CHUNK + 8
STEPS_PER_BLK = BLK // GROUP_CH
N_BLK = CHUNK // STEPS_PER_BLK
V7X_VMEM_LIMIT = 56 * 1024 * 1024

F32 = jnp.float32
BF16 = jnp.bfloat16


def _dot(a, b):
    return jnp.dot(a, b, preferred_element_type=F32)


def _dot_nt(a, b):
    return lax.dot_general(a, b, (((1,), (1,)), ((), ())), preferred_element_type=F32)


def _params(*sem):
    return pltpu.CompilerParams(dimension_semantics=sem, vmem_limit_bytes=V7X_VMEM_LIMIT)


def _cast_riders(weights, nsteps, step_of):
    in_specs, out_specs, out_shape = [], [], []
    for w in weights:
        rows, cols = w.shape
        slab = rows // nsteps
        assert slab * nsteps == rows and slab % 16 == 0, (w.shape, nsteps)
        spec = pl.BlockSpec((slab, cols), lambda *ids: (step_of(*ids), 0))
        in_specs.append(spec)
        out_specs.append(spec)
        out_shape.append(jax.ShapeDtypeStruct(w.shape, BF16))
    return in_specs, out_specs, out_shape


def _run_riders(src_refs, dst_refs):
    for s, d in zip(src_refs, dst_refs):
        d[...] = s[...].astype(d.dtype)


def _ssm_prep_kernel(*refs):
    ins, wf_ref, outs, wb_ref, scratch = refs[:8], refs[8], refs[9:13], refs[13], refs[14:]
    _run_riders([wf_ref], [wb_ref])
    for g in range(ins[0].shape[0]):
        _ssm_prep_group(*[r.at[g] for r in ins + outs + scratch])


def _ssm_prep_group(are_ref, aim_ref, ldt_ref, bt_ref, bts_ref, c_ref, cs_ref, d_ref,
                    blk_ref, wst_ref, wout_ref, a2_ref, l_s, z_s, w_s):
    lane = lax.broadcasted_iota(jnp.int32, (1, 2 * STATE), 1)
    sg = jnp.where(lane < STATE, 1.0, -1.0).astype(F32)
    a_re = are_ref[...]
    a_im = aim_ref[...]
    dt = jnp.exp(ldt_ref[...])
    lre = a_re * dt
    lim = a_im * dt

    def cpow(e):
        mag = jnp.exp(e * lre)
        ang = e * lim
        return mag * jnp.cos(ang), mag * jnp.sin(ang)

    def cmul(ar, ai, br, bi):
        return ar * br - ai * bi, ar * bi + ai * br

    col = lax.broadcasted_iota(jnp.int32, (STEPS_PER_BLK, 1), 0).astype(F32)
    t_r, t_i = cpow(col)
    r_r, r_i = cpow(float(STEPS_PER_BLK - 1) - col)
    h_r, h_i = cpow(float(STEPS_PER_BLK) * col[:8])
    ab_r, ab_i = t_r[1:2], t_i[1:2]

    fr, fi = ab_r - 1.0, ab_i
    den = a_re * a_re + a_im * a_im
    cr = (fr * a_re + fi * a_im) / den
    ci = (fi * a_re - fr * a_im) / den
    bb = cr * bt_ref[...] - (ci * sg) * bts_ref[...]
    bbs = cr * bts_ref[...] + (ci * sg) * bt_ref[...]
    cc = c_ref[...]
    ccs = cs_ref[...]

    def outer_rows(pr, pi, m, ms, dst, row0, conj):
        pa = pr * sg if conj else pr
        pb = -pi if conj else -(pi * sg)
        for r in range(pr.shape[0]):
            z = pa[r:r + 1, :] * m + pb[r:r + 1, :] * ms
            dst[row0 + r * GROUP_CH:row0 + (r + 1) * GROUP_CH, :] = z

    e_r, e_i = cmul(ab_r, ab_i, r_r, r_i)
    outer_rows(e_r, e_i, bb, bbs, l_s, 0, False)
    l2 = l_s[...].astype(BF16)

    for d in range(N_BLK - 1):
        e_r, e_i = cmul(h_r[d:d + 1], h_i[d:d + 1], t_r, t_i)
        outer_rows(e_r, e_i, cc, ccs, z_s, 0, True)
        blk_ref[d + 1] = _dot_nt(z_s[...].astype(BF16), l2).astype(BF16)

    outer_rows(r_r, r_i, bb, bbs, z_s, 0, False)
    kr = _dot_nt((cc * sg).astype(BF16), z_s[...].astype(BF16))
    rows = lax.broadcasted_iota(jnp.int32, (GROUP_CH, BLK), 0)
    lanes = lax.broadcasted_iota(jnp.int32, (GROUP_CH, BLK), 1)
    kr = kr + jnp.where(lanes == rows + (BLK - GROUP_CH), d_ref[...], 0.0)
    for t0 in range(STEPS_PER_BLK):
        sh = GROUP_CH * (STEPS_PER_BLK - 1 - t0)
        piece = kr if sh == 0 else jnp.where(lanes < BLK - sh, pltpu.roll(kr, BLK - sh, axis=1), 0.0)
        blk_ref[0, t0 * GROUP_CH:(t0 + 1) * GROUP_CH, :] = piece.astype(BF16)

    for sb in range(N_BLK):
        m = N_BLK - 1 - sb
        e_r, e_i = cmul(h_r[m:m + 1], h_i[m:m + 1], r_r, r_i)
        outer_rows(e_r, e_i, bb, bbs, w_s, sb * BLK, False)
    wst_ref[...] = w_s[...].astype(BF16)

    for tb in range(N_BLK):
        e_r, e_i = cmul(*cmul(ab_r, ab_i, h_r[tb:tb + 1], h_i[tb:tb + 1]), t_r, t_i)
        outer_rows(e_r, e_i, cc, ccs, w_s, tb * BLK, True)
    wout_ref[...] = w_s[...].astype(BF16)

    a2_ref[0:1, :] = h_r[N_BLK:N_BLK + 1]
    a2_ref[1:2, :] = -(h_i[N_BLK:N_BLK + 1] * sg)


def _ssm_prep(a_re, a_im, log_dt, b_re, b_im, c_re, c_im, d_skip, ride):
    g = N_GROUPS
    ng = 8
    r_in, r_out, r_shape = _cast_riders([ride], g // ng, lambda i: i)
    f = lambda a: a.astype(F32)
    dup = lambda a: jnp.concatenate([f(a), f(a)], axis=-1).reshape(g, 1, 2 * STATE)
    bt_re = jnp.swapaxes(f(b_re), 1, 2)
    bt_im = jnp.swapaxes(f(b_im), 1, 2)
    bt = jnp.concatenate([bt_re, bt_im], axis=-1)
    bts = jnp.concatenate([bt_im, bt_re], axis=-1)
    cc = jnp.concatenate([f(c_re), f(c_im)], axis=-1)
    ccs = jnp.concatenate([f(c_im), f(c_re)], axis=-1)
    d_pad = jnp.pad(f(d_skip).reshape(g, 1, GROUP_CH), ((0, 0), (0, 0), (BLK - GROUP_CH, 0)))
    vec = lambda n: pl.BlockSpec((ng, 1, n), lambda i: (i, 0, 0))
    mat = pl.BlockSpec((ng, GROUP_CH, 2 * STATE), lambda i: (i, 0, 0))
    kdim = CHUNK * GROUP_CH
    return pl.pallas_call(
        _ssm_prep_kernel,
        grid=(g // ng,),
        in_specs=[vec(2 * STATE), vec(2 * STATE), vec(1), mat, mat, mat, mat, vec(BLK)] + r_in,
        out_specs=[pl.BlockSpec((ng, N_BLK, BLK, BLK), lambda i: (i, 0, 0, 0)),
                   pl.BlockSpec((ng, kdim, 2 * STATE), lambda i: (i, 0, 0)),
                   pl.BlockSpec((ng, kdim, 2 * STATE), lambda i: (i, 0, 0)),
                   pl.BlockSpec((ng, 2, 2 * STATE), lambda i: (i, 0, 0))] + r_out,
        out_shape=[jax.ShapeDtypeStruct((g, N_BLK, BLK, BLK), BF16),
                   jax.ShapeDtypeStruct((g, kdim, 2 * STATE), BF16),
                   jax.ShapeDtypeStruct((g, kdim, 2 * STATE), BF16),
                   jax.ShapeDtypeStruct((g, 2, 2 * STATE), F32)] + r_shape,
        scratch_shapes=[pltpu.VMEM((ng, BLK, 2 * STATE), F32), pltpu.VMEM((ng, BLK, 2 * STATE), F32),
                        pltpu.VMEM((ng, kdim, 2 * STATE), F32)],
        compiler_params=_params("arbitrary"),
        name="ssm_prep",
    )(dup(a_re), dup(a_im), f(log_dt).reshape(g, 1, 1), bt, bts, cc, ccs, d_pad, ride)


def _rms(x, g):
    return x * lax.rsqrt(jnp.mean(x * x, axis=-1, keepdims=True) + EPS) * g


def _in_proj_kernel(x_ref, n1_ref, w_ref, qg_ref, kg_ref, rc_ref, rm_ref, rp_ref, *rest):
    n_ride = (len(rest) - 4) // 2
    u_ref, q_ref, k_ref, v_ref = rest[n_ride:n_ride + 4]
    _run_riders(rest[:n_ride], rest[n_ride + 4:])
    hr = x_ref.shape[0] // 2
    lo = lax.broadcasted_iota(jnp.int32, (hr, 128), 1) < HEAD_DIM

    for r in range(2):
        rows = slice(r * hr, (r + 1) * hr)
        xn = _rms(x_ref[rows, :], n1_ref[...]).astype(BF16)
        rc = rc_ref[rows, :]
        rm = rm_ref[rows, :]
        rp = rp_ref[rows, :]

        def norm_rope(z, gain, width, scale):
            outs = []
            for cb in range(width // 128):
                zb = z[:, cb * 128:(cb + 1) * 128]
                sq = zb * zb
                s_lo = jnp.sum(jnp.where(lo, sq, 0.0), axis=-1, keepdims=True)
                s_hi = jnp.sum(jnp.where(lo, 0.0, sq), axis=-1, keepdims=True)
                ms = jnp.where(lo, s_lo, s_hi) * (1.0 / HEAD_DIM)
                zn = zb * lax.rsqrt(ms + EPS) * gain
                rot = zn * rc + pltpu.roll(zn, 128 - ROT_DIM // 2, axis=1) * rm \
                    + pltpu.roll(zn, ROT_DIM // 2, axis=1) * rp
                outs.append(rot * scale)
            return outs

        q = _dot(xn, w_ref[:, D_SSM:D_SSM + D_ATTN])
        k = _dot(xn, w_ref[:, D_SSM + D_ATTN:D_SSM + D_ATTN + D_KV])
        for cb, blk in enumerate(norm_rope(q, qg_ref[...], D_ATTN, HEAD_DIM ** -0.5 * LOG2E)):
            q_ref[rows, cb * 128:(cb + 1) * 128] = blk.astype(q_ref.dtype)
        for cb, blk in enumerate(norm_rope(k, kg_ref[...], D_KV, 1.0)):
            k_ref[rows, cb * 128:(cb + 1) * 128] = blk
        u = _dot(xn, w_ref[:, 0:D_SSM])
        for c in range(hr // CHUNK):
            base = (r * (hr // CHUNK) + c) * PITCH
            u_ref[base:base + CHUNK, :] = u[c * CHUNK:(c + 1) * CHUNK]
            u_ref[base + CHUNK:base + PITCH, :] = jnp.zeros((PITCH - CHUNK, D_SSM), F32)
        v_ref[rows, :] = _dot(xn, w_ref[:, D_SSM + D_ATTN + D_KV:])


def _rope_tables(pos):
    half = ROT_DIM // 2
    inv = ROPE_THETA ** (-np.arange(half, dtype=np.float64) * 2.0 / ROT_DIM)
    ang = pos.astype(np.float64)[:, None] * inv[None, :]
    cos, sin = np.cos(ang), np.sin(ang)
    n = pos.shape[0]
    ones = np.ones((n, HEAD_DIM - ROT_DIM))
    zeros = np.zeros((n, HEAD_DIM - ROT_DIM))
    zh = np.zeros((n, half))
    rc = np.concatenate([cos, cos, ones], axis=1)
    rm = np.concatenate([-sin, zh, zeros], axis=1)
    rp = np.concatenate([zh, sin, zeros], axis=1)
    two = lambda t: jnp.asarray(np.concatenate([t, t], axis=1), F32)
    return two(rc), two(rm), two(rp)


def _in_proj(x2d, pos, tm, norm1, w_in_b, qg, kg, ride=()):
    t = x2d.shape[0]
    r_in, r_out, r_shape = _cast_riders(ride, t // tm, lambda i: i)
    rc, rm, rp = _rope_tables(pos)
    nrope = pos.shape[0] // tm
    rope_spec = pl.BlockSpec((tm, 128), lambda i: (i % nrope, 0))
    const = lambda shape: pl.BlockSpec(shape, lambda i: (0, 0))
    tok = lambda w: pl.BlockSpec((tm, w), lambda i: (i, 0))
    return pl.pallas_call(
        _in_proj_kernel,
        grid=(t // tm,),
        in_specs=[tok(D_MODEL), const((1, D_MODEL)), const(w_in_b.shape),
                  const((1, 128)), const((1, 128)), rope_spec, rope_spec, rope_spec] + r_in,
        out_specs=[pl.BlockSpec((tm // CHUNK * PITCH, D_SSM), lambda i: (i, 0)), tok(D_ATTN), tok(D_KV),
                   tok(D_KV)] + r_out,
        out_shape=[jax.ShapeDtypeStruct((t // CHUNK * PITCH, D_SSM), F32), jax.ShapeDtypeStruct((t, D_ATTN), BF16),
                   jax.ShapeDtypeStruct((t, D_KV), F32), jax.ShapeDtypeStruct((t, D_KV), F32)] + r_shape,
        compiler_params=_params("arbitrary"),
        name="in_proj",
    )(x2d, norm1, w_in_b, qg, kg, rc, rm, rp, *ride)


def _ssm_state_kernel(up_ref, us_ref, wst_ref, xtp_ref, xts_ref, sp_ref, ss_ref, *, mp, bs):
    ng = xtp_ref.shape[0]
    pad = jnp.zeros((128 - bs, 128), F32)
    for s in range(CHUNK):
        vp = up_ref[pl.ds(s, mp, stride=PITCH), :].T.astype(BF16)
        vs = jnp.concatenate([us_ref[pl.ds(s, bs, stride=PITCH), :], pad], axis=0).T.astype(BF16)
        for g in range(ng):
            xtp_ref[g, s * GROUP_CH:(s + 1) * GROUP_CH, :] = vp[g * GROUP_CH:(g + 1) * GROUP_CH, :]
            xts_ref[g, s * GROUP_CH:(s + 1) * GROUP_CH, :] = vs[g * GROUP_CH:(g + 1) * GROUP_CH, :]
    tn = (((0,), (0,)), ((), ()))
    for g in range(ng):
        st = lax.dot_general(wst_ref[g], xtp_ref[g], tn, preferred_element_type=F32)
        sp_ref[g * mp:(g + 1) * mp, :] = st.T
        st = lax.dot_general(wst_ref[g], xts_ref[g], tn, preferred_element_type=F32)
        ss_ref[g * bs:(g + 1) * bs, :] = st.T[:bs]


def _cmul_add(h, ar2, ai2, s):
    return h * ar2 + pltpu.roll(h, STATE, axis=1) * ai2 + s


def _ssm_out_kernel(xtp_ref, xts_ref, blk_ref, wout_ref, sp_ref, ss_ref, arp_ref, aip_ref, ars_ref, ais_ref,
                    h0s_ref, yp_ref, ys_ref, hfp_ref, hfs_ref, hin_s, stp_s, sts_s, *, mp, bs, nchunk):
    ng = xtp_ref.shape[0]
    rows = mp // nchunk * ng
    ar = arp_ref[...]
    ai = aip_ref[...]
    ai_sw = pltpu.roll(ai, STATE, axis=1)
    s_all = [sp_ref[pl.ds(k, rows, stride=nchunk), :] for k in range(nchunk)]
    s_sw = [pltpu.roll(s, STATE, axis=1) for s in s_all]
    h = jnp.zeros((rows, 2 * STATE), F32)
    h_sw = h
    for k in range(nchunk):
        hin_s[pl.ds(k, rows, stride=nchunk), :] = h
        h, h_sw = h * ar + h_sw * ai + s_all[k], h_sw * ar + h * ai_sw + s_sw[k]
    hfp_ref[...] = h
    hfs_ref[...] = _cmul_add(h0s_ref[...], ars_ref[...], ais_ref[...], ss_ref[...])

    pad = jnp.zeros((128 - bs, 2 * STATE), F32)
    hp = [hin_s[g * mp:(g + 1) * mp, :].astype(BF16) for g in range(ng)]
    hs = [jnp.concatenate([h0s_ref[g * bs:(g + 1) * bs, :], pad], axis=0).astype(BF16) for g in range(ng)]
    for tb in range(N_BLK):
        for g in range(ng):
            accp = accs = None
            for sb in range(tb + 1):
                b = blk_ref[g, tb - sb]
                dp = _dot(b, xtp_ref[g, sb * BLK:(sb + 1) * BLK, :])
                ds = _dot(b, xts_ref[g, sb * BLK:(sb + 1) * BLK, :])
                accp = dp if accp is None else accp + dp
                accs = ds if accs is None else accs + ds
            w = wout_ref[g, tb * BLK:(tb + 1) * BLK, :]
            accp = accp + _dot_nt(w, hp[g])
            accs = accs + _dot_nt(w, hs[g])
            for t0 in range(STEPS_PER_BLK):
                stp_s[t0, g * GROUP_CH:(g + 1) * GROUP_CH, :] = accp[t0 * GROUP_CH:(t0 + 1) * GROUP_CH, :]
                sts_s[t0, g * GROUP_CH:(g + 1) * GROUP_CH, :] = accs[t0 * GROUP_CH:(t0 + 1) * GROUP_CH, :]
        for t0 in range(STEPS_PER_BLK):
            t = tb * STEPS_PER_BLK + t0
            yp_ref[pl.ds(t, mp, stride=PITCH), :] = stp_s[t0].T
            ys_ref[pl.ds(t, bs, stride=PITCH), :] = sts_s[t0].T[:bs]
    for r in range(CHUNK, PITCH):
        yp_ref[pl.ds(r, mp, stride=PITCH), :] = jnp.zeros((mp, 128), F32)
        ys_ref[pl.ds(r, bs, stride=PITCH), :] = jnp.zeros((bs, 128), F32)


def _ssm(up, us, blk, wst, wout, a2, h0s, bp, nchunk, bs):
    g = N_GROUPS
    ng = 128 // GROUP_CH
    mp = bp * nchunk
    kdim = CHUNK * GROUP_CH
    tok = lambda a: pl.BlockSpec((a.shape[0], 128), lambda i: (0, i))
    grp = lambda *tail: pl.BlockSpec((ng,) + tail, lambda i: (i,) + (0,) * len(tail))
    rows = lambda r: pl.BlockSpec((ng * r, 2 * STATE), lambda i: (i, 0))
    xtp, xts, sp, ss = pl.pallas_call(
        functools.partial(_ssm_state_kernel, mp=mp, bs=bs),
        grid=(g // ng,),
        in_specs=[tok(up), tok(us), grp(kdim, 2 * STATE)],
        out_specs=[grp(kdim, mp), grp(kdim, 128), rows(mp), rows(bs)],
        out_shape=[jax.ShapeDtypeStruct((g, kdim, mp), BF16), jax.ShapeDtypeStruct((g, kdim, 128), BF16),
                   jax.ShapeDtypeStruct((g * mp, 2 * STATE), F32),
                   jax.ShapeDtypeStruct((g * bs, 2 * STATE), F32)],
        compiler_params=_params("arbitrary"),
        name="ssm_state",
    )(up, us, wst)

    arp = jnp.repeat(a2[:, 0, :], bp, axis=0)
    aip = jnp.repeat(a2[:, 1, :], bp, axis=0)
    ars = jnp.repeat(a2[:, 0, :], bs, axis=0)
    ais = jnp.repeat(a2[:, 1, :], bs, axis=0)
    yp, ys, hfp, hfs = pl.pallas_call(
        functools.partial(_ssm_out_kernel, mp=mp, bs=bs, nchunk=nchunk),
        grid=(g // ng,),
        in_specs=[grp(kdim, mp), grp(kdim, 128), grp(N_BLK, BLK, BLK), grp(kdim, 2 * STATE), rows(mp), rows(bs),
                  rows(bp), rows(bp), rows(bs), rows(bs), rows(bs)],
        out_specs=[tok(up), tok(us), rows(bp), rows(bs)],
        out_shape=[jax.ShapeDtypeStruct(up.shape, F32), jax.ShapeDtypeStruct(us.shape, F32),
                   jax.ShapeDtypeStruct((g * bp, 2 * STATE), F32), jax.ShapeDtypeStruct((g * bs, 2 * STATE), F32)],
        scratch_shapes=[pltpu.VMEM((ng * mp, 2 * STATE), F32), pltpu.VMEM((STEPS_PER_BLK, 128, mp), F32),
                        pltpu.VMEM((STEPS_PER_BLK, 128, 128), F32)],
        compiler_params=_params("arbitrary"),
        name="ssm_out",
    )(xtp, xts, blk, wout, sp, ss, arp, aip, ars, ais, h0s)
    return yp, ys, hfp, hfs


def _attn_kernel(sinks_ref, bias_ref, q_ref, kp_ref, kc_ref, vp_ref, vc_ref, *rest, qp, kw, masked):
    n_ride = (len(rest) - 1) // 2
    o_ref = rest[n_ride]
    _run_riders(rest[:n_ride], rest[n_ride + 1:])
    i = pl.program_id(1)
    n_units = q_ref.shape[0] // qp
    gw = Q_PER_KV * HEAD_DIM
    half = HEAD_DIM
    nl = Q_PER_KV * qp

    def duplicated(prev_ref, cur_ref):
        a = jnp.concatenate([prev_ref[...], cur_ref[...]], axis=0)
        lo = lax.broadcasted_iota(jnp.int32, (a.shape[0], 2 * half), 1) < half
        out = []
        for c in range(N_KV // 2):
            col = a[:, c * 2 * half:(c + 1) * 2 * half]
            rot = pltpu.roll(col, half, axis=1)
            out.append(jnp.where(lo, col, rot).astype(BF16))
            out.append(jnp.where(lo, rot, col).astype(BF16))
        return out

    kdup = duplicated(kp_ref, kc_ref)
    vdup = duplicated(vp_ref, vc_ref)
    pair_lo = lax.broadcasted_iota(jnp.int32, (qp, 2 * half), 1) < half
    lane_g = lax.broadcasted_iota(jnp.int32, (1, nl), 1) // qp

    for u in range(n_units):
        rows = slice(u * qp, (u + 1) * qp)
        win = slice(u * qp, u * qp + kw)
        if masked:
            bias = bias_ref[jnp.where(i == 0, 0, 1)] if u == 0 else bias_ref[1]
        scores = []
        for kv in range(N_KV):
            qm = []
            for g in range(Q_PER_KV):
                pair = q_ref[rows, kv * gw + (g // 2) * 2 * half:kv * gw + (g // 2 + 1) * 2 * half]
                keep = pair_lo if g % 2 == 0 else jnp.logical_not(pair_lo)
                qm.append(jnp.where(keep, pair, jnp.zeros_like(pair)))
            scores.append(_dot_nt(kdup[kv][win], jnp.concatenate(qm, axis=0)))
        probs = []
        for kv in range(N_KV):
            s = scores[kv] + bias if masked else scores[kv]
            sink = jnp.zeros((1, nl), F32)
            for g in range(Q_PER_KV):
                sink = jnp.where(lane_g == g, sinks_ref[kv * Q_PER_KV + g], sink)
            m = jnp.maximum(jnp.max(s, axis=0, keepdims=True), sink)
            p = jnp.exp2(s - m)
            den = jnp.sum(p, axis=0, keepdims=True) + jnp.exp2(sink - m)
            probs.append((p * (1.0 / den)).astype(BF16))
        for kv in range(N_KV):
            o = lax.dot_general(probs[kv], vdup[kv][win], (((0,), (0,)), ((), ())),
                                preferred_element_type=F32)
            for h in range(Q_PER_KV // 2):
                both = jnp.where(pair_lo, o[2 * h * qp:(2 * h + 1) * qp], o[(2 * h + 1) * qp:(2 * h + 2) * qp])
                o_ref[rows, kv * gw + h * 2 * half:kv * gw + (h + 1) * 2 * half] = both.astype(o_ref.dtype)


def _attention(sinks, q, k_prev, k_cur, v_prev, v_cur, qt, qp, masked, ride=()):
    b, l, _ = q.shape
    nt = l // qt
    r_in, r_out, r_shape = _cast_riders(ride, b * nt, lambda bi, i: bi * nt + i)
    per = qt // WINDOW
    kw = WINDOW + qp
    kc = np.arange(kw)[:, None] // CHUNK
    qc = (np.arange(Q_PER_KV * qp)[None, :] % qp) // CHUNK
    band = (kc >= qc) & (kc <= qc + WINDOW // CHUNK)
    bias = np.stack([np.where(band & (kc >= WINDOW // CHUNK), 0.0, NEG), np.where(band, 0.0, NEG)])
    bias = jnp.asarray(bias, F32)
    if masked:
        prev_map = lambda bi, i: (bi, jnp.maximum(i * per - 1, 0), 0)
    else:
        prev_map = lambda bi, i: (bi, 0, 0)
    cur = lambda w: pl.BlockSpec((None, qt, w), lambda bi, i: (bi, i, 0))
    prev = pl.BlockSpec((None, WINDOW, D_KV), prev_map)
    return pl.pallas_call(
        functools.partial(_attn_kernel, qp=qp, kw=kw, masked=masked),
        grid=(b, l // qt),
        in_specs=[pl.BlockSpec(memory_space=pltpu.SMEM), pl.BlockSpec(bias.shape, lambda bi, i: (0, 0, 0)),
                  cur(D_ATTN), prev, cur(D_KV), prev, cur(D_KV)] + r_in,
        out_specs=[cur(D_ATTN)] + r_out,
        out_shape=[jax.ShapeDtypeStruct((b, l, D_ATTN), BF16)] + r_shape,
        compiler_params=_params("arbitrary", "arbitrary"),
        name="attention",
    )(sinks, bias, q, k_prev, k_cur, v_prev, v_cur, *ride)


def _mix_kernel(x_ref, y_ref, o_ref, n1_ref, wglu_ref, wga_ref, wgb_ref, wbs_ref, wba_ref, wout_ref,
                h_ref, xn_s, ys_s):
    j = pl.program_id(1)

    @pl.when(j == 0)
    def _():
        x = x_ref[...]
        xn_s[...] = _rms(x, n1_ref[...]).astype(BF16)
        y = jnp.concatenate([y_ref[c * PITCH:c * PITCH + CHUNK, :] for c in range(x.shape[0] // CHUNK)], axis=0)
        ya = jax.nn.gelu(y)
        ys_s[...] = (ya * jax.nn.sigmoid(_dot(ya.astype(BF16), wglu_ref[...]))).astype(BF16)
        h_ref[...] = x

    xn = xn_s[...]
    ga = jax.nn.sigmoid(_dot(xn, wga_ref[...]))
    gb = jax.nn.sigmoid(_dot(xn, wgb_ref[...]))
    mixed = ga * _dot(ys_s[...], wbs_ref[...]) + gb * _dot(o_ref[...], wba_ref[...])
    h_ref[...] += _dot(mixed.astype(BF16), wout_ref[...])


def _mix(x2d, y2d, o2d, norm1, wglu, wgate, wbs, wba, wout, tm, tn):
    t = x2d.shape[0]
    nj = D_MODEL // tn
    tok = lambda w: pl.BlockSpec((tm, w), lambda i, j: (i, 0))
    return pl.pallas_call(
        _mix_kernel,
        grid=(t // tm, nj),
        in_specs=[tok(D_MODEL), pl.BlockSpec((tm // CHUNK * PITCH, D_SSM), lambda i, j: (i, 0)), tok(D_ATTN),
                  pl.BlockSpec((1, D_MODEL), lambda i, j: (0, 0)),
                  pl.BlockSpec((D_SSM, D_SSM), lambda i, j: (0, 0)),
                  pl.BlockSpec((D_MODEL, tn), lambda i, j: (0, j)),
                  pl.BlockSpec((D_MODEL, tn), lambda i, j: (0, nj + j)),
                  pl.BlockSpec((D_SSM, tn), lambda i, j: (0, j)),
                  pl.BlockSpec((D_ATTN, tn), lambda i, j: (0, j)),
                  pl.BlockSpec((tn, D_MODEL), lambda i, j: (j, 0))],
        out_specs=tok(D_MODEL),
        out_shape=jax.ShapeDtypeStruct((t, D_MODEL), F32),
        scratch_shapes=[pltpu.VMEM((tm, D_MODEL), BF16), pltpu.VMEM((tm, D_SSM), BF16)],
        compiler_params=_params("arbitrary", "arbitrary"),
        name="mix",
    )(x2d, y2d, o2d, norm1, wglu, wgate, wgate, wbs, wba, wout)


def _ffn_kernel(h_ref, n2_ref, wg_ref, wu_ref, wd_ref, out_ref, hn_s):
    j = pl.program_id(1)

    @pl.when(j == 0)
    def _():
        h = h_ref[...]
        hn_s[...] = _rms(h, n2_ref[...]).astype(BF16)
        out_ref[...] = h

    hn = hn_s[...]
    act = jax.nn.silu(_dot(hn, wg_ref[...])) * _dot(hn, wu_ref[...])
    out_ref[...] += _dot(act.astype(BF16), wd_ref[...])


def _ffn(h2d, norm2, wg, wu, wd, tm, tf):
    t = h2d.shape[0]
    d_ff = wg.shape[1]
    tok = pl.BlockSpec((tm, D_MODEL), lambda i, j: (i, 0))
    return pl.pallas_call(
        _ffn_kernel,
        grid=(t // tm, d_ff // tf),
        in_specs=[tok, pl.BlockSpec((1, D_MODEL), lambda i, j: (0, 0)),
                  pl.BlockSpec((D_MODEL, tf), lambda i, j: (0, j)),
                  pl.BlockSpec((D_MODEL, tf), lambda i, j: (0, j)),
                  pl.BlockSpec((tf, D_MODEL), lambda i, j: (j, 0))],
        out_specs=tok,
        out_shape=jax.ShapeDtypeStruct((t, D_MODEL), F32),
        scratch_shapes=[pltpu.VMEM((tm, D_MODEL), BF16)],
        compiler_params=_params("arbitrary", "arbitrary"),
        name="ffn",
    )(h2d, norm2, wg, wu, wd)


def kernel(x_prompt, x_sample, cache_k, cache_v, state_ssm_re, state_ssm_im, norm1, w_in, q_norm, k_norm,
           sinks, ssm_a_re, ssm_a_im, ssm_log_dt, ssm_b_re, ssm_b_im, ssm_c_re, ssm_c_im, ssm_d, w_glu,
           w_br_ssm, w_br_attn, w_gate, w_out, norm2, w_ffn_gate, w_ffn_up, w_ffn_down):
    bp, lp, _ = x_prompt.shape
    bs, ls, _ = x_sample.shape
    ncp = lp // CHUNK
    assert ls == CHUNK and cache_k.shape[2] == WINDOW and norm1.shape[0] == 1
    l = 0

    f32 = lambda a: a[l].astype(F32)
    n1 = norm1[l].astype(F32).reshape(1, D_MODEL)
    n2 = norm2[l].astype(F32).reshape(1, D_MODEL)
    qg = jnp.tile(q_norm[l].astype(F32), 2).reshape(1, 128)
    kg = jnp.tile(k_norm[l].astype(F32), 2).reshape(1, 128)
    sink = sinks[l].astype(F32) * LOG2E

    blk, wst, wout_t, a2, w_in_b = _ssm_prep(ssm_a_re[l], ssm_a_im[l], ssm_log_dt[l], ssm_b_re[l], ssm_b_im[l],
                                             ssm_c_re[l], ssm_c_im[l], ssm_d[l], f32(w_in))

    xp = x_prompt.astype(F32).reshape(bp * lp, D_MODEL)
    xs = x_sample.astype(F32).reshape(bs * ls, D_MODEL)
    tm = 512
    up, qp, kp, vp, wfg_b, wfu_b, wfd_b = _in_proj(xp, np.arange(lp), tm, n1, w_in_b, qg, kg,
                                                   ride=(f32(w_ffn_gate), f32(w_ffn_up), f32(w_ffn_down)))
    us, qs, ks, vs = _in_proj(xs, np.tile(PAST_LEN + np.arange(ls), tm // ls), tm, n1, w_in_b, qg, kg)

    h0s = jnp.concatenate([state_ssm_re[l], state_ssm_im[l]], axis=-1).astype(F32)
    h0s = jnp.swapaxes(h0s, 0, 1).reshape(N_GROUPS * bs, 2 * STATE)
    y_p, y_s, hfp, hfs = _ssm(up, us, blk, wst, wout_t, a2, h0s, bp, ncp, bs)

    def states(hf, b):
        hf = jnp.swapaxes(hf.reshape(N_GROUPS, b, 2 * STATE), 0, 1)
        return hf[None, :, :, :STATE], hf[None, :, :, STATE:]

    kp3, vp3 = kp.reshape(bp, lp, D_KV), vp.reshape(bp, lp, D_KV)
    o_p, wgate_b, wout_b, wbs_b, wba_b, wglu_b = _attention(
        sink, qp.reshape(bp, lp, D_ATTN), kp3, kp3, vp3, vp3, 256, 2 * CHUNK, True,
        ride=(f32(w_gate), f32(w_out), f32(w_br_ssm), f32(w_br_attn), f32(w_glu)))
    ck = cache_k[l].astype(F32).reshape(bs, WINDOW, D_KV)
    cv = cache_v[l].astype(F32).reshape(bs, WINDOW, D_KV)
    ks3, vs3 = ks.reshape(bs, ls, D_KV), vs.reshape(bs, ls, D_KV)
    o_s, = _attention(sink, qs.reshape(bs, ls, D_ATTN), ck, ks3, cv, vs3, ls, ls, False)

    outs = []
    for x2d, y2d, o3d in ((xp, y_p, o_p), (xs, y_s, o_s)):
        h = _mix(x2d, y2d, o3d.reshape(-1, D_ATTN), n1, wglu_b, wgate_b, wbs_b, wba_b, wout_b, 512, 512)
        outs.append(_ffn(h, n2, wfg_b, wfu_b, wfd_b, 1024, 512))

    win = lambda a, b, n: a.reshape(b, n, N_KV, HEAD_DIM)[None]
    k_win_p = win(kp3[:, lp - WINDOW:], bp, WINDOW)
    v_win_p = win(vp3[:, lp - WINDOW:], bp, WINDOW)
    k_win_s = win(jnp.concatenate([ck, ks3], axis=1)[:, ls:], bs, WINDOW)
    v_win_s = win(jnp.concatenate([cv, vs3], axis=1)[:, ls:], bs, WINDOW)
    re_p, im_p = states(hfp, bp)
    re_s, im_s = states(hfs, bs)
    dt = x_prompt.dtype
    return (outs[0].reshape(bp, lp, D_MODEL).astype(dt), outs[1].reshape(bs, ls, D_MODEL).astype(dt),
            k_win_p.astype(dt), v_win_p.astype(dt), re_p.astype(dt), im_p.astype(dt),
            k_win_s.astype(dt), v_win_s.astype(dt), re_s.astype(dt), im_s.astype(dt))
```

```python
import functools

import jax
import jax.numpy as jnp
import numpy as np
from jax import lax
from jax.experimental import pallas as pl
from jax.experimental.pallas import tpu as pltpu

D_MODEL = 2048
CHUNK = 64
D_SSM = 1024
GROUP_CH = 16
N_GROUPS = 64
STATE = 64
HEAD_DIM = 64
N_HEADS = 16
N_KV = 4
Q_PER_KV = 4
D_ATTN = 1024
D_KV = 256
WINDOW = 128
ROT_DIM = 16
ROPE_THETA = 500000.0
PAST_LEN = 1024
EPS = 1e-6
NEG = -1e30
LOG2E = 1.4426950408889634

BLK = 256
PITCH = CHUNK + 8
STEPS_PER_BLK = BLK // GROUP_CH
N_BLK = CHUNK // STEPS_PER_BLK
V7X_VMEM_LIMIT = 56 * 1024 * 1024
TM_IN, TM_MIX, TN_MIX, TM_FFN, TF_FFN = 512, 512, 512, 1024, 512

F32 = jnp.float32
BF16 = jnp.bfloat16


def _dot(a, b):
    return jnp.dot(a, b, preferred_element_type=F32)


def _dot_nt(a, b):
    return lax.dot_general(a, b, (((1,), (1,)), ((), ())), preferred_element_type=F32)


def _params(*sem):
    return pltpu.CompilerParams(dimension_semantics=sem, vmem_limit_bytes=V7X_VMEM_LIMIT)


def _cast_riders(weights, nsteps, step_of):
    in_specs, out_specs, out_shape = [], [], []
    for item in weights:
        w, cb = item if isinstance(item, tuple) else (item, None)
        rows, cols = w.shape
        slab = rows // nsteps
        assert slab * nsteps == rows and slab % 16 == 0, (w.shape, nsteps)
        in_specs.append(pl.BlockSpec((slab, cols), lambda *ids: (step_of(*ids), 0)))
        if cb is None:
            out_specs.append(pl.BlockSpec((slab, cols), lambda *ids: (step_of(*ids), 0)))
            out_shape.append(jax.ShapeDtypeStruct(w.shape, BF16))
        else:
            out_specs.append(pl.BlockSpec((cols // cb, slab, cb), lambda *ids: (0, step_of(*ids), 0)))
            out_shape.append(jax.ShapeDtypeStruct((cols // cb, rows, cb), BF16))
    return in_specs, out_specs, out_shape


def _run_riders(src_refs, dst_refs):
    for s, d in zip(src_refs, dst_refs):
        if len(d.shape) == 2:
            d[...] = s[...].astype(d.dtype)
        else:
            cb = d.shape[2]
            for c in range(d.shape[0]):
                d[c] = s[:, c * cb:(c + 1) * cb].astype(d.dtype)


def _ssm_prep_kernel(*refs):
    ins, wf_ref, outs, wb_ref, scratch = refs[:8], refs[8], refs[9:13], refs[13], refs[14:]
    _run_riders([wf_ref], [wb_ref])
    for g in range(ins[0].shape[0]):
        _ssm_prep_group(*[r.at[g] for r in ins + outs + scratch])


def _ssm_prep_group(are_ref, aim_ref, ldt_ref, bt_ref, bts_ref, c_ref, cs_ref, d_ref,
                    blk_ref, wst_ref, wout_ref, a2_ref, l_s, z_s, w_s):
    lane = lax.broadcasted_iota(jnp.int32, (1, 2 * STATE), 1)
    sg = jnp.where(lane < STATE, 1.0, -1.0).astype(F32)
    a_re = are_ref[...]
    a_im = aim_ref[...]
    dt = jnp.exp(ldt_ref[...])
    lre = a_re * dt
    lim = a_im * dt

    def cpow(e):
        mag = jnp.exp(e * lre)
        ang = e * lim
        return mag * jnp.cos(ang), mag * jnp.sin(ang)

    def cmul(ar, ai, br, bi):
        return ar * br - ai * bi, ar * bi + ai * br

    col = lax.broadcasted_iota(jnp.int32, (STEPS_PER_BLK, 1), 0).astype(F32)
    t_r, t_i = cpow(col)
    r_r, r_i = cpow(float(STEPS_PER_BLK - 1) - col)
    h_r, h_i = cpow(float(STEPS_PER_BLK) * col[:8])
    ab_r, ab_i = t_r[1:2], t_i[1:2]

    fr, fi = ab_r - 1.0, ab_i
    den = a_re * a_re + a_im * a_im
    cr = (fr * a_re + fi * a_im) / den
    ci = (fi * a_re - fr * a_im) / den
    bb = cr * bt_ref[...] - (ci * sg) * bts_ref[...]
    bbs = cr * bts_ref[...] + (ci * sg) * bt_ref[...]
    cc = c_ref[...]
    ccs = cs_ref[...]

    def outer_rows(pr, pi, m, ms, dst, row0, conj):
        pa = pr * sg if conj else pr
        pb = -pi if conj else -(pi * sg)
        for r in range(pr.shape[0]):
            z = pa[r:r + 1, :] * m + pb[r:r + 1, :] * ms
            dst[row0 + r * GROUP_CH:row0 + (r + 1) * GROUP_CH, :] = z

    e_r, e_i = cmul(ab_r, ab_i, r_r, r_i)
    outer_rows(e_r, e_i, bb, bbs, l_s, 0, False)
    l2 = l_s[...].astype(BF16)

    for d in range(N_BLK - 1):
        e_r, e_i = cmul(h_r[d:d + 1], h_i[d:d + 1], t_r, t_i)
        outer_rows(e_r, e_i, cc, ccs, z_s, 0, True)
        blk_ref[d + 1] = _dot_nt(z_s[...].astype(BF16), l2).astype(BF16)

    outer_rows(r_r, r_i, bb, bbs, z_s, 0, False)
    kr = _dot_nt((cc * sg).astype(BF16), z_s[...].astype(BF16))
    rows = lax.broadcasted_iota(jnp.int32, (GROUP_CH, BLK), 0)
    lanes = lax.broadcasted_iota(jnp.int32, (GROUP_CH, BLK), 1)
    kr = kr + jnp.where(lanes == rows + (BLK - GROUP_CH), d_ref[...], 0.0)
    for t0 in range(STEPS_PER_BLK):
        sh = GROUP_CH * (STEPS_PER_BLK - 1 - t0)
        piece = kr if sh == 0 else jnp.where(lanes < BLK - sh, pltpu.roll(kr, BLK - sh, axis=1), 0.0)
        blk_ref[0, t0 * GROUP_CH:(t0 + 1) * GROUP_CH, :] = piece.astype(BF16)

    for sb in range(N_BLK):
        m = N_BLK - 1 - sb
        e_r, e_i = cmul(h_r[m:m + 1], h_i[m:m + 1], r_r, r_i)
        outer_rows(e_r, e_i, bb, bbs, w_s, sb * BLK, False)
    wst_ref[...] = w_s[...].astype(BF16)

    for tb in range(N_BLK):
        e_r, e_i = cmul(*cmul(ab_r, ab_i, h_r[tb:tb + 1], h_i[tb:tb + 1]), t_r, t_i)
        outer_rows(e_r, e_i, cc, ccs, w_s, tb * BLK, True)
    wout_ref[...] = w_s[...].astype(BF16)

    a2_ref[0:1, :] = h_r[N_BLK:N_BLK + 1]
    a2_ref[1:2, :] = -(h_i[N_BLK:N_BLK + 1] * sg)


def _ssm_prep(a_re, a_im, log_dt, b_re, b_im, c_re, c_im, d_skip, ride):
    g = N_GROUPS
    ng = 8
    r_in, r_out, r_shape = _cast_riders([ride], g // ng, lambda i: i)
    f = lambda a: a.astype(F32)
    dup = lambda a: jnp.concatenate([f(a), f(a)], axis=-1).reshape(g, 1, 2 * STATE)
    bt_re = jnp.swapaxes(f(b_re), 1, 2)
    bt_im = jnp.swapaxes(f(b_im), 1, 2)
    bt = jnp.concatenate([bt_re, bt_im], axis=-1)
    bts = jnp.concatenate([bt_im, bt_re], axis=-1)
    cc = jnp.concatenate([f(c_re), f(c_im)], axis=-1)
    ccs = jnp.concatenate([f(c_im), f(c_re)], axis=-1)
    d_pad = jnp.pad(f(d_skip).reshape(g, 1, GROUP_CH), ((0, 0), (0, 0), (BLK - GROUP_CH, 0)))
    vec = lambda n: pl.BlockSpec((ng, 1, n), lambda i: (i, 0, 0))
    mat = pl.BlockSpec((ng, GROUP_CH, 2 * STATE), lambda i: (i, 0, 0))
    kdim = CHUNK * GROUP_CH
    return pl.pallas_call(
        _ssm_prep_kernel,
        grid=(g // ng,),
        in_specs=[vec(2 * STATE), vec(2 * STATE), vec(1), mat, mat, mat, mat, vec(BLK)] + r_in,
        out_specs=[pl.BlockSpec((ng, N_BLK, BLK, BLK), lambda i: (i, 0, 0, 0)),
                   pl.BlockSpec((ng, kdim, 2 * STATE), lambda i: (i, 0, 0)),
                   pl.BlockSpec((ng, kdim, 2 * STATE), lambda i: (i, 0, 0)),
                   pl.BlockSpec((ng, 2, 2 * STATE), lambda i: (i, 0, 0))] + r_out,
        out_shape=[jax.ShapeDtypeStruct((g, N_BLK, BLK, BLK), BF16),
                   jax.ShapeDtypeStruct((g, kdim, 2 * STATE), BF16),
                   jax.ShapeDtypeStruct((g, kdim, 2 * STATE), BF16),
                   jax.ShapeDtypeStruct((g, 2, 2 * STATE), F32)] + r_shape,
        scratch_shapes=[pltpu.VMEM((ng, BLK, 2 * STATE), F32), pltpu.VMEM((ng, BLK, 2 * STATE), F32),
                        pltpu.VMEM((ng, kdim, 2 * STATE), F32)],
        compiler_params=_params("arbitrary"),
        name="ssm_prep",
    )(dup(a_re), dup(a_im), f(log_dt).reshape(g, 1, 1), bt, bts, cc, ccs, d_pad, ride)


def _rms(x, g):
    return x * lax.rsqrt(jnp.mean(x * x, axis=-1, keepdims=True) + EPS) * g


def _in_proj_kernel(x_ref, n1_ref, w_ref, qg_ref, kg_ref, rc_ref, rm_ref, rp_ref, *rest):
    n_ride = (len(rest) - 4) // 2
    u_ref, q_ref, k_ref, v_ref = rest[n_ride:n_ride + 4]
    _run_riders(rest[:n_ride], rest[n_ride + 4:])
    hr = x_ref.shape[0] // 2
    lo = lax.broadcasted_iota(jnp.int32, (hr, 128), 1) < HEAD_DIM

    for r in range(2):
        rows = slice(r * hr, (r + 1) * hr)
        xn = _rms(x_ref[rows, :], n1_ref[...]).astype(BF16)
        rc = rc_ref[rows, :]
        rm = rm_ref[rows, :]
        rp = rp_ref[rows, :]

        def norm_rope(z, gain, width, scale):
            outs = []
            for cb in range(width // 128):
                zb = z[:, cb * 128:(cb + 1) * 128]
                sq = zb * zb
                s_lo = jnp.sum(jnp.where(lo, sq, 0.0), axis=-1, keepdims=True)
                s_hi = jnp.sum(jnp.where(lo, 0.0, sq), axis=-1, keepdims=True)
                ms = jnp.where(lo, s_lo, s_hi) * (1.0 / HEAD_DIM)
                zn = zb * lax.rsqrt(ms + EPS) * gain
                rot = zn * rc + pltpu.roll(zn, 128 - ROT_DIM // 2, axis=1) * rm \
                    + pltpu.roll(zn, ROT_DIM // 2, axis=1) * rp
                outs.append(rot * scale)
            return outs

        q = _dot(xn, w_ref[:, D_SSM:D_SSM + D_ATTN])
        k = _dot(xn, w_ref[:, D_SSM + D_ATTN:D_SSM + D_ATTN + D_KV])
        for cb, blk in enumerate(norm_rope(q, qg_ref[...], D_ATTN, HEAD_DIM ** -0.5 * LOG2E)):
            q_ref[rows, cb * 128:(cb + 1) * 128] = blk.astype(q_ref.dtype)
        for cb, blk in enumerate(norm_rope(k, kg_ref[...], D_KV, 1.0)):
            k_ref[rows, cb * 128:(cb + 1) * 128] = blk
        u = _dot(xn, w_ref[:, 0:D_SSM])
        for c in range(hr // CHUNK):
            base = (r * (hr // CHUNK) + c) * PITCH
            u_ref[base:base + CHUNK, :] = u[c * CHUNK:(c + 1) * CHUNK]
            u_ref[base + CHUNK:base + PITCH, :] = jnp.zeros((PITCH - CHUNK, D_SSM), F32)
        v_ref[rows, :] = _dot(xn, w_ref[:, D_SSM + D_ATTN + D_KV:])


def _rope_tables(pos):
    half = ROT_DIM // 2
    inv = ROPE_THETA ** (-np.arange(half, dtype=np.float64) * 2.0 / ROT_DIM)
    ang = pos.astype(np.float64)[:, None] * inv[None, :]
    cos, sin = np.cos(ang), np.sin(ang)
    n = pos.shape[0]
    ones = np.ones((n, HEAD_DIM - ROT_DIM))
    zeros = np.zeros((n, HEAD_DIM - ROT_DIM))
    zh = np.zeros((n, half))
    rc = np.concatenate([cos, cos, ones], axis=1)
    rm = np.concatenate([-sin, zh, zeros], axis=1)
    rp = np.concatenate([zh, sin, zeros], axis=1)
    two = lambda t: jnp.asarray(np.concatenate([t, t], axis=1), F32)
    return two(rc), two(rm), two(rp)


def _in_proj(x2d, pos, tm, norm1, w_in_b, qg, kg, ride=()):
    t = x2d.shape[0]
    r_in, r_out, r_shape = _cast_riders(ride, t // tm, lambda i: i)
    rc, rm, rp = _rope_tables(pos)
    nrope = pos.shape[0] // tm
    rope_spec = pl.BlockSpec((tm, 128), lambda i: (i % nrope, 0))
    const = lambda shape: pl.BlockSpec(shape, lambda i: (0, 0))
    tok = lambda w: pl.BlockSpec((tm, w), lambda i: (i, 0))
    return pl.pallas_call(
        _in_proj_kernel,
        grid=(t // tm,),
        in_specs=[tok(D_MODEL), const((1, D_MODEL)), const(w_in_b.shape),
                  const((1, 128)), const((1, 128)), rope_spec, rope_spec, rope_spec] + r_in,
        out_specs=[pl.BlockSpec((tm // CHUNK * PITCH, D_SSM), lambda i: (i, 0)), tok(D_ATTN), tok(D_KV),
                   tok(D_KV)] + r_out,
        out_shape=[jax.ShapeDtypeStruct((t // CHUNK * PITCH, D_SSM), F32), jax.ShapeDtypeStruct((t, D_ATTN), BF16),
                   jax.ShapeDtypeStruct((t, D_KV), F32), jax.ShapeDtypeStruct((t, D_KV), F32)] + r_shape,
        compiler_params=_params("arbitrary"),
        name="in_proj",
    )(x2d, norm1, w_in_b, qg, kg, rc, rm, rp, *[r[0] if isinstance(r, tuple) else r for r in ride])


def _ssm_state_kernel(up_ref, us_ref, wst_ref, xtp_ref, xts_ref, sp_ref, ss_ref, *, mp, bs):
    ng = xtp_ref.shape[0]
    pad = jnp.zeros((128 - bs, 128), F32)
    for s in range(CHUNK):
        vp = up_ref[pl.ds(s, mp, stride=PITCH), :].T.astype(BF16)
        vs = jnp.concatenate([us_ref[pl.ds(s, bs, stride=PITCH), :], pad], axis=0).T.astype(BF16)
        for g in range(ng):
            xtp_ref[g, s * GROUP_CH:(s + 1) * GROUP_CH, :] = vp[g * GROUP_CH:(g + 1) * GROUP_CH, :]
            xts_ref[g, s * GROUP_CH:(s + 1) * GROUP_CH, :] = vs[g * GROUP_CH:(g + 1) * GROUP_CH, :]
    tn = (((0,), (0,)), ((), ()))
    for g in range(ng):
        st = lax.dot_general(wst_ref[g], xtp_ref[g], tn, preferred_element_type=F32)
        sp_ref[g * mp:(g + 1) * mp, :] = st.T
        st = lax.dot_general(wst_ref[g], xts_ref[g], tn, preferred_element_type=F32)
        ss_ref[g * bs:(g + 1) * bs, :] = st.T[:bs]


def _cmul_add(h, ar2, ai2, s):
    return h * ar2 + pltpu.roll(h, STATE, axis=1) * ai2 + s


def _ssm_out_kernel(xtp_ref, xts_ref, blk_ref, wout_ref, sp_ref, ss_ref, arp_ref, aip_ref, ars_ref, ais_ref,
                    h0s_ref, yp_ref, ys_ref, hfp_ref, hfs_ref, hin_s, stp_s, sts_s, *, mp, bs, nchunk):
    ng = xtp_ref.shape[0]
    rows = mp // nchunk * ng
    ar = arp_ref[...]
    ai = aip_ref[...]
    ai_sw = pltpu.roll(ai, STATE, axis=1)
    s_all = [sp_ref[pl.ds(k, rows, stride=nchunk), :] for k in range(nchunk)]
    s_sw = [pltpu.roll(s, STATE, axis=1) for s in s_all]
    h = jnp.zeros((rows, 2 * STATE), F32)
    h_sw = h
    for k in range(nchunk):
        hin_s[pl.ds(k, rows, stride=nchunk), :] = h
        h, h_sw = h * ar + h_sw * ai + s_all[k], h_sw * ar + h * ai_sw + s_sw[k]
    hfp_ref[...] = h
    hfs_ref[...] = _cmul_add(h0s_ref[...], ars_ref[...], ais_ref[...], ss_ref[...])

    pad = jnp.zeros((128 - bs, 2 * STATE), F32)
    hp = [hin_s[g * mp:(g + 1) * mp, :].astype(BF16) for g in range(ng)]
    hs = [jnp.concatenate([h0s_ref[g * bs:(g + 1) * bs, :], pad], axis=0).astype(BF16) for g in range(ng)]
    for tb in range(N_BLK):
        for g in range(ng):
            accp = accs = None
            for sb in range(tb + 1):
                b = blk_ref[g, tb - sb]
                dp = _dot(b, xtp_ref[g, sb * BLK:(sb + 1) * BLK, :])
                ds = _dot(b, xts_ref[g, sb * BLK:(sb + 1) * BLK, :])
                accp = dp if accp is None else accp + dp
                accs = ds if accs is None else accs + ds
            w = wout_ref[g, tb * BLK:(tb + 1) * BLK, :]
            accp = accp + _dot_nt(w, hp[g])
            accs = accs + _dot_nt(w, hs[g])
            for t0 in range(STEPS_PER_BLK):
                stp_s[t0, g * GROUP_CH:(g + 1) * GROUP_CH, :] = accp[t0 * GROUP_CH:(t0 + 1) * GROUP_CH, :]
                sts_s[t0, g * GROUP_CH:(g + 1) * GROUP_CH, :] = accs[t0 * GROUP_CH:(t0 + 1) * GROUP_CH, :]
        for t0 in range(STEPS_PER_BLK):
            t = tb * STEPS_PER_BLK + t0
            yp_ref[pl.ds(t, mp, stride=PITCH), :] = stp_s[t0].T
            ys_ref[pl.ds(t, bs, stride=PITCH), :] = sts_s[t0].T[:bs]
    for r in range(CHUNK, PITCH):
        yp_ref[pl.ds(r, mp, stride=PITCH), :] = jnp.zeros((mp, 128), F32)
        ys_ref[pl.ds(r, bs, stride=PITCH), :] = jnp.zeros((bs, 128), F32)


def _ssm(up, us, blk, wst, wout, a2, h0s, bp, nchunk, bs):
    g = N_GROUPS
    ng = 128 // GROUP_CH
    mp = bp * nchunk
    kdim = CHUNK * GROUP_CH
    tok = lambda a: pl.BlockSpec((a.shape[0], 128), lambda i: (0, i))
    grp = lambda *tail: pl.BlockSpec((ng,) + tail, lambda i: (i,) + (0,) * len(tail))
    rows = lambda r: pl.BlockSpec((ng * r, 2 * STATE), lambda i: (i, 0))
    xtp, xts, sp, ss = pl.pallas_call(
        functools.partial(_ssm_state_kernel, mp=mp, bs=bs),
        grid=(g // ng,),
        in_specs=[tok(up), tok(us), grp(kdim, 2 * STATE)],
        out_specs=[grp(kdim, mp), grp(kdim, 128), rows(mp), rows(bs)],
        out_shape=[jax.ShapeDtypeStruct((g, kdim, mp), BF16), jax.ShapeDtypeStruct((g, kdim, 128), BF16),
                   jax.ShapeDtypeStruct((g * mp, 2 * STATE), F32),
                   jax.ShapeDtypeStruct((g * bs, 2 * STATE), F32)],
        compiler_params=_params("arbitrary"),
        name="ssm_state",
    )(up, us, wst)

    arp = jnp.repeat(a2[:, 0, :], bp, axis=0)
    aip = jnp.repeat(a2[:, 1, :], bp, axis=0)
    ars = jnp.repeat(a2[:, 0, :], bs, axis=0)
    ais = jnp.repeat(a2[:, 1, :], bs, axis=0)
    yp, ys, hfp, hfs = pl.pallas_call(
        functools.partial(_ssm_out_kernel, mp=mp, bs=bs, nchunk=nchunk),
        grid=(g // ng,),
        in_specs=[grp(kdim, mp), grp(kdim, 128), grp(N_BLK, BLK, BLK), grp(kdim, 2 * STATE), rows(mp), rows(bs),
                  rows(bp), rows(bp), rows(bs), rows(bs), rows(bs)],
        out_specs=[tok(up), tok(us), rows(bp), rows(bs)],
        out_shape=[jax.ShapeDtypeStruct(up.shape, F32), jax.ShapeDtypeStruct(us.shape, F32),
                   jax.ShapeDtypeStruct((g * bp, 2 * STATE), F32), jax.ShapeDtypeStruct((g * bs, 2 * STATE), F32)],
        scratch_shapes=[pltpu.VMEM((ng * mp, 2 * STATE), F32), pltpu.VMEM((STEPS_PER_BLK, 128, mp), F32),
                        pltpu.VMEM((STEPS_PER_BLK, 128, 128), F32)],
        compiler_params=_params("arbitrary"),
        name="ssm_out",
    )(xtp, xts, blk, wout, sp, ss, arp, aip, ars, ais, h0s)
    return yp, ys, hfp, hfs


def _attn_kernel(sinks_ref, bias_ref, q_ref, kp_ref, kc_ref, vp_ref, vc_ref, *rest, qp, kw, masked):
    n_ride = (len(rest) - 1) // 2
    o_ref = rest[n_ride]
    _run_riders(rest[:n_ride], rest[n_ride + 1:])
    i = pl.program_id(1)
    n_units = q_ref.shape[0] // qp
    gw = Q_PER_KV * HEAD_DIM
    half = HEAD_DIM
    nl = Q_PER_KV * qp

    def duplicated(prev_ref, cur_ref):
        a = jnp.concatenate([prev_ref[...], cur_ref[...]], axis=0)
        lo = lax.broadcasted_iota(jnp.int32, (a.shape[0], 2 * half), 1) < half
        out = []
        for c in range(N_KV // 2):
            col = a[:, c * 2 * half:(c + 1) * 2 * half]
            rot = pltpu.roll(col, half, axis=1)
            out.append(jnp.where(lo, col, rot).astype(BF16))
            out.append(jnp.where(lo, rot, col).astype(BF16))
        return out

    kdup = duplicated(kp_ref, kc_ref)
    vdup = duplicated(vp_ref, vc_ref)
    pair_lo = lax.broadcasted_iota(jnp.int32, (qp, 2 * half), 1) < half
    lane_g = lax.broadcasted_iota(jnp.int32, (1, nl), 1) // qp

    for u in range(n_units):
        rows = slice(u * qp, (u + 1) * qp)
        win = slice(u * qp, u * qp + kw)
        if masked:
            bias = bias_ref[jnp.where(i == 0, 0, 1)] if u == 0 else bias_ref[1]
        scores = []
        for kv in range(N_KV):
            qm = []
            for g in range(Q_PER_KV):
                pair = q_ref[rows, kv * gw + (g // 2) * 2 * half:kv * gw + (g // 2 + 1) * 2 * half]
                keep = pair_lo if g % 2 == 0 else jnp.logical_not(pair_lo)
                qm.append(jnp.where(keep, pair, jnp.zeros_like(pair)))
            scores.append(_dot_nt(kdup[kv][win], jnp.concatenate(qm, axis=0)))
        probs = []
        for kv in range(N_KV):
            s = scores[kv] + bias if masked else scores[kv]
            sink = jnp.zeros((1, nl), F32)
            for g in range(Q_PER_KV):
                sink = jnp.where(lane_g == g, sinks_ref[kv * Q_PER_KV + g], sink)
            m = jnp.maximum(jnp.max(s, axis=0, keepdims=True), sink)
            p = jnp.exp2(s - m)
            den = jnp.sum(p, axis=0, keepdims=True) + jnp.exp2(sink - m)
            probs.append((p * (1.0 / den)).astype(BF16))
        for kv in range(N_KV):
            o = lax.dot_general(probs[kv], vdup[kv][win], (((0,), (0,)), ((), ())),
                                preferred_element_type=F32)
            for h in range(Q_PER_KV // 2):
                both = jnp.where(pair_lo, o[2 * h * qp:(2 * h + 1) * qp], o[(2 * h + 1) * qp:(2 * h + 2) * qp])
                o_ref[rows, kv * gw + h * 2 * half:kv * gw + (h + 1) * 2 * half] = both.astype(o_ref.dtype)


def _attention(sinks, q, k_prev, k_cur, v_prev, v_cur, qt, qp, masked, ride=()):
    b, l, _ = q.shape
    nt = l // qt
    r_in, r_out, r_shape = _cast_riders(ride, b * nt, lambda bi, i: bi * nt + i)
    per = qt // WINDOW
    kw = WINDOW + qp
    kc = np.arange(kw)[:, None] // CHUNK
    qc = (np.arange(Q_PER_KV * qp)[None, :] % qp) // CHUNK
    band = (kc >= qc) & (kc <= qc + WINDOW // CHUNK)
    bias = np.stack([np.where(band & (kc >= WINDOW // CHUNK), 0.0, NEG), np.where(band, 0.0, NEG)])
    bias = jnp.asarray(bias, F32)
    if masked:
        prev_map = lambda bi, i: (bi, jnp.maximum(i * per - 1, 0), 0)
    else:
        prev_map = lambda bi, i: (bi, 0, 0)
    cur = lambda w: pl.BlockSpec((None, qt, w), lambda bi, i: (bi, i, 0))
    prev = pl.BlockSpec((None, WINDOW, D_KV), prev_map)
    return pl.pallas_call(
        functools.partial(_attn_kernel, qp=qp, kw=kw, masked=masked),
        grid=(b, l // qt),
        in_specs=[pl.BlockSpec(memory_space=pltpu.SMEM), pl.BlockSpec(bias.shape, lambda bi, i: (0, 0, 0)),
                  cur(D_ATTN), prev, cur(D_KV), prev, cur(D_KV)] + r_in,
        out_specs=[cur(D_ATTN)] + r_out,
        out_shape=[jax.ShapeDtypeStruct((b, l, D_ATTN), BF16)] + r_shape,
        compiler_params=_params("arbitrary", "arbitrary"),
        name="attention",
    )(sinks, bias, q, k_prev, k_cur, v_prev, v_cur, *[r[0] if isinstance(r, tuple) else r for r in ride])


def _mix_kernel(x_ref, y_ref, o_ref, n1_ref, wglu_ref, wga_ref, wgb_ref, wbs_ref, wba_ref, wout_ref,
                h_ref, xn_s, ys_s):
    j = pl.program_id(1)

    @pl.when(j == 0)
    def _():
        hr = x_ref.shape[0] // 2
        for r in range(2):
            rows = slice(r * hr, (r + 1) * hr)
            x = x_ref[rows, :]
            xn_s[rows, :] = _rms(x, n1_ref[...]).astype(BF16)
            c0 = r * (hr // CHUNK)
            y = jnp.concatenate([y_ref[c * PITCH:c * PITCH + CHUNK, :] for c in range(c0, c0 + hr // CHUNK)], axis=0)
            ya = jax.nn.gelu(y)
            ys_s[rows, :] = (ya * jax.nn.sigmoid(_dot(ya.astype(BF16), wglu_ref[...]))).astype(BF16)
            h_ref[rows, :] = x

    xn = xn_s[...]
    ga = jax.nn.sigmoid(_dot(xn, wga_ref[...]))
    gb = jax.nn.sigmoid(_dot(xn, wgb_ref[...]))
    mixed = ga * _dot(ys_s[...], wbs_ref[...]) + gb * _dot(o_ref[...], wba_ref[...])
    h_ref[...] += _dot(mixed.astype(BF16), wout_ref[...])


def _mix(x2d, y2d, o2d, norm1, wglu, wgate, wbs, wba, wout, tm):
    t = x2d.shape[0]
    tn = wbs.shape[2]
    nj = D_MODEL // tn
    tok = lambda w: pl.BlockSpec((tm, w), lambda i, j: (i, 0))
    colblk = lambda rows, off: pl.BlockSpec((None, rows, tn), lambda i, j: (off + j, 0, 0))
    return pl.pallas_call(
        _mix_kernel,
        grid=(t // tm, nj),
        in_specs=[tok(D_MODEL), pl.BlockSpec((tm // CHUNK * PITCH, D_SSM), lambda i, j: (i, 0)), tok(D_ATTN),
                  pl.BlockSpec((1, D_MODEL), lambda i, j: (0, 0)),
                  pl.BlockSpec((D_SSM, D_SSM), lambda i, j: (0, 0)),
                  colblk(D_MODEL, 0), colblk(D_MODEL, nj), colblk(D_SSM, 0), colblk(D_ATTN, 0),
                  pl.BlockSpec((tn, D_MODEL), lambda i, j: (j, 0))],
        out_specs=tok(D_MODEL),
        out_shape=jax.ShapeDtypeStruct((t, D_MODEL), F32),
        scratch_shapes=[pltpu.VMEM((tm, D_MODEL), BF16), pltpu.VMEM((tm, D_SSM), BF16)],
        compiler_params=_params("arbitrary", "arbitrary"),
        name="mix",
    )(x2d, y2d, o2d, norm1, wglu, wgate, wgate, wbs, wba, wout)


def _ffn_kernel(h_ref, n2_ref, wg_ref, wu_ref, wd_ref, out_ref, hn_s):
    j = pl.program_id(1)

    @pl.when(j == 0)
    def _():
        h = h_ref[...]
        hn_s[...] = _rms(h, n2_ref[...]).astype(BF16)
        out_ref[...] = h

    hn = hn_s[...]
    act = jax.nn.silu(_dot(hn, wg_ref[...])) * _dot(hn, wu_ref[...])
    out_ref[...] += _dot(act.astype(BF16), wd_ref[...])


def _ffn(h2d, norm2, wg, wu, wd, tm):
    t = h2d.shape[0]
    nf, _, tf = wg.shape
    d_ff = nf * tf
    tok = pl.BlockSpec((tm, D_MODEL), lambda i, j: (i, 0))
    return pl.pallas_call(
        _ffn_kernel,
        grid=(t // tm, d_ff // tf),
        in_specs=[tok, pl.BlockSpec((1, D_MODEL), lambda i, j: (0, 0)),
                  pl.BlockSpec((None, D_MODEL, tf), lambda i, j: (j, 0, 0)),
                  pl.BlockSpec((None, D_MODEL, tf), lambda i, j: (j, 0, 0)),
                  pl.BlockSpec((tf, D_MODEL), lambda i, j: (j, 0))],
        out_specs=tok,
        out_shape=jax.ShapeDtypeStruct((t, D_MODEL), F32),
        scratch_shapes=[pltpu.VMEM((tm, D_MODEL), BF16)],
        compiler_params=_params("arbitrary", "arbitrary"),
        name="ffn",
    )(h2d, norm2, wg, wu, wd)


def kernel(x_prompt, x_sample, cache_k, cache_v, state_ssm_re, state_ssm_im, norm1, w_in, q_norm, k_norm,
           sinks, ssm_a_re, ssm_a_im, ssm_log_dt, ssm_b_re, ssm_b_im, ssm_c_re, ssm_c_im, ssm_d, w_glu,
           w_br_ssm, w_br_attn, w_gate, w_out, norm2, w_ffn_gate, w_ffn_up, w_ffn_down):
    bp, lp, _ = x_prompt.shape
    bs, ls, _ = x_sample.shape
    ncp = lp // CHUNK
    assert ls == CHUNK and cache_k.shape[2] == WINDOW and norm1.shape[0] == 1
    l = 0

    f32 = lambda a: a[l].astype(F32)
    n1 = norm1[l].astype(F32).reshape(1, D_MODEL)
    n2 = norm2[l].astype(F32).reshape(1, D_MODEL)
    qg = jnp.tile(q_norm[l].astype(F32), 2).reshape(1, 128)
    kg = jnp.tile(k_norm[l].astype(F32), 2).reshape(1, 128)
    sink = sinks[l].astype(F32) * LOG2E

    blk, wst, wout_t, a2, w_in_b = _ssm_prep(ssm_a_re[l], ssm_a_im[l], ssm_log_dt[l], ssm_b_re[l], ssm_b_im[l],
                                             ssm_c_re[l], ssm_c_im[l], ssm_d[l], f32(w_in))

    xp = x_prompt.astype(F32).reshape(bp * lp, D_MODEL)
    xs = x_sample.astype(F32).reshape(bs * ls, D_MODEL)
    tm = TM_IN
    up, qp, kp, vp, wfg_b, wfu_b, wfd_b = _in_proj(
        xp, np.arange(lp), tm, n1, w_in_b, qg, kg,
        ride=((f32(w_ffn_gate), TF_FFN), (f32(w_ffn_up), TF_FFN), f32(w_ffn_down)))
    us, qs, ks, vs = _in_proj(xs, np.tile(PAST_LEN + np.arange(ls), tm // ls), tm, n1, w_in_b, qg, kg)

    h0s = jnp.concatenate([state_ssm_re[l], state_ssm_im[l]], axis=-1).astype(F32)
    h0s = jnp.swapaxes(h0s, 0, 1).reshape(N_GROUPS * bs, 2 * STATE)
    y_p, y_s, hfp, hfs = _ssm(up, us, blk, wst, wout_t, a2, h0s, bp, ncp, bs)

    def states(hf, b):
        hf = jnp.swapaxes(hf.reshape(N_GROUPS, b, 2 * STATE), 0, 1)
        return hf[None, :, :, :STATE], hf[None, :, :, STATE:]

    kp3, vp3 = kp.reshape(bp, lp, D_KV), vp.reshape(bp, lp, D_KV)
    o_p, wgate_b, wout_b, wbs_b, wba_b, wglu_b = _attention(
        sink, qp.reshape(bp, lp, D_ATTN), kp3, kp3, vp3, vp3, 256, 2 * CHUNK, True,
        ride=((f32(w_gate), TN_MIX), f32(w_out), (f32(w_br_ssm), TN_MIX), (f32(w_br_attn), TN_MIX), f32(w_glu)))
    ck = cache_k[l].astype(F32).reshape(bs, WINDOW, D_KV)
    cv = cache_v[l].astype(F32).reshape(bs, WINDOW, D_KV)
    ks3, vs3 = ks.reshape(bs, ls, D_KV), vs.reshape(bs, ls, D_KV)
    o_s, = _attention(sink, qs.reshape(bs, ls, D_ATTN), ck, ks3, cv, vs3, ls, ls, False)

    outs = []
    for x2d, y2d, o3d in ((xp, y_p, o_p), (xs, y_s, o_s)):
        h = _mix(x2d, y2d, o3d.reshape(-1, D_ATTN), n1, wglu_b, wgate_b, wbs_b, wba_b, wout_b, TM_MIX)
        outs.append(_ffn(h, n2, wfg_b, wfu_b, wfd_b, TM_FFN))

    win = lambda a, b, n: a.reshape(b, n, N_KV, HEAD_DIM)[None]
    k_win_p = win(kp3[:, lp - WINDOW:], bp, WINDOW)
    v_win_p = win(vp3[:, lp - WINDOW:], bp, WINDOW)
    k_win_s = win(jnp.concatenate([ck, ks3], axis=1)[:, ls:], bs, WINDOW)
    v_win_s = win(jnp.concatenate([cv, vs3], axis=1)[:, ls:], bs, WINDOW)
    re_p, im_p = states(hfp, bp)
    re_s, im_s = states(hfs, bs)
    dt = x_prompt.dtype
    return (outs[0].reshape(bp, lp, D_MODEL).astype(dt), outs[1].reshape(bs, ls, D_MODEL).astype(dt),
            k_win_p.astype(dt), v_win_p.astype(dt), re_p.astype(dt), im_p.astype(dt),
            k_win_s.astype(dt), v_win_s.astype(dt), re_s.astype(dt), im_s.astype(dt))
```

```python
import functools

import jax
import jax.numpy as jnp
import numpy as np
from jax import lax
from jax.experimental import pallas as pl
from jax.experimental.pallas import tpu as pltpu

D_MODEL = 2048
CHUNK = 64
D_SSM = 1024
GROUP_CH = 16
N_GROUPS = 64
STATE = 64
HEAD_DIM = 64
N_HEADS = 16
N_KV = 4
Q_PER_KV = 4
D_ATTN = 1024
D_KV = 256
WINDOW = 128
ROT_DIM = 16
ROPE_THETA = 500000.0
PAST_LEN = 1024
EPS = 1e-6
NEG = -1e30
LOG2E = 1.4426950408889634

BLK = 256
PITCH = CHUNK + 8
STEPS_PER_BLK = BLK // GROUP_CH
N_BLK = CHUNK // STEPS_PER_BLK
V7X_VMEM_LIMIT = 56 * 1024 * 1024
TM_IN, TM_MIX, TN_MIX, TM_FFN, TF_FFN = 512, 512, 512, 1024, 512

F32 = jnp.float32
BF16 = jnp.bfloat16


def _dot(a, b):
    return jnp.dot(a, b, preferred_element_type=F32)


def _dot_nt(a, b):
    return lax.dot_general(a, b, (((1,), (1,)), ((), ())), preferred_element_type=F32)


def _params(*sem):
    return pltpu.CompilerParams(dimension_semantics=sem, vmem_limit_bytes=V7X_VMEM_LIMIT)


def _cast_riders(weights, nsteps, step_of):
    in_specs, out_specs, out_shape = [], [], []
    for item in weights:
        w, cb = item if isinstance(item, tuple) else (item, None)
        rows, cols = w.shape
        slab = rows // nsteps
        assert slab * nsteps == rows and slab % 16 == 0, (w.shape, nsteps)
        in_specs.append(pl.BlockSpec((slab, cols), lambda *ids: (step_of(*ids), 0)))
        if cb is None:
            out_specs.append(pl.BlockSpec((slab, cols), lambda *ids: (step_of(*ids), 0)))
            out_shape.append(jax.ShapeDtypeStruct(w.shape, BF16))
        else:
            out_specs.append(pl.BlockSpec((cols // cb, slab, cb), lambda *ids: (0, step_of(*ids), 0)))
            out_shape.append(jax.ShapeDtypeStruct((cols // cb, rows, cb), BF16))
    return in_specs, out_specs, out_shape


def _run_riders(src_refs, dst_refs):
    for s, d in zip(src_refs, dst_refs):
        if len(d.shape) == 2:
            d[...] = s[...].astype(d.dtype)
        else:
            cb = d.shape[2]
            for c in range(d.shape[0]):
                d[c] = s[:, c * cb:(c + 1) * cb].astype(d.dtype)


def _ssm_prep_kernel(*refs):
    ins, wf_ref, outs, wb_ref, scratch = refs[:8], refs[8], refs[9:13], refs[13], refs[14:]
    _run_riders([wf_ref], [wb_ref])
    for g in range(ins[0].shape[0]):
        _ssm_prep_group(*[r.at[g] for r in ins + outs + scratch])


def _ssm_prep_group(are_ref, aim_ref, ldt_ref, bt_ref, bts_ref, c_ref, cs_ref, d_ref,
                    blk_ref, wst_ref, wout_ref, a2_ref, l_s, z_s, w_s):
    lane = lax.broadcasted_iota(jnp.int32, (1, 2 * STATE), 1)
    sg = jnp.where(lane < STATE, 1.0, -1.0).astype(F32)
    a_re = are_ref[...]
    a_im = aim_ref[...]
    dt = jnp.exp(ldt_ref[...])
    lre = a_re * dt
    lim = a_im * dt

    def cpow(e):
        mag = jnp.exp(e * lre)
        ang = e * lim
        return mag * jnp.cos(ang), mag * jnp.sin(ang)

    def cmul(ar, ai, br, bi):
        return ar * br - ai * bi, ar * bi + ai * br

    col = lax.broadcasted_iota(jnp.int32, (STEPS_PER_BLK, 1), 0).astype(F32)
    t_r, t_i = cpow(col)
    r_r, r_i = cpow(float(STEPS_PER_BLK - 1) - col)
    h_r, h_i = cpow(float(STEPS_PER_BLK) * col[:8])
    ab_r, ab_i = t_r[1:2], t_i[1:2]

    fr, fi = ab_r - 1.0, ab_i
    den = a_re * a_re + a_im * a_im
    cr = (fr * a_re + fi * a_im) / den
    ci = (fi * a_re - fr * a_im) / den
    bb = cr * bt_ref[...] - (ci * sg) * bts_ref[...]
    bbs = cr * bts_ref[...] + (ci * sg) * bt_ref[...]
    cc = c_ref[...]
    ccs = cs_ref[...]

    def outer_rows(pr, pi, m, ms, dst, row0, conj):
        pa = pr * sg if conj else pr
        pb = -pi if conj else -(pi * sg)
        for r in range(pr.shape[0]):
            z = pa[r:r + 1, :] * m + pb[r:r + 1, :] * ms
            dst[row0 + r * GROUP_CH:row0 + (r + 1) * GROUP_CH, :] = z

    e_r, e_i = cmul(ab_r, ab_i, r_r, r_i)
    outer_rows(e_r, e_i, bb, bbs, l_s, 0, False)
    l2 = l_s[...].astype(BF16)

    for d in range(N_BLK - 1):
        e_r, e_i = cmul(h_r[d:d + 1], h_i[d:d + 1], t_r, t_i)
        outer_rows(e_r, e_i, cc, ccs, z_s, 0, True)
        blk_ref[d + 1] = _dot_nt(z_s[...].astype(BF16), l2).astype(BF16)

    outer_rows(r_r, r_i, bb, bbs, z_s, 0, False)
    kr = _dot_nt((cc * sg).astype(BF16), z_s[...].astype(BF16))
    rows = lax.broadcasted_iota(jnp.int32, (GROUP_CH, BLK), 0)
    lanes = lax.broadcasted_iota(jnp.int32, (GROUP_CH, BLK), 1)
    kr = kr + jnp.where(lanes == rows + (BLK - GROUP_CH), d_ref[...], 0.0)
    for t0 in range(STEPS_PER_BLK):
        sh = GROUP_CH * (STEPS_PER_BLK - 1 - t0)
        piece = kr if sh == 0 else jnp.where(lanes < BLK - sh, pltpu.roll(kr, BLK - sh, axis=1), 0.0)
        blk_ref[0, t0 * GROUP_CH:(t0 + 1) * GROUP_CH, :] = piece.astype(BF16)

    for sb in range(N_BLK):
        m = N_BLK - 1 - sb
        e_r, e_i = cmul(h_r[m:m + 1], h_i[m:m + 1], r_r, r_i)
        outer_rows(e_r, e_i, bb, bbs, w_s, sb * BLK, False)
    wst_ref[...] = w_s[...].astype(BF16)

    for tb in range(N_BLK):
        e_r, e_i = cmul(*cmul(ab_r, ab_i, h_r[tb:tb + 1], h_i[tb:tb + 1]), t_r, t_i)
        outer_rows(e_r, e_i, cc, ccs, w_s, tb * BLK, True)
    wout_ref[...] = w_s[...].astype(BF16)

    a2_ref[0:1, :] = h_r[N_BLK:N_BLK + 1]
    a2_ref[1:2, :] = -(h_i[N_BLK:N_BLK + 1] * sg)


def _ssm_prep(a_re, a_im, log_dt, b_re, b_im, c_re, c_im, d_skip, ride):
    g = N_GROUPS
    ng = 8
    r_in, r_out, r_shape = _cast_riders([ride], g // ng, lambda i: i)
    f = lambda a: a.astype(F32)
    dup = lambda a: jnp.concatenate([f(a), f(a)], axis=-1).reshape(g, 1, 2 * STATE)
    bt_re = jnp.swapaxes(f(b_re), 1, 2)
    bt_im = jnp.swapaxes(f(b_im), 1, 2)
    bt = jnp.concatenate([bt_re, bt_im], axis=-1)
    bts = jnp.concatenate([bt_im, bt_re], axis=-1)
    cc = jnp.concatenate([f(c_re), f(c_im)], axis=-1)
    ccs = jnp.concatenate([f(c_im), f(c_re)], axis=-1)
    d_pad = jnp.pad(f(d_skip).reshape(g, 1, GROUP_CH), ((0, 0), (0, 0), (BLK - GROUP_CH, 0)))
    vec = lambda n: pl.BlockSpec((ng, 1, n), lambda i: (i, 0, 0))
    mat = pl.BlockSpec((ng, GROUP_CH, 2 * STATE), lambda i: (i, 0, 0))
    kdim = CHUNK * GROUP_CH
    return pl.pallas_call(
        _ssm_prep_kernel,
        grid=(g // ng,),
        in_specs=[vec(2 * STATE), vec(2 * STATE), vec(1), mat, mat, mat, mat, vec(BLK)] + r_in,
        out_specs=[pl.BlockSpec((ng, N_BLK, BLK, BLK), lambda i: (i, 0, 0, 0)),
                   pl.BlockSpec((ng, kdim, 2 * STATE), lambda i: (i, 0, 0)),
                   pl.BlockSpec((ng, kdim, 2 * STATE), lambda i: (i, 0, 0)),
                   pl.BlockSpec((ng, 2, 2 * STATE), lambda i: (i, 0, 0))] + r_out,
        out_shape=[jax.ShapeDtypeStruct((g, N_BLK, BLK, BLK), BF16),
                   jax.ShapeDtypeStruct((g, kdim, 2 * STATE), BF16),
                   jax.ShapeDtypeStruct((g, kdim, 2 * STATE), BF16),
                   jax.ShapeDtypeStruct((g, 2, 2 * STATE), F32)] + r_shape,
        scratch_shapes=[pltpu.VMEM((ng, BLK, 2 * STATE), F32), pltpu.VMEM((ng, BLK, 2 * STATE), F32),
                        pltpu.VMEM((ng, kdim, 2 * STATE), F32)],
        compiler_params=_params("arbitrary"),
        name="ssm_prep",
    )(dup(a_re), dup(a_im), f(log_dt).reshape(g, 1, 1), bt, bts, cc, ccs, d_pad, ride)


def _rms(x, g):
    return x * lax.rsqrt(jnp.mean(x * x, axis=-1, keepdims=True) + EPS) * g


def _in_proj_kernel(x_ref, n1_ref, w_ref, qg_ref, kg_ref, rc_ref, rm_ref, rp_ref, *rest):
    n_ride = (len(rest) - 4) // 2
    u_ref, q_ref, k_ref, v_ref = rest[n_ride:n_ride + 4]
    _run_riders(rest[:n_ride], rest[n_ride + 4:])
    hr = x_ref.shape[0] // 2
    lo = lax.broadcasted_iota(jnp.int32, (hr, 128), 1) < HEAD_DIM

    for r in range(2):
        rows = slice(r * hr, (r + 1) * hr)
        xn = _rms(x_ref[rows, :], n1_ref[...]).astype(BF16)
        rc = rc_ref[rows, :]
        rm = rm_ref[rows, :]
        rp = rp_ref[rows, :]

        def norm_rope(z, gain, width, scale):
            outs = []
            for cb in range(width // 128):
                zb = z[:, cb * 128:(cb + 1) * 128]
                sq = zb * zb
                s_lo = jnp.sum(jnp.where(lo, sq, 0.0), axis=-1, keepdims=True)
                s_hi = jnp.sum(jnp.where(lo, 0.0, sq), axis=-1, keepdims=True)
                ms = jnp.where(lo, s_lo, s_hi) * (1.0 / HEAD_DIM)
                zn = zb * lax.rsqrt(ms + EPS) * gain
                rot = zn * rc + pltpu.roll(zn, 128 - ROT_DIM // 2, axis=1) * rm \
                    + pltpu.roll(zn, ROT_DIM // 2, axis=1) * rp
                outs.append(rot * scale)
            return outs

        q = _dot(xn, w_ref[:, D_SSM:D_SSM + D_ATTN])
        k = _dot(xn, w_ref[:, D_SSM + D_ATTN:D_SSM + D_ATTN + D_KV])
        for cb, blk in enumerate(norm_rope(q, qg_ref[...], D_ATTN, HEAD_DIM ** -0.5 * LOG2E)):
            q_ref[rows, cb * 128:(cb + 1) * 128] = blk.astype(q_ref.dtype)
        for cb, blk in enumerate(norm_rope(k, kg_ref[...], D_KV, 1.0)):
            k_ref[rows, cb * 128:(cb + 1) * 128] = blk
        u = _dot(xn, w_ref[:, 0:D_SSM])
        for c in range(hr // CHUNK):
            base = (r * (hr // CHUNK) + c) * PITCH
            u_ref[base:base + CHUNK, :] = u[c * CHUNK:(c + 1) * CHUNK]
            u_ref[base + CHUNK:base + PITCH, :] = jnp.zeros((PITCH - CHUNK, D_SSM), F32)
        v_ref[rows, :] = _dot(xn, w_ref[:, D_SSM + D_ATTN + D_KV:])


def _rope_tables(pos):
    half = ROT_DIM // 2
    inv = ROPE_THETA ** (-np.arange(half, dtype=np.float64) * 2.0 / ROT_DIM)
    ang = pos.astype(np.float64)[:, None] * inv[None, :]
    cos, sin = np.cos(ang), np.sin(ang)
    n = pos.shape[0]
    ones = np.ones((n, HEAD_DIM - ROT_DIM))
    zeros = np.zeros((n, HEAD_DIM - ROT_DIM))
    zh = np.zeros((n, half))
    rc = np.concatenate([cos, cos, ones], axis=1)
    rm = np.concatenate([-sin, zh, zeros], axis=1)
    rp = np.concatenate([zh, sin, zeros], axis=1)
    two = lambda t: jnp.asarray(np.concatenate([t, t], axis=1), F32)
    return two(rc), two(rm), two(rp)


def _in_proj(x2d, pos, tm, norm1, w_in_b, qg, kg, ride=()):
    t = x2d.shape[0]
    r_in, r_out, r_shape = _cast_riders(ride, t // tm, lambda i: i)
    rc, rm, rp = _rope_tables(pos)
    nrope = pos.shape[0] // tm
    rope_spec = pl.BlockSpec((tm, 128), lambda i: (i % nrope, 0))
    const = lambda shape: pl.BlockSpec(shape, lambda i: (0, 0))
    tok = lambda w: pl.BlockSpec((tm, w), lambda i: (i, 0))
    return pl.pallas_call(
        _in_proj_kernel,
        grid=(t // tm,),
        in_specs=[tok(D_MODEL), const((1, D_MODEL)), const(w_in_b.shape),
                  const((1, 128)), const((1, 128)), rope_spec, rope_spec, rope_spec] + r_in,
        out_specs=[pl.BlockSpec((tm // CHUNK * PITCH, D_SSM), lambda i: (i, 0)), tok(D_ATTN), tok(D_KV),
                   tok(D_KV)] + r_out,
        out_shape=[jax.ShapeDtypeStruct((t // CHUNK * PITCH, D_SSM), F32), jax.ShapeDtypeStruct((t, D_ATTN), BF16),
                   jax.ShapeDtypeStruct((t, D_KV), F32), jax.ShapeDtypeStruct((t, D_KV), F32)] + r_shape,
        compiler_params=_params("arbitrary"),
        name="in_proj",
    )(x2d, norm1, w_in_b, qg, kg, rc, rm, rp, *[r[0] if isinstance(r, tuple) else r for r in ride])


def _ssm_state_kernel(up_ref, us_ref, wst_ref, xtp_ref, xts_ref, sp_ref, ss_ref, *, mp, bs):
    ng = xtp_ref.shape[0]
    pad = jnp.zeros((128 - bs, 128), F32)
    for s in range(CHUNK):
        vp = up_ref[pl.ds(s, mp, stride=PITCH), :].T.astype(BF16)
        vs = jnp.concatenate([us_ref[pl.ds(s, bs, stride=PITCH), :], pad], axis=0).T.astype(BF16)
        for g in range(ng):
            xtp_ref[g, s * GROUP_CH:(s + 1) * GROUP_CH, :] = vp[g * GROUP_CH:(g + 1) * GROUP_CH, :]
            xts_ref[g, s * GROUP_CH:(s + 1) * GROUP_CH, :] = vs[g * GROUP_CH:(g + 1) * GROUP_CH, :]
    tn = (((0,), (0,)), ((), ()))
    for g in range(ng):
        st = lax.dot_general(wst_ref[g], xtp_ref[g], tn, preferred_element_type=F32)
        sp_ref[g * mp:(g + 1) * mp, :] = st.T
        st = lax.dot_general(wst_ref[g], xts_ref[g], tn, preferred_element_type=F32)
        ss_ref[g * bs:(g + 1) * bs, :] = st.T[:bs]


def _cmul_add(h, ar2, ai2, s):
    return h * ar2 + pltpu.roll(h, STATE, axis=1) * ai2 + s


def _ssm_out_kernel(xtp_ref, xts_ref, blk_ref, wout_ref, sp_ref, ss_ref, arp_ref, aip_ref, ars_ref, ais_ref,
                    h0s_ref, yp_ref, ys_ref, hfp_ref, hfs_ref, hin_s, stp_s, sts_s, *, mp, bs, nchunk):
    ng = xtp_ref.shape[0]
    rows = mp // nchunk * ng
    ar = arp_ref[...]
    ai = aip_ref[...]
    ai_sw = pltpu.roll(ai, STATE, axis=1)
    s_all = [sp_ref[pl.ds(k, rows, stride=nchunk), :] for k in range(nchunk)]
    s_sw = [pltpu.roll(s, STATE, axis=1) for s in s_all]
    h = jnp.zeros((rows, 2 * STATE), F32)
    h_sw = h
    for k in range(nchunk):
        hin_s[pl.ds(k, rows, stride=nchunk), :] = h
        h, h_sw = h * ar + h_sw * ai + s_all[k], h_sw * ar + h * ai_sw + s_sw[k]
    hfp_ref[...] = h
    hfs_ref[...] = _cmul_add(h0s_ref[...], ars_ref[...], ais_ref[...], ss_ref[...])

    pad = jnp.zeros((128 - bs, 2 * STATE), F32)
    hp = [hin_s[g * mp:(g + 1) * mp, :].astype(BF16) for g in range(ng)]
    hs = [jnp.concatenate([h0s_ref[g * bs:(g + 1) * bs, :], pad], axis=0).astype(BF16) for g in range(ng)]
    for tb in range(N_BLK):
        for g in range(ng):
            accp = accs = None
            for sb in range(tb + 1):
                b = blk_ref[g, tb - sb]
                dp = _dot(b, xtp_ref[g, sb * BLK:(sb + 1) * BLK, :])
                ds = _dot(b, xts_ref[g, sb * BLK:(sb + 1) * BLK, :])
                accp = dp if accp is None else accp + dp
                accs = ds if accs is None else accs + ds
            w = wout_ref[g, tb * BLK:(tb + 1) * BLK, :]
            accp = accp + _dot_nt(w, hp[g])
            accs = accs + _dot_nt(w, hs[g])
            for t0 in range(STEPS_PER_BLK):
                stp_s[t0, g * GROUP_CH:(g + 1) * GROUP_CH, :] = accp[t0 * GROUP_CH:(t0 + 1) * GROUP_CH, :]
                sts_s[t0, g * GROUP_CH:(g + 1) * GROUP_CH, :] = accs[t0 * GROUP_CH:(t0 + 1) * GROUP_CH, :]
        for t0 in range(STEPS_PER_BLK):
            t = tb * STEPS_PER_BLK + t0
            yp_ref[pl.ds(t, mp, stride=PITCH), :] = stp_s[t0].T
            ys_ref[pl.ds(t, bs, stride=PITCH), :] = sts_s[t0].T[:bs]
    for r in range(CHUNK, PITCH):
        yp_ref[pl.ds(r, mp, stride=PITCH), :] = jnp.zeros((mp, 128), F32)
        ys_ref[pl.ds(r, bs, stride=PITCH), :] = jnp.zeros((bs, 128), F32)


def _ssm(up, us, blk, wst, wout, a2, h0s, bp, nchunk, bs):
    g = N_GROUPS
    ng = 128 // GROUP_CH
    mp = bp * nchunk
    kdim = CHUNK * GROUP_CH
    tok = lambda a: pl.BlockSpec((a.shape[0], 128), lambda i: (0, i))
    grp = lambda *tail: pl.BlockSpec((ng,) + tail, lambda i: (i,) + (0,) * len(tail))
    rows = lambda r: pl.BlockSpec((ng * r, 2 * STATE), lambda i: (i, 0))
    xtp, xts, sp, ss = pl.pallas_call(
        functools.partial(_ssm_state_kernel, mp=mp, bs=bs),
        grid=(g // ng,),
        in_specs=[tok(up), tok(us), grp(kdim, 2 * STATE)],
        out_specs=[grp(kdim, mp), grp(kdim, 128), rows(mp), rows(bs)],
        out_shape=[jax.ShapeDtypeStruct((g, kdim, mp), BF16), jax.ShapeDtypeStruct((g, kdim, 128), BF16),
                   jax.ShapeDtypeStruct((g * mp, 2 * STATE), F32),
                   jax.ShapeDtypeStruct((g * bs, 2 * STATE), F32)],
        compiler_params=_params("arbitrary"),
        name="ssm_state",
    )(up, us, wst)

    arp = jnp.repeat(a2[:, 0, :], bp, axis=0)
    aip = jnp.repeat(a2[:, 1, :], bp, axis=0)
    ars = jnp.repeat(a2[:, 0, :], bs, axis=0)
    ais = jnp.repeat(a2[:, 1, :], bs, axis=0)
    yp, ys, hfp, hfs = pl.pallas_call(
        functools.partial(_ssm_out_kernel, mp=mp, bs=bs, nchunk=nchunk),
        grid=(g // ng,),
        in_specs=[grp(kdim, mp), grp(kdim, 128), grp(N_BLK, BLK, BLK), grp(kdim, 2 * STATE), rows(mp), rows(bs),
                  rows(bp), rows(bp), rows(bs), rows(bs), rows(bs)],
        out_specs=[tok(up), tok(us), rows(bp), rows(bs)],
        out_shape=[jax.ShapeDtypeStruct(up.shape, F32), jax.ShapeDtypeStruct(us.shape, F32),
                   jax.ShapeDtypeStruct((g * bp, 2 * STATE), F32), jax.ShapeDtypeStruct((g * bs, 2 * STATE), F32)],
        scratch_shapes=[pltpu.VMEM((ng * mp, 2 * STATE), F32), pltpu.VMEM((STEPS_PER_BLK, 128, mp), F32),
                        pltpu.VMEM((STEPS_PER_BLK, 128, 128), F32)],
        compiler_params=_params("arbitrary"),
        name="ssm_out",
    )(xtp, xts, blk, wout, sp, ss, arp, aip, ars, ais, h0s)
    return yp, ys, hfp, hfs


def _attn_kernel(sinks_ref, bias_ref, q_ref, kp_ref, kc_ref, vp_ref, vc_ref, *rest, qp, kw, masked):
    n_ride = (len(rest) - 1) // 2
    o_ref = rest[n_ride]
    _run_riders(rest[:n_ride], rest[n_ride + 1:])
    for e in range(q_ref.shape[0]):
        _attn_tile(sinks_ref, bias_ref, q_ref.at[e], kp_ref.at[e], kc_ref.at[e], vp_ref.at[e], vc_ref.at[e],
                   o_ref.at[e], qp=qp, kw=kw, masked=masked)


def _attn_tile(sinks_ref, bias_ref, q_ref, kp_ref, kc_ref, vp_ref, vc_ref, o_ref, *, qp, kw, masked):
    i = pl.program_id(1)
    n_units = q_ref.shape[0] // qp
    gw = Q_PER_KV * HEAD_DIM
    half = HEAD_DIM
    nl = Q_PER_KV * qp

    def duplicated(prev_ref, cur_ref):
        a = jnp.concatenate([prev_ref[...], cur_ref[...]], axis=0)
        lo = lax.broadcasted_iota(jnp.int32, (a.shape[0], 2 * half), 1) < half
        out = []
        for c in range(N_KV // 2):
            col = a[:, c * 2 * half:(c + 1) * 2 * half]
            rot = pltpu.roll(col, half, axis=1)
            out.append(jnp.where(lo, col, rot).astype(BF16))
            out.append(jnp.where(lo, rot, col).astype(BF16))
        return out

    kdup = duplicated(kp_ref, kc_ref)
    vdup = duplicated(vp_ref, vc_ref)
    pair_lo = lax.broadcasted_iota(jnp.int32, (qp, 2 * half), 1) < half
    lane_g = lax.broadcasted_iota(jnp.int32, (1, nl), 1) // qp

    for u in range(n_units):
        rows = slice(u * qp, (u + 1) * qp)
        win = slice(u * qp, u * qp + kw)
        if masked:
            bias = bias_ref[jnp.where(i == 0, 0, 1)] if u == 0 else bias_ref[1]
        scores = []
        for kv in range(N_KV):
            qm = []
            for g in range(Q_PER_KV):
                pair = q_ref[rows, kv * gw + (g // 2) * 2 * half:kv * gw + (g // 2 + 1) * 2 * half]
                keep = pair_lo if g % 2 == 0 else jnp.logical_not(pair_lo)
                qm.append(jnp.where(keep, pair, jnp.zeros_like(pair)))
            scores.append(_dot_nt(kdup[kv][win], jnp.concatenate(qm, axis=0)))
        probs = []
        for kv in range(N_KV):
            s = scores[kv] + bias if masked else scores[kv]
            sink = jnp.zeros((1, nl), F32)
            for g in range(Q_PER_KV):
                sink = jnp.where(lane_g == g, sinks_ref[kv * Q_PER_KV + g], sink)
            m = jnp.maximum(jnp.max(s, axis=0, keepdims=True), sink)
            p = jnp.exp2(s - m)
            den = jnp.sum(p, axis=0, keepdims=True) + jnp.exp2(sink - m)
            probs.append((p * (1.0 / den)).astype(BF16))
        for kv in range(N_KV):
            o = lax.dot_general(probs[kv], vdup[kv][win], (((0,), (0,)), ((), ())),
                                preferred_element_type=F32)
            for h in range(Q_PER_KV // 2):
                both = jnp.where(pair_lo, o[2 * h * qp:(2 * h + 1) * qp], o[(2 * h + 1) * qp:(2 * h + 2) * qp])
                o_ref[rows, kv * gw + h * 2 * half:kv * gw + (h + 1) * 2 * half] = both.astype(o_ref.dtype)


def _attention(sinks, q, k_prev, k_cur, v_prev, v_cur, qt, qp, masked, bb=1, ride=()):
    b, l, _ = q.shape
    nt = l // qt
    r_in, r_out, r_shape = _cast_riders(ride, b // bb * nt, lambda bi, i: bi * nt + i)
    per = qt // WINDOW
    kw = WINDOW + qp
    kc = np.arange(kw)[:, None] // CHUNK
    qc = (np.arange(Q_PER_KV * qp)[None, :] % qp) // CHUNK
    band = (kc >= qc) & (kc <= qc + WINDOW // CHUNK)
    bias = np.stack([np.where(band & (kc >= WINDOW // CHUNK), 0.0, NEG), np.where(band, 0.0, NEG)])
    bias = jnp.asarray(bias, F32)
    if masked:
        prev_map = lambda bi, i: (bi, jnp.maximum(i * per - 1, 0), 0)
    else:
        prev_map = lambda bi, i: (bi, 0, 0)
    cur = lambda w: pl.BlockSpec((bb, qt, w), lambda bi, i: (bi, i, 0))
    prev = pl.BlockSpec((bb, WINDOW, D_KV), prev_map)
    return pl.pallas_call(
        functools.partial(_attn_kernel, qp=qp, kw=kw, masked=masked),
        grid=(b // bb, nt),
        in_specs=[pl.BlockSpec(memory_space=pltpu.SMEM), pl.BlockSpec(bias.shape, lambda bi, i: (0, 0, 0)),
                  cur(D_ATTN), prev, cur(D_KV), prev, cur(D_KV)] + r_in,
        out_specs=[cur(D_ATTN)] + r_out,
        out_shape=[jax.ShapeDtypeStruct((b, l, D_ATTN), BF16)] + r_shape,
        compiler_params=_params("arbitrary", "arbitrary"),
        name="attention",
    )(sinks, bias, q, k_prev, k_cur, v_prev, v_cur, *[r[0] if isinstance(r, tuple) else r for r in ride])


def _mix_kernel(x_ref, y_ref, o_ref, n1_ref, wglu_ref, wga_ref, wgb_ref, wbs_ref, wba_ref, wout_ref,
                h_ref, xn_s, ys_s):
    j = pl.program_id(1)

    @pl.when(j == 0)
    def _():
        hr = x_ref.shape[0] // 2
        for r in range(2):
            rows = slice(r * hr, (r + 1) * hr)
            x = x_ref[rows, :]
            xn_s[rows, :] = _rms(x, n1_ref[...]).astype(BF16)
            c0 = r * (hr // CHUNK)
            y = jnp.concatenate([y_ref[c * PITCH:c * PITCH + CHUNK, :] for c in range(c0, c0 + hr // CHUNK)], axis=0)
            ya = jax.nn.gelu(y)
            ys_s[rows, :] = (ya * jax.nn.sigmoid(_dot(ya.astype(BF16), wglu_ref[...]))).astype(BF16)
            h_ref[rows, :] = x

    xn = xn_s[...]
    ga = jax.nn.sigmoid(_dot(xn, wga_ref[...]))
    gb = jax.nn.sigmoid(_dot(xn, wgb_ref[...]))
    mixed = ga * _dot(ys_s[...], wbs_ref[...]) + gb * _dot(o_ref[...], wba_ref[...])
    h_ref[...] += _dot(mixed.astype(BF16), wout_ref[...])


def _mix(x2d, y2d, o2d, norm1, wglu, wgate, wbs, wba, wout, tm):
    t = x2d.shape[0]
    tn = wbs.shape[2]
    nj = D_MODEL // tn
    tok = lambda w: pl.BlockSpec((tm, w), lambda i, j: (i, 0))
    colblk = lambda rows, off: pl.BlockSpec((None, rows, tn), lambda i, j: (off + j, 0, 0))
    return pl.pallas_call(
        _mix_kernel,
        grid=(t // tm, nj),
        in_specs=[tok(D_MODEL), pl.BlockSpec((tm // CHUNK * PITCH, D_SSM), lambda i, j: (i, 0)), tok(D_ATTN),
                  pl.BlockSpec((1, D_MODEL), lambda i, j: (0, 0)),
                  pl.BlockSpec((D_SSM, D_SSM), lambda i, j: (0, 0)),
                  colblk(D_MODEL, 0), colblk(D_MODEL, nj), colblk(D_SSM, 0), colblk(D_ATTN, 0),
                  pl.BlockSpec((tn, D_MODEL), lambda i, j: (j, 0))],
        out_specs=tok(D_MODEL),
        out_shape=jax.ShapeDtypeStruct((t, D_MODEL), F32),
        scratch_shapes=[pltpu.VMEM((tm, D_MODEL), BF16), pltpu.VMEM((tm, D_SSM), BF16)],
        compiler_params=_params("arbitrary", "arbitrary"),
        name="mix",
    )(x2d, y2d, o2d, norm1, wglu, wgate, wgate, wbs, wba, wout)


def _ffn_kernel(h_ref, n2_ref, wg_ref, wu_ref, wd_ref, out_ref, hn_s):
    j = pl.program_id(1)

    @pl.when(j == 0)
    def _():
        h = h_ref[...]
        hn_s[...] = _rms(h, n2_ref[...]).astype(BF16)
        out_ref[...] = h

    hn = hn_s[...]
    act = jax.nn.silu(_dot(hn, wg_ref[...])) * _dot(hn, wu_ref[...])
    out_ref[...] += _dot(act.astype(BF16), wd_ref[...])


def _ffn(h2d, norm2, wg, wu, wd, tm):
    t = h2d.shape[0]
    nf, _, tf = wg.shape
    d_ff = nf * tf
    tok = pl.BlockSpec((tm, D_MODEL), lambda i, j: (i, 0))
    return pl.pallas_call(
        _ffn_kernel,
        grid=(t // tm, d_ff // tf),
        in_specs=[tok, pl.BlockSpec((1, D_MODEL), lambda i, j: (0, 0)),
                  pl.BlockSpec((None, D_MODEL, tf), lambda i, j: (j, 0, 0)),
                  pl.BlockSpec((None, D_MODEL, tf), lambda i, j: (j, 0, 0)),
                  pl.BlockSpec((tf, D_MODEL), lambda i, j: (j, 0))],
        out_specs=tok,
        out_shape=jax.ShapeDtypeStruct((t, D_MODEL), F32),
        scratch_shapes=[pltpu.VMEM((tm, D_MODEL), BF16)],
        compiler_params=_params("arbitrary", "arbitrary"),
        name="ffn",
    )(h2d, norm2, wg, wu, wd)


def kernel(x_prompt, x_sample, cache_k, cache_v, state_ssm_re, state_ssm_im, norm1, w_in, q_norm, k_norm,
           sinks, ssm_a_re, ssm_a_im, ssm_log_dt, ssm_b_re, ssm_b_im, ssm_c_re, ssm_c_im, ssm_d, w_glu,
           w_br_ssm, w_br_attn, w_gate, w_out, norm2, w_ffn_gate, w_ffn_up, w_ffn_down):
    bp, lp, _ = x_prompt.shape
    bs, ls, _ = x_sample.shape
    ncp = lp // CHUNK
    assert ls == CHUNK and cache_k.shape[2] == WINDOW and norm1.shape[0] == 1
    l = 0

    f32 = lambda a: a[l].astype(F32)
    n1 = norm1[l].astype(F32).reshape(1, D_MODEL)
    n2 = norm2[l].astype(F32).reshape(1, D_MODEL)
    qg = jnp.tile(q_norm[l].astype(F32), 2).reshape(1, 128)
    kg = jnp.tile(k_norm[l].astype(F32), 2).reshape(1, 128)
    sink = sinks[l].astype(F32) * LOG2E

    blk, wst, wout_t, a2, w_in_b = _ssm_prep(ssm_a_re[l], ssm_a_im[l], ssm_log_dt[l], ssm_b_re[l], ssm_b_im[l],
                                             ssm_c_re[l], ssm_c_im[l], ssm_d[l], f32(w_in))

    xp = x_prompt.astype(F32).reshape(bp * lp, D_MODEL)
    xs = x_sample.astype(F32).reshape(bs * ls, D_MODEL)
    tm = TM_IN
    up, qp, kp, vp, wfg_b, wfu_b, wfd_b = _in_proj(
        xp, np.arange(lp), tm, n1, w_in_b, qg, kg,
        ride=((f32(w_ffn_gate), TF_FFN), (f32(w_ffn_up), TF_FFN), f32(w_ffn_down)))
    us, qs, ks, vs = _in_proj(xs, np.tile(PAST_LEN + np.arange(ls), tm // ls), tm, n1, w_in_b, qg, kg)

    h0s = jnp.concatenate([state_ssm_re[l], state_ssm_im[l]], axis=-1).astype(F32)
    h0s = jnp.swapaxes(h0s, 0, 1).reshape(N_GROUPS * bs, 2 * STATE)
    y_p, y_s, hfp, hfs = _ssm(up, us, blk, wst, wout_t, a2, h0s, bp, ncp, bs)

    def states(hf, b):
        hf = jnp.swapaxes(hf.reshape(N_GROUPS, b, 2 * STATE), 0, 1)
        return hf[None, :, :, :STATE], hf[None, :, :, STATE:]

    kp3, vp3 = kp.reshape(bp, lp, D_KV), vp.reshape(bp, lp, D_KV)
    o_p, wgate_b, wout_b, wbs_b, wba_b, wglu_b = _attention(
        sink, qp.reshape(bp, lp, D_ATTN), kp3, kp3, vp3, vp3, 1024, 2 * CHUNK, True,
        ride=((f32(w_gate), TN_MIX), f32(w_out), (f32(w_br_ssm), TN_MIX), (f32(w_br_attn), TN_MIX), f32(w_glu)))
    ck = cache_k[l].astype(F32).reshape(bs, WINDOW, D_KV)
    cv = cache_v[l].astype(F32).reshape(bs, WINDOW, D_KV)
    ks3, vs3 = ks.reshape(bs, ls, D_KV), vs.reshape(bs, ls, D_KV)
    o_s, = _attention(sink, qs.reshape(bs, ls, D_ATTN), ck, ks3, cv, vs3, ls, ls, False, bb=8)

    outs = []
    for x2d, y2d, o3d in ((xp, y_p, o_p), (xs, y_s, o_s)):
        h = _mix(x2d, y2d, o3d.reshape(-1, D_ATTN), n1, wglu_b, wgate_b, wbs_b, wba_b, wout_b, TM_MIX)
        outs.append(_ffn(h, n2, wfg_b, wfu_b, wfd_b, TM_FFN))

    win = lambda a, b, n: a.reshape(b, n, N_KV, HEAD_DIM)[None]
    k_win_p = win(kp3[:, lp - WINDOW:], bp, WINDOW)
    v_win_p = win(vp3[:, lp - WINDOW:], bp, WINDOW)
    k_win_s = win(jnp.concatenate([ck, ks3], axis=1)[:, ls:], bs, WINDOW)
    v_win_s = win(jnp.concatenate([cv, vs3], axis=1)[:, ls:], bs, WINDOW)
    re_p, im_p = states(hfp, bp)
    re_s, im_s = states(hfs, bs)
    dt = x_prompt.dtype
    return (outs[0].reshape(bp, lp, D_MODEL).astype(dt), outs[1].reshape(bs, ls, D_MODEL).astype(dt),
            k_win_p.astype(dt), v_win_p.astype(dt), re_p.astype(dt), im_p.astype(dt),
            k_win_s.astype(dt), v_win_s.astype(dt), re_s.astype(dt), im_s.astype(dt))
```

```python
import functools

import jax
import jax.numpy as jnp
import numpy as np
from jax import lax
from jax.experimental import pallas as pl
from jax.experimental.pallas import tpu as pltpu

D_MODEL = 2048
CHUNK = 64
D_SSM = 1024
GROUP_CH = 16
N_GROUPS = 64
STATE = 64
HEAD_DIM = 64
N_HEADS = 16
N_KV = 4
Q_PER_KV = 4
D_ATTN = 1024
D_KV = 256
WINDOW = 128
ROT_DIM = 16
ROPE_THETA = 500000.0
PAST_LEN = 1024
EPS = 1e-6
NEG = -1e30
LOG2E = 1.4426950408889634

LANES = 128
BLK = 256
GROUPS_PER_BLOCK = LANES // GROUP_CH
PITCH = CHUNK + 8
STEPS_PER_BLK = BLK // GROUP_CH
N_BLK = CHUNK // STEPS_PER_BLK
V7X_VMEM_LIMIT = 56 * 1024 * 1024
TM_IN, TM_MIX, TN_MIX, TM_FFN, TF_FFN = 512, 512, 512, 1024, 512

F32 = jnp.float32
BF16 = jnp.bfloat16


def _dot(a, b):
    return jnp.dot(a, b, preferred_element_type=F32)


def _dot_nt(a, b):
    return lax.dot_general(a, b, (((1,), (1,)), ((), ())), preferred_element_type=F32)


def _params(*sem):
    return pltpu.CompilerParams(dimension_semantics=sem, vmem_limit_bytes=V7X_VMEM_LIMIT)


def _cast_riders(weights, nsteps, step_of):
    in_specs, out_specs, out_shape = [], [], []
    for item in weights:
        w, cb = item if isinstance(item, tuple) else (item, None)
        rows, cols = w.shape
        slab = rows // nsteps
        assert slab * nsteps == rows and slab % 16 == 0, (w.shape, nsteps)
        in_specs.append(pl.BlockSpec((slab, cols), lambda *ids: (step_of(*ids), 0)))
        if cb is None:
            out_specs.append(pl.BlockSpec((slab, cols), lambda *ids: (step_of(*ids), 0)))
            out_shape.append(jax.ShapeDtypeStruct(w.shape, BF16))
        else:
            out_specs.append(pl.BlockSpec((cols // cb, slab, cb), lambda *ids: (0, step_of(*ids), 0)))
            out_shape.append(jax.ShapeDtypeStruct((cols // cb, rows, cb), BF16))
    return in_specs, out_specs, out_shape


def _run_riders(src_refs, dst_refs):
    for s, d in zip(src_refs, dst_refs):
        if len(d.shape) == 2:
            d[...] = s[...].astype(d.dtype)
        else:
            cb = d.shape[2]
            for c in range(d.shape[0]):
                d[c] = s[:, c * cb:(c + 1) * cb].astype(d.dtype)


def _ssm_prep_kernel(*refs):
    ins, wf_ref, outs, wb_ref, scratch = refs[:8], refs[8], refs[9:13], refs[13], refs[14:]
    _run_riders([wf_ref], [wb_ref])
    for g in range(ins[0].shape[0]):
        _ssm_prep_group(*[r.at[g] for r in ins + outs + scratch])


def _ssm_prep_group(are_ref, aim_ref, ldt_ref, bt_ref, bts_ref, c_ref, cs_ref, d_ref,
                    blk_ref, wst_ref, wout_ref, a2_ref, l_s, z_s, w_s):
    lane = lax.broadcasted_iota(jnp.int32, (1, 2 * STATE), 1)
    sg = jnp.where(lane < STATE, 1.0, -1.0).astype(F32)
    a_re = are_ref[...]
    a_im = aim_ref[...]
    dt = jnp.exp(ldt_ref[...])
    lre = a_re * dt
    lim = a_im * dt

    def cpow(e):
        mag = jnp.exp(e * lre)
        ang = e * lim
        return mag * jnp.cos(ang), mag * jnp.sin(ang)

    def cmul(ar, ai, br, bi):
        return ar * br - ai * bi, ar * bi + ai * br

    col = lax.broadcasted_iota(jnp.int32, (STEPS_PER_BLK, 1), 0).astype(F32)
    t_r, t_i = cpow(col)
    r_r, r_i = cpow(float(STEPS_PER_BLK - 1) - col)
    h_r, h_i = cpow(float(STEPS_PER_BLK) * col[:8])
    ab_r, ab_i = t_r[1:2], t_i[1:2]

    fr, fi = ab_r - 1.0, ab_i
    den = a_re * a_re + a_im * a_im
    cr = (fr * a_re + fi * a_im) / den
    ci = (fi * a_re - fr * a_im) / den
    bb = cr * bt_ref[...] - (ci * sg) * bts_ref[...]
    bbs = cr * bts_ref[...] + (ci * sg) * bt_ref[...]
    cc = c_ref[...]
    ccs = cs_ref[...]

    def outer_rows(pr, pi, m, ms, dst, row0, conj):
        pa = pr * sg if conj else pr
        pb = -pi if conj else -(pi * sg)
        for r in range(pr.shape[0]):
            z = pa[r:r + 1, :] * m + pb[r:r + 1, :] * ms
            dst[row0 + r * GROUP_CH:row0 + (r + 1) * GROUP_CH, :] = z

    e_r, e_i = cmul(ab_r, ab_i, r_r, r_i)
    outer_rows(e_r, e_i, bb, bbs, l_s, 0, False)
    l2 = l_s[...].astype(BF16)

    for d in range(N_BLK - 1):
        e_r, e_i = cmul(h_r[d:d + 1], h_i[d:d + 1], t_r, t_i)
        outer_rows(e_r, e_i, cc, ccs, z_s, 0, True)
        blk_ref[d + 1] = _dot_nt(z_s[...].astype(BF16), l2).astype(BF16)

    outer_rows(r_r, r_i, bb, bbs, z_s, 0, False)
    kr = _dot_nt((cc * sg).astype(BF16), z_s[...].astype(BF16))
    rows = lax.broadcasted_iota(jnp.int32, (GROUP_CH, BLK), 0)
    lanes = lax.broadcasted_iota(jnp.int32, (GROUP_CH, BLK), 1)
    kr = kr + jnp.where(lanes == rows + (BLK - GROUP_CH), d_ref[...], 0.0)
    for t0 in range(STEPS_PER_BLK):
        sh = GROUP_CH * (STEPS_PER_BLK - 1 - t0)
        piece = kr if sh == 0 else jnp.where(lanes < BLK - sh, pltpu.roll(kr, BLK - sh, axis=1), 0.0)
        blk_ref[0, t0 * GROUP_CH:(t0 + 1) * GROUP_CH, :] = piece.astype(BF16)

    for sb in range(N_BLK):
        m = N_BLK - 1 - sb
        e_r, e_i = cmul(h_r[m:m + 1], h_i[m:m + 1], r_r, r_i)
        outer_rows(e_r, e_i, bb, bbs, w_s, sb * BLK, False)
    wst_ref[...] = w_s[...].astype(BF16)

    for tb in range(N_BLK):
        e_r, e_i = cmul(*cmul(ab_r, ab_i, h_r[tb:tb + 1], h_i[tb:tb + 1]), t_r, t_i)
        outer_rows(e_r, e_i, cc, ccs, w_s, tb * BLK, True)
    wout_ref[...] = w_s[...].astype(BF16)

    a2_ref[0:1, :] = h_r[N_BLK:N_BLK + 1]
    a2_ref[1:2, :] = -(h_i[N_BLK:N_BLK + 1] * sg)


def _ssm_prep(a_re, a_im, log_dt, b_re, b_im, c_re, c_im, d_skip, ride):
    g = N_GROUPS
    ng = GROUPS_PER_BLOCK
    r_in, r_out, r_shape = _cast_riders([ride], g // ng, lambda i: i)
    f = lambda a: a.astype(F32)
    dup = lambda a: jnp.concatenate([f(a), f(a)], axis=-1).reshape(g, 1, 2 * STATE)
    bt_re = jnp.swapaxes(f(b_re), 1, 2)
    bt_im = jnp.swapaxes(f(b_im), 1, 2)
    bt = jnp.concatenate([bt_re, bt_im], axis=-1)
    bts = jnp.concatenate([bt_im, bt_re], axis=-1)
    cc = jnp.concatenate([f(c_re), f(c_im)], axis=-1)
    ccs = jnp.concatenate([f(c_im), f(c_re)], axis=-1)
    d_pad = jnp.pad(f(d_skip).reshape(g, 1, GROUP_CH), ((0, 0), (0, 0), (BLK - GROUP_CH, 0)))
    vec = lambda n: pl.BlockSpec((ng, 1, n), lambda i: (i, 0, 0))
    mat = pl.BlockSpec((ng, GROUP_CH, 2 * STATE), lambda i: (i, 0, 0))
    kdim = CHUNK * GROUP_CH
    return pl.pallas_call(
        _ssm_prep_kernel,
        grid=(g // ng,),
        in_specs=[vec(2 * STATE), vec(2 * STATE), vec(1), mat, mat, mat, mat, vec(BLK)] + r_in,
        out_specs=[pl.BlockSpec((ng, N_BLK, BLK, BLK), lambda i: (i, 0, 0, 0)),
                   pl.BlockSpec((ng, kdim, 2 * STATE), lambda i: (i, 0, 0)),
                   pl.BlockSpec((ng, kdim, 2 * STATE), lambda i: (i, 0, 0)),
                   pl.BlockSpec((ng, 2, 2 * STATE), lambda i: (i, 0, 0))] + r_out,
        out_shape=[jax.ShapeDtypeStruct((g, N_BLK, BLK, BLK), BF16),
                   jax.ShapeDtypeStruct((g, kdim, 2 * STATE), BF16),
                   jax.ShapeDtypeStruct((g, kdim, 2 * STATE), BF16),
                   jax.ShapeDtypeStruct((g, 2, 2 * STATE), F32)] + r_shape,
        scratch_shapes=[pltpu.VMEM((ng, BLK, 2 * STATE), F32), pltpu.VMEM((ng, BLK, 2 * STATE), F32),
                        pltpu.VMEM((ng, kdim, 2 * STATE), F32)],
        compiler_params=_params("arbitrary"),
        name="ssm_prep",
    )(dup(a_re), dup(a_im), f(log_dt).reshape(g, 1, 1), bt, bts, cc, ccs, d_pad, ride)


def _rms(x, g):
    return x * lax.rsqrt(jnp.mean(x * x, axis=-1, keepdims=True) + EPS) * g


def _in_proj_kernel(x_ref, n1_ref, w_ref, qg_ref, kg_ref, rc_ref, rm_ref, rp_ref, *rest):
    n_ride = (len(rest) - 4) // 2
    u_ref, q_ref, k_ref, v_ref = rest[n_ride:n_ride + 4]
    _run_riders(rest[:n_ride], rest[n_ride + 4:])
    hr = x_ref.shape[0] // 2
    lo = lax.broadcasted_iota(jnp.int32, (hr, LANES), 1) < HEAD_DIM

    for r in range(2):
        rows = slice(r * hr, (r + 1) * hr)
        xn = _rms(x_ref[rows, :], n1_ref[...]).astype(BF16)
        rc = rc_ref[rows, :]
        rm = rm_ref[rows, :]
        rp = rp_ref[rows, :]

        def norm_rope(z, gain, width, scale):
            outs = []
            for cb in range(width // LANES):
                zb = z[:, cb * LANES:(cb + 1) * LANES]
                sq = zb * zb
                s_lo = jnp.sum(jnp.where(lo, sq, 0.0), axis=-1, keepdims=True)
                s_hi = jnp.sum(jnp.where(lo, 0.0, sq), axis=-1, keepdims=True)
                ms = jnp.where(lo, s_lo, s_hi) * (1.0 / HEAD_DIM)
                zn = zb * lax.rsqrt(ms + EPS) * gain
                rot = zn * rc + pltpu.roll(zn, LANES - ROT_DIM // 2, axis=1) * rm \
                    + pltpu.roll(zn, ROT_DIM // 2, axis=1) * rp
                outs.append(rot * scale)
            return outs

        q = _dot(xn, w_ref[:, D_SSM:D_SSM + D_ATTN])
        k = _dot(xn, w_ref[:, D_SSM + D_ATTN:D_SSM + D_ATTN + D_KV])
        for cb, blk in enumerate(norm_rope(q, qg_ref[...], D_ATTN, HEAD_DIM ** -0.5 * LOG2E)):
            q_ref[rows, cb * LANES:(cb + 1) * LANES] = blk.astype(q_ref.dtype)
        for cb, blk in enumerate(norm_rope(k, kg_ref[...], D_KV, 1.0)):
            k_ref[rows, cb * LANES:(cb + 1) * LANES] = blk
        u = _dot(xn, w_ref[:, 0:D_SSM])
        for c in range(hr // CHUNK):
            base = (r * (hr // CHUNK) + c) * PITCH
            u_ref[base:base + CHUNK, :] = u[c * CHUNK:(c + 1) * CHUNK]
            u_ref[base + CHUNK:base + PITCH, :] = jnp.zeros((PITCH - CHUNK, D_SSM), F32)
        v_ref[rows, :] = _dot(xn, w_ref[:, D_SSM + D_ATTN + D_KV:])


def _rope_tables(pos):
    half = ROT_DIM // 2
    inv = ROPE_THETA ** (-np.arange(half, dtype=np.float64) * 2.0 / ROT_DIM)
    ang = pos.astype(np.float64)[:, None] * inv[None, :]
    cos, sin = np.cos(ang), np.sin(ang)
    n = pos.shape[0]
    ones = np.ones((n, HEAD_DIM - ROT_DIM))
    zeros = np.zeros((n, HEAD_DIM - ROT_DIM))
    zh = np.zeros((n, half))
    rc = np.concatenate([cos, cos, ones], axis=1)
    rm = np.concatenate([-sin, zh, zeros], axis=1)
    rp = np.concatenate([zh, sin, zeros], axis=1)
    two = lambda t: jnp.asarray(np.concatenate([t, t], axis=1), F32)
    return two(rc), two(rm), two(rp)


def _in_proj(x2d, pos, tm, norm1, w_in_b, qg, kg, ride=()):
    t = x2d.shape[0]
    r_in, r_out, r_shape = _cast_riders(ride, t // tm, lambda i: i)
    rc, rm, rp = _rope_tables(pos)
    nrope = pos.shape[0] // tm
    rope_spec = pl.BlockSpec((tm, LANES), lambda i: (i % nrope, 0))
    const = lambda shape: pl.BlockSpec(shape, lambda i: (0, 0))
    tok = lambda w: pl.BlockSpec((tm, w), lambda i: (i, 0))
    return pl.pallas_call(
        _in_proj_kernel,
        grid=(t // tm,),
        in_specs=[tok(D_MODEL), const((1, D_MODEL)), const(w_in_b.shape),
                  const((1, LANES)), const((1, LANES)), rope_spec, rope_spec, rope_spec] + r_in,
        out_specs=[pl.BlockSpec((tm // CHUNK * PITCH, D_SSM), lambda i: (i, 0)), tok(D_ATTN), tok(D_KV),
                   tok(D_KV)] + r_out,
        out_shape=[jax.ShapeDtypeStruct((t // CHUNK * PITCH, D_SSM), F32), jax.ShapeDtypeStruct((t, D_ATTN), BF16),
                   jax.ShapeDtypeStruct((t, D_KV), F32), jax.ShapeDtypeStruct((t, D_KV), F32)] + r_shape,
        compiler_params=_params("arbitrary"),
        name="in_proj",
    )(x2d, norm1, w_in_b, qg, kg, rc, rm, rp, *[r[0] if isinstance(r, tuple) else r for r in ride])


def _ssm_state_kernel(up_ref, us_ref, wst_ref, xtp_ref, xts_ref, sp_ref, ss_ref, *, mp, bs):
    ng = xtp_ref.shape[0]
    pad = jnp.zeros((LANES - bs, LANES), F32)
    for s in range(CHUNK):
        vp = up_ref[pl.ds(s, mp, stride=PITCH), :].T.astype(BF16)
        vs = jnp.concatenate([us_ref[pl.ds(s, bs, stride=PITCH), :], pad], axis=0).T.astype(BF16)
        for g in range(ng):
            xtp_ref[g, s * GROUP_CH:(s + 1) * GROUP_CH, :] = vp[g * GROUP_CH:(g + 1) * GROUP_CH, :]
            xts_ref[g, s * GROUP_CH:(s + 1) * GROUP_CH, :] = vs[g * GROUP_CH:(g + 1) * GROUP_CH, :]
    tn = (((0,), (0,)), ((), ()))
    for g in range(ng):
        st = lax.dot_general(wst_ref[g], xtp_ref[g], tn, preferred_element_type=F32)
        sp_ref[g * mp:(g + 1) * mp, :] = st.T
        st = lax.dot_general(wst_ref[g], xts_ref[g], tn, preferred_element_type=F32)
        ss_ref[g * bs:(g + 1) * bs, :] = st.T[:bs]


def _cmul_add(h, ar2, ai2, s):
    return h * ar2 + pltpu.roll(h, STATE, axis=1) * ai2 + s


def _ssm_out_kernel(xtp_ref, xts_ref, blk_ref, wout_ref, sp_ref, ss_ref, arp_ref, aip_ref, ars_ref, ais_ref,
                    h0s_ref, yp_ref, ys_ref, hfp_ref, hfs_ref, hin_s, stp_s, sts_s, *, mp, bs, nchunk):
    ng = xtp_ref.shape[0]
    rows = mp // nchunk * ng
    ar = arp_ref[...]
    ai = aip_ref[...]
    ai_sw = pltpu.roll(ai, STATE, axis=1)
    s_all = [sp_ref[pl.ds(k, rows, stride=nchunk), :] for k in range(nchunk)]
    s_sw = [pltpu.roll(s, STATE, axis=1) for s in s_all]
    h = jnp.zeros((rows, 2 * STATE), F32)
    h_sw = h
    for k in range(nchunk):
        hin_s[pl.ds(k, rows, stride=nchunk), :] = h
        h, h_sw = h * ar + h_sw * ai + s_all[k], h_sw * ar + h * ai_sw + s_sw[k]
    hfp_ref[...] = h
    hfs_ref[...] = _cmul_add(h0s_ref[...], ars_ref[...], ais_ref[...], ss_ref[...])

    pad = jnp.zeros((LANES - bs, 2 * STATE), F32)
    hp = [hin_s[g * mp:(g + 1) * mp, :].astype(BF16) for g in range(ng)]
    hs = [jnp.concatenate([h0s_ref[g * bs:(g + 1) * bs, :], pad], axis=0).astype(BF16) for g in range(ng)]
    for tb in range(N_BLK):
        for g in range(ng):
            accp = accs = None
            for sb in range(tb + 1):
                b = blk_ref[g, tb - sb]
                dp = _dot(b, xtp_ref[g, sb * BLK:(sb + 1) * BLK, :])
                ds = _dot(b, xts_ref[g, sb * BLK:(sb + 1) * BLK, :])
                accp = dp if accp is None else accp + dp
                accs = ds if accs is None else accs + ds
            w = wout_ref[g, tb * BLK:(tb + 1) * BLK, :]
            accp = accp + _dot_nt(w, hp[g])
            accs = accs + _dot_nt(w, hs[g])
            for t0 in range(STEPS_PER_BLK):
                stp_s[t0, g * GROUP_CH:(g + 1) * GROUP_CH, :] = accp[t0 * GROUP_CH:(t0 + 1) * GROUP_CH, :]
                sts_s[t0, g * GROUP_CH:(g + 1) * GROUP_CH, :] = accs[t0 * GROUP_CH:(t0 + 1) * GROUP_CH, :]
        for t0 in range(STEPS_PER_BLK):
            t = tb * STEPS_PER_BLK + t0
            yp_ref[pl.ds(t, mp, stride=PITCH), :] = stp_s[t0].T
            ys_ref[pl.ds(t, bs, stride=PITCH), :] = sts_s[t0].T[:bs]
    for r in range(CHUNK, PITCH):
        yp_ref[pl.ds(r, mp, stride=PITCH), :] = jnp.zeros((mp, LANES), F32)
        ys_ref[pl.ds(r, bs, stride=PITCH), :] = jnp.zeros((bs, LANES), F32)


def _ssm(up, us, blk, wst, wout, a2, h0s, bp, nchunk, bs):
    g = N_GROUPS
    ng = GROUPS_PER_BLOCK
    mp = bp * nchunk
    kdim = CHUNK * GROUP_CH
    tok = lambda a: pl.BlockSpec((a.shape[0], LANES), lambda i: (0, i))
    grp = lambda *tail: pl.BlockSpec((ng,) + tail, lambda i: (i,) + (0,) * len(tail))
    rows = lambda r: pl.BlockSpec((ng * r, 2 * STATE), lambda i: (i, 0))
    xtp, xts, sp, ss = pl.pallas_call(
        functools.partial(_ssm_state_kernel, mp=mp, bs=bs),
        grid=(g // ng,),
        in_specs=[tok(up), tok(us), grp(kdim, 2 * STATE)],
        out_specs=[grp(kdim, mp), grp(kdim, LANES), rows(mp), rows(bs)],
        out_shape=[jax.ShapeDtypeStruct((g, kdim, mp), BF16), jax.ShapeDtypeStruct((g, kdim, LANES), BF16),
                   jax.ShapeDtypeStruct((g * mp, 2 * STATE), F32),
                   jax.ShapeDtypeStruct((g * bs, 2 * STATE), F32)],
        compiler_params=_params("arbitrary"),
        name="ssm_state",
    )(up, us, wst)

    arp = jnp.repeat(a2[:, 0, :], bp, axis=0)
    aip = jnp.repeat(a2[:, 1, :], bp, axis=0)
    ars = jnp.repeat(a2[:, 0, :], bs, axis=0)
    ais = jnp.repeat(a2[:, 1, :], bs, axis=0)
    yp, ys, hfp, hfs = pl.pallas_call(
        functools.partial(_ssm_out_kernel, mp=mp, bs=bs, nchunk=nchunk),
        grid=(g // ng,),
        in_specs=[grp(kdim, mp), grp(kdim, LANES), grp(N_BLK, BLK, BLK), grp(kdim, 2 * STATE), rows(mp), rows(bs),
                  rows(bp), rows(bp), rows(bs), rows(bs), rows(bs)],
        out_specs=[tok(up), tok(us), rows(bp), rows(bs)],
        out_shape=[jax.ShapeDtypeStruct(up.shape, F32), jax.ShapeDtypeStruct(us.shape, F32),
                   jax.ShapeDtypeStruct((g * bp, 2 * STATE), F32), jax.ShapeDtypeStruct((g * bs, 2 * STATE), F32)],
        scratch_shapes=[pltpu.VMEM((ng * mp, 2 * STATE), F32), pltpu.VMEM((STEPS_PER_BLK, LANES, mp), F32),
                        pltpu.VMEM((STEPS_PER_BLK, LANES, LANES), F32)],
        compiler_params=_params("arbitrary"),
        name="ssm_out",
    )(xtp, xts, blk, wout, sp, ss, arp, aip, ars, ais, h0s)
    return yp, ys, hfp, hfs


def _attn_kernel(sinks_ref, bias_ref, q_ref, kp_ref, kc_ref, vp_ref, vc_ref, *rest, qp, kw, masked):
    n_ride = (len(rest) - 1) // 2
    o_ref = rest[n_ride]
    _run_riders(rest[:n_ride], rest[n_ride + 1:])
    for e in range(q_ref.shape[0]):
        _attn_tile(sinks_ref, bias_ref, q_ref.at[e], kp_ref.at[e], kc_ref.at[e], vp_ref.at[e], vc_ref.at[e],
                   o_ref.at[e], qp=qp, kw=kw, masked=masked)


def _attn_tile(sinks_ref, bias_ref, q_ref, kp_ref, kc_ref, vp_ref, vc_ref, o_ref, *, qp, kw, masked):
    i = pl.program_id(1)
    n_units = q_ref.shape[0] // qp
    gw = Q_PER_KV * HEAD_DIM
    half = HEAD_DIM
    nl = Q_PER_KV * qp

    def duplicated(prev_ref, cur_ref):
        a = jnp.concatenate([prev_ref[...], cur_ref[...]], axis=0)
        lo = lax.broadcasted_iota(jnp.int32, (a.shape[0], 2 * half), 1) < half
        out = []
        for c in range(N_KV // 2):
            col = a[:, c * 2 * half:(c + 1) * 2 * half]
            rot = pltpu.roll(col, half, axis=1)
            out.append(jnp.where(lo, col, rot).astype(BF16))
            out.append(jnp.where(lo, rot, col).astype(BF16))
        return out

    kdup = duplicated(kp_ref, kc_ref)
    vdup = duplicated(vp_ref, vc_ref)
    pair_lo = lax.broadcasted_iota(jnp.int32, (qp, 2 * half), 1) < half
    lane_g = lax.broadcasted_iota(jnp.int32, (1, nl), 1) // qp

    for u in range(n_units):
        rows = slice(u * qp, (u + 1) * qp)
        win = slice(u * qp, u * qp + kw)
        if masked:
            bias = bias_ref[jnp.where(i == 0, 0, 1)] if u == 0 else bias_ref[1]
        scores = []
        for kv in range(N_KV):
            qm = []
            for g in range(Q_PER_KV):
                pair = q_ref[rows, kv * gw + (g // 2) * 2 * half:kv * gw + (g // 2 + 1) * 2 * half]
                keep = pair_lo if g % 2 == 0 else jnp.logical_not(pair_lo)
                qm.append(jnp.where(keep, pair, jnp.zeros_like(pair)))
            scores.append(_dot_nt(kdup[kv][win], jnp.concatenate(qm, axis=0)))
        probs = []
        for kv in range(N_KV):
            s = scores[kv] + bias if masked else scores[kv]
            sink = jnp.zeros((1, nl), F32)
            for g in range(Q_PER_KV):
                sink = jnp.where(lane_g == g, sinks_ref[kv * Q_PER_KV + g], sink)
            m = jnp.maximum(jnp.max(s, axis=0, keepdims=True), sink)
            p = jnp.exp2(s - m)
            den = jnp.sum(p, axis=0, keepdims=True) + jnp.exp2(sink - m)
            probs.append((p * (1.0 / den)).astype(BF16))
        for kv in range(N_KV):
            o = lax.dot_general(probs[kv], vdup[kv][win], (((0,), (0,)), ((), ())),
                                preferred_element_type=F32)
            for h in range(Q_PER_KV // 2):
                both = jnp.where(pair_lo, o[2 * h * qp:(2 * h + 1) * qp], o[(2 * h + 1) * qp:(2 * h + 2) * qp])
                o_ref[rows, kv * gw + h * 2 * half:kv * gw + (h + 1) * 2 * half] = both.astype(o_ref.dtype)


def _attention(sinks, q, k_prev, k_cur, v_prev, v_cur, qt, qp, masked, bb=1, ride=()):
    b, l, _ = q.shape
    nt = l // qt
    r_in, r_out, r_shape = _cast_riders(ride, b // bb * nt, lambda bi, i: bi * nt + i)
    per = qt // WINDOW
    kw = WINDOW + qp
    kc = np.arange(kw)[:, None] // CHUNK
    qc = (np.arange(Q_PER_KV * qp)[None, :] % qp) // CHUNK
    band = (kc >= qc) & (kc <= qc + WINDOW // CHUNK)
    bias = np.stack([np.where(band & (kc >= WINDOW // CHUNK), 0.0, NEG), np.where(band, 0.0, NEG)])
    bias = jnp.asarray(bias, F32)
    if masked:
        prev_map = lambda bi, i: (bi, jnp.maximum(i * per - 1, 0), 0)
    else:
        prev_map = lambda bi, i: (bi, 0, 0)
    cur = lambda w: pl.BlockSpec((bb, qt, w), lambda bi, i: (bi, i, 0))
    prev = pl.BlockSpec((bb, WINDOW, D_KV), prev_map)
    return pl.pallas_call(
        functools.partial(_attn_kernel, qp=qp, kw=kw, masked=masked),
        grid=(b // bb, nt),
        in_specs=[pl.BlockSpec(memory_space=pltpu.SMEM), pl.BlockSpec(bias.shape, lambda bi, i: (0, 0, 0)),
                  cur(D_ATTN), prev, cur(D_KV), prev, cur(D_KV)] + r_in,
        out_specs=[cur(D_ATTN)] + r_out,
        out_shape=[jax.ShapeDtypeStruct((b, l, D_ATTN), BF16)] + r_shape,
        compiler_params=_params("arbitrary", "arbitrary"),
        name="attention",
    )(sinks, bias, q, k_prev, k_cur, v_prev, v_cur, *[r[0] if isinstance(r, tuple) else r for r in ride])


def _mix_kernel(x_ref, y_ref, o_ref, n1_ref, wglu_ref, wga_ref, wgb_ref, wbs_ref, wba_ref, wout_ref,
                h_ref, xn_s, ys_s):
    j = pl.program_id(1)

    @pl.when(j == 0)
    def _():
        hr = x_ref.shape[0] // 2
        for r in range(2):
            rows = slice(r * hr, (r + 1) * hr)
            x = x_ref[rows, :]
            xn_s[rows, :] = _rms(x, n1_ref[...]).astype(BF16)
            c0 = r * (hr // CHUNK)
            y = jnp.concatenate([y_ref[c * PITCH:c * PITCH + CHUNK, :] for c in range(c0, c0 + hr // CHUNK)], axis=0)
            ya = jax.nn.gelu(y)
            ys_s[rows, :] = (ya * jax.nn.sigmoid(_dot(ya.astype(BF16), wglu_ref[...]))).astype(BF16)
            h_ref[rows, :] = x

    xn = xn_s[...]
    ga = jax.nn.sigmoid(_dot(xn, wga_ref[...]))
    gb = jax.nn.sigmoid(_dot(xn, wgb_ref[...]))
    mixed = ga * _dot(ys_s[...], wbs_ref[...]) + gb * _dot(o_ref[...], wba_ref[...])
    h_ref[...] += _dot(mixed.astype(BF16), wout_ref[...])


def _mix(x2d, y2d, o2d, norm1, wglu, wgate, wbs, wba, wout, tm):
    t = x2d.shape[0]
    tn = wbs.shape[2]
    nj = D_MODEL // tn
    tok = lambda w: pl.BlockSpec((tm, w), lambda i, j: (i, 0))
    colblk = lambda rows, off: pl.BlockSpec((None, rows, tn), lambda i, j: (off + j, 0, 0))
    return pl.pallas_call(
        _mix_kernel,
        grid=(t // tm, nj),
        in_specs=[tok(D_MODEL), pl.BlockSpec((tm // CHUNK * PITCH, D_SSM), lambda i, j: (i, 0)), tok(D_ATTN),
                  pl.BlockSpec((1, D_MODEL), lambda i, j: (0, 0)),
                  pl.BlockSpec((D_SSM, D_SSM), lambda i, j: (0, 0)),
                  colblk(D_MODEL, 0), colblk(D_MODEL, nj), colblk(D_SSM, 0), colblk(D_ATTN, 0),
                  pl.BlockSpec((tn, D_MODEL), lambda i, j: (j, 0))],
        out_specs=tok(D_MODEL),
        out_shape=jax.ShapeDtypeStruct((t, D_MODEL), F32),
        scratch_shapes=[pltpu.VMEM((tm, D_MODEL), BF16), pltpu.VMEM((tm, D_SSM), BF16)],
        compiler_params=_params("arbitrary", "arbitrary"),
        name="mix",
    )(x2d, y2d, o2d, norm1, wglu, wgate, wgate, wbs, wba, wout)


def _ffn_kernel(h_ref, n2_ref, wg_ref, wu_ref, wd_ref, out_ref, hn_s):
    j = pl.program_id(1)

    @pl.when(j == 0)
    def _():
        h = h_ref[...]
        hn_s[...] = _rms(h, n2_ref[...]).astype(BF16)
        out_ref[...] = h

    hn = hn_s[...]
    act = jax.nn.silu(_dot(hn, wg_ref[...])) * _dot(hn, wu_ref[...])
    out_ref[...] += _dot(act.astype(BF16), wd_ref[...])


def _ffn(h2d, norm2, wg, wu, wd, tm):
    t = h2d.shape[0]
    nf, _, tf = wg.shape
    d_ff = nf * tf
    tok = pl.BlockSpec((tm, D_MODEL), lambda i, j: (i, 0))
    return pl.pallas_call(
        _ffn_kernel,
        grid=(t // tm, d_ff // tf),
        in_specs=[tok, pl.BlockSpec((1, D_MODEL), lambda i, j: (0, 0)),
                  pl.BlockSpec((None, D_MODEL, tf), lambda i, j: (j, 0, 0)),
                  pl.BlockSpec((None, D_MODEL, tf), lambda i, j: (j, 0, 0)),
                  pl.BlockSpec((tf, D_MODEL), lambda i, j: (j, 0))],
        out_specs=tok,
        out_shape=jax.ShapeDtypeStruct((t, D_MODEL), F32),
        scratch_shapes=[pltpu.VMEM((tm, D_MODEL), BF16)],
        compiler_params=_params("arbitrary", "arbitrary"),
        name="ffn",
    )(h2d, norm2, wg, wu, wd)


def kernel(x_prompt, x_sample, cache_k, cache_v, state_ssm_re, state_ssm_im, norm1, w_in, q_norm, k_norm,
           sinks, ssm_a_re, ssm_a_im, ssm_log_dt, ssm_b_re, ssm_b_im, ssm_c_re, ssm_c_im, ssm_d, w_glu,
           w_br_ssm, w_br_attn, w_gate, w_out, norm2, w_ffn_gate, w_ffn_up, w_ffn_down):
    bp, lp, _ = x_prompt.shape
    bs, ls, _ = x_sample.shape
    ncp = lp // CHUNK
    assert ls == CHUNK and cache_k.shape[2] == WINDOW and norm1.shape[0] == 1
    l = 0

    f32 = lambda a: a[l].astype(F32)
    n1 = norm1[l].astype(F32).reshape(1, D_MODEL)
    n2 = norm2[l].astype(F32).reshape(1, D_MODEL)
    qg = jnp.tile(q_norm[l].astype(F32), LANES // HEAD_DIM).reshape(1, LANES)
    kg = jnp.tile(k_norm[l].astype(F32), LANES // HEAD_DIM).reshape(1, LANES)
    sink = sinks[l].astype(F32) * LOG2E

    blk, wst, wout_t, a2, w_in_b = _ssm_prep(ssm_a_re[l], ssm_a_im[l], ssm_log_dt[l], ssm_b_re[l], ssm_b_im[l],
                                             ssm_c_re[l], ssm_c_im[l], ssm_d[l], f32(w_in))

    xp = x_prompt.astype(F32).reshape(bp * lp, D_MODEL)
    xs = x_sample.astype(F32).reshape(bs * ls, D_MODEL)
    tm = TM_IN
    up, qp, kp, vp, wfg_b, wfu_b, wfd_b = _in_proj(
        xp, np.arange(lp), tm, n1, w_in_b, qg, kg,
        ride=((f32(w_ffn_gate), TF_FFN), (f32(w_ffn_up), TF_FFN), f32(w_ffn_down)))
    us, qs, ks, vs = _in_proj(xs, np.tile(PAST_LEN + np.arange(ls), tm // ls), tm, n1, w_in_b, qg, kg)

    h0s = jnp.concatenate([state_ssm_re[l], state_ssm_im[l]], axis=-1).astype(F32)
    h0s = jnp.swapaxes(h0s, 0, 1).reshape(N_GROUPS * bs, 2 * STATE)
    y_p, y_s, hfp, hfs = _ssm(up, us, blk, wst, wout_t, a2, h0s, bp, ncp, bs)

    def states(hf, b):
        hf = jnp.swapaxes(hf.reshape(N_GROUPS, b, 2 * STATE), 0, 1)
        return hf[None, :, :, :STATE], hf[None, :, :, STATE:]

    kp3, vp3 = kp.reshape(bp, lp, D_KV), vp.reshape(bp, lp, D_KV)
    o_p, wgate_b, wout_b, wbs_b, wba_b, wglu_b = _attention(
        sink, qp.reshape(bp, lp, D_ATTN), kp3, kp3, vp3, vp3, 2048, 2 * CHUNK, True,
        ride=((f32(w_gate), TN_MIX), f32(w_out), (f32(w_br_ssm), TN_MIX), (f32(w_br_attn), TN_MIX), f32(w_glu)))
    ck = cache_k[l].astype(F32).reshape(bs, WINDOW, D_KV)
    cv = cache_v[l].astype(F32).reshape(bs, WINDOW, D_KV)
    ks3, vs3 = ks.reshape(bs, ls, D_KV), vs.reshape(bs, ls, D_KV)
    o_s, = _attention(sink, qs.reshape(bs, ls, D_ATTN), ck, ks3, cv, vs3, ls, ls, False, bb=8)

    outs = []
    for x2d, y2d, o3d in ((xp, y_p, o_p), (xs, y_s, o_s)):
        h = _mix(x2d, y2d, o3d.reshape(-1, D_ATTN), n1, wglu_b, wgate_b, wbs_b, wba_b, wout_b, TM_MIX)
        outs.append(_ffn(h, n2, wfg_b, wfu_b, wfd_b, TM_FFN))

    win = lambda a, b, n: a.reshape(b, n, N_KV, HEAD_DIM)[None]
    k_win_p = win(kp3[:, lp - WINDOW:], bp, WINDOW)
    v_win_p = win(vp3[:, lp - WINDOW:], bp, WINDOW)
    k_win_s = win(jnp.concatenate([ck, ks3], axis=1)[:, ls:], bs, WINDOW)
    v_win_s = win(jnp.concatenate([cv, vs3], axis=1)[:, ls:], bs, WINDOW)
    re_p, im_p = states(hfp, bp)
    re_s, im_s = states(hfs, bs)
    dt = x_prompt.dtype
    return (outs[0].reshape(bp, lp, D_MODEL).astype(dt), outs[1].reshape(bs, ls, D_MODEL).astype(dt),
            k_win_p.astype(dt), v_win_p.astype(dt), re_p.astype(dt), im_p.astype(dt),
            k_win_s.astype(dt), v_win_s.astype(dt), re_s.astype(dt), im_s.astype(dt))
```

```python
import functools

import jax
import jax.numpy as jnp
import numpy as np
from jax import lax
from jax.experimental import pallas as pl
from jax.experimental.pallas import tpu as pltpu

D_MODEL = 2048
CHUNK = 64
D_SSM = 1024
GROUP_CH = 16
N_GROUPS = 64
STATE = 64
HEAD_DIM = 64
N_HEADS = 16
N_KV = 4
Q_PER_KV = 4
D_ATTN = 1024
D_KV = 256
WINDOW = 128
ROT_DIM = 16
ROPE_THETA = 500000.0
PAST_LEN = 1024
EPS = 1e-6
NEG = -1e30
LOG2E = 1.4426950408889634

LANES = 128
BLK = 256
GROUPS_PER_BLOCK = LANES // GROUP_CH
PITCH = CHUNK + 8
STEPS_PER_BLK = BLK // GROUP_CH
N_BLK = CHUNK // STEPS_PER_BLK
V7X_VMEM_LIMIT = 56 * 1024 * 1024
TM_IN, TM_MIX, TN_MIX, TM_FFN, TF_FFN = 512, 512, 512, 1024, 512

F32 = jnp.float32
BF16 = jnp.bfloat16


def _dot(a, b):
    return jnp.dot(a, b, preferred_element_type=F32)


def _dot_nt(a, b):
    return lax.dot_general(a, b, (((1,), (1,)), ((), ())), preferred_element_type=F32)


def _params(*sem):
    return pltpu.CompilerParams(dimension_semantics=sem, vmem_limit_bytes=V7X_VMEM_LIMIT)


def _cast_riders(weights, nsteps, step_of):
    in_specs, out_specs, out_shape = [], [], []
    for item in weights:
        w, cb = item if isinstance(item, tuple) else (item, None)
        rows, cols = w.shape
        slab = rows // nsteps
        assert slab * nsteps == rows and slab % 16 == 0, (w.shape, nsteps)
        in_specs.append(pl.BlockSpec((slab, cols), lambda *ids: (step_of(*ids), 0)))
        if cb is None:
            out_specs.append(pl.BlockSpec((slab, cols), lambda *ids: (step_of(*ids), 0)))
            out_shape.append(jax.ShapeDtypeStruct(w.shape, BF16))
        else:
            out_specs.append(pl.BlockSpec((cols // cb, slab, cb), lambda *ids: (0, step_of(*ids), 0)))
            out_shape.append(jax.ShapeDtypeStruct((cols // cb, rows, cb), BF16))
    return in_specs, out_specs, out_shape


def _run_riders(src_refs, dst_refs):
    for s, d in zip(src_refs, dst_refs):
        if len(d.shape) == 2:
            d[...] = s[...].astype(d.dtype)
        else:
            cb = d.shape[2]
            for c in range(d.shape[0]):
                d[c] = s[:, c * cb:(c + 1) * cb].astype(d.dtype)


def _ssm_prep_kernel(*refs):
    ins, wf_ref, outs, wb_ref, scratch = refs[:8], refs[8], refs[9:13], refs[13], refs[14:]
    _run_riders([wf_ref], [wb_ref])
    for g in range(ins[0].shape[0]):
        _ssm_prep_group(*[r.at[g] for r in ins + outs + scratch])


def _ssm_prep_group(are_ref, aim_ref, ldt_ref, bt_ref, bts_ref, c_ref, cs_ref, d_ref,
                    blk_ref, wst_ref, wout_ref, a2_ref, l_s, z_s, w_s):
    lane = lax.broadcasted_iota(jnp.int32, (1, 2 * STATE), 1)
    sg = jnp.where(lane < STATE, 1.0, -1.0).astype(F32)
    a_re = are_ref[...]
    a_im = aim_ref[...]
    dt = jnp.exp(ldt_ref[...])
    lre = a_re * dt
    lim = a_im * dt

    def cpow(e):
        mag = jnp.exp(e * lre)
        ang = e * lim
        return mag * jnp.cos(ang), mag * jnp.sin(ang)

    def cmul(ar, ai, br, bi):
        return ar * br - ai * bi, ar * bi + ai * br

    col = lax.broadcasted_iota(jnp.int32, (STEPS_PER_BLK, 1), 0).astype(F32)
    t_r, t_i = cpow(col)
    r_r, r_i = cpow(float(STEPS_PER_BLK - 1) - col)
    h_r, h_i = cpow(float(STEPS_PER_BLK) * col[:8])
    ab_r, ab_i = t_r[1:2], t_i[1:2]

    fr, fi = ab_r - 1.0, ab_i
    den = a_re * a_re + a_im * a_im
    cr = (fr * a_re + fi * a_im) / den
    ci = (fi * a_re - fr * a_im) / den
    bb = cr * bt_ref[...] - (ci * sg) * bts_ref[...]
    bbs = cr * bts_ref[...] + (ci * sg) * bt_ref[...]
    cc = c_ref[...]
    ccs = cs_ref[...]

    def outer_rows(pr, pi, m, ms, dst, row0, conj):
        pa = pr * sg if conj else pr
        pb = -pi if conj else -(pi * sg)
        for r in range(pr.shape[0]):
            z = pa[r:r + 1, :] * m + pb[r:r + 1, :] * ms
            dst[row0 + r * GROUP_CH:row0 + (r + 1) * GROUP_CH, :] = z

    e_r, e_i = cmul(ab_r, ab_i, r_r, r_i)
    outer_rows(e_r, e_i, bb, bbs, l_s, 0, False)
    l2 = l_s[...].astype(BF16)

    for d in range(N_BLK - 1):
        e_r, e_i = cmul(h_r[d:d + 1], h_i[d:d + 1], t_r, t_i)
        outer_rows(e_r, e_i, cc, ccs, z_s, 0, True)
        blk_ref[d + 1] = _dot_nt(z_s[...].astype(BF16), l2).astype(BF16)

    outer_rows(r_r, r_i, bb, bbs, z_s, 0, False)
    kr = _dot_nt((cc * sg).astype(BF16), z_s[...].astype(BF16))
    rows = lax.broadcasted_iota(jnp.int32, (GROUP_CH, BLK), 0)
    lanes = lax.broadcasted_iota(jnp.int32, (GROUP_CH, BLK), 1)
    kr = kr + jnp.where(lanes == rows + (BLK - GROUP_CH), d_ref[...], 0.0)
    for t0 in range(STEPS_PER_BLK):
        sh = GROUP_CH * (STEPS_PER_BLK - 1 - t0)
        piece = kr if sh == 0 else jnp.where(lanes < BLK - sh, pltpu.roll(kr, BLK - sh, axis=1), 0.0)
        blk_ref[0, t0 * GROUP_CH:(t0 + 1) * GROUP_CH, :] = piece.astype(BF16)

    for sb in range(N_BLK):
        m = N_BLK - 1 - sb
        e_r, e_i = cmul(h_r[m:m + 1], h_i[m:m + 1], r_r, r_i)
        outer_rows(e_r, e_i, bb, bbs, w_s, sb * BLK, False)
    wst_ref[...] = w_s[...].astype(BF16)

    for tb in range(N_BLK):
        e_r, e_i = cmul(*cmul(ab_r, ab_i, h_r[tb:tb + 1], h_i[tb:tb + 1]), t_r, t_i)
        outer_rows(e_r, e_i, cc, ccs, w_s, tb * BLK, True)
    wout_ref[...] = w_s[...].astype(BF16)

    a2_ref[0:1, :] = h_r[N_BLK:N_BLK + 1]
    a2_ref[1:2, :] = -(h_i[N_BLK:N_BLK + 1] * sg)


def _ssm_prep(a_re, a_im, log_dt, b_re, b_im, c_re, c_im, d_skip, ride):
    g = N_GROUPS
    ng = GROUPS_PER_BLOCK
    r_in, r_out, r_shape = _cast_riders([ride], g // ng, lambda i: i)
    f = lambda a: a.astype(F32)
    dup = lambda a: jnp.concatenate([f(a), f(a)], axis=-1).reshape(g, 1, 2 * STATE)
    bt_re = jnp.swapaxes(f(b_re), 1, 2)
    bt_im = jnp.swapaxes(f(b_im), 1, 2)
    bt = jnp.concatenate([bt_re, bt_im], axis=-1)
    bts = jnp.concatenate([bt_im, bt_re], axis=-1)
    cc = jnp.concatenate([f(c_re), f(c_im)], axis=-1)
    ccs = jnp.concatenate([f(c_im), f(c_re)], axis=-1)
    d_pad = jnp.pad(f(d_skip).reshape(g, 1, GROUP_CH), ((0, 0), (0, 0), (BLK - GROUP_CH, 0)))
    vec = lambda n: pl.BlockSpec((ng, 1, n), lambda i: (i, 0, 0))
    mat = pl.BlockSpec((ng, GROUP_CH, 2 * STATE), lambda i: (i, 0, 0))
    kdim = CHUNK * GROUP_CH
    return pl.pallas_call(
        _ssm_prep_kernel,
        grid=(g // ng,),
        in_specs=[vec(2 * STATE), vec(2 * STATE), vec(1), mat, mat, mat, mat, vec(BLK)] + r_in,
        out_specs=[pl.BlockSpec((ng, N_BLK, BLK, BLK), lambda i: (i, 0, 0, 0)),
                   pl.BlockSpec((ng, kdim, 2 * STATE), lambda i: (i, 0, 0)),
                   pl.BlockSpec((ng, kdim, 2 * STATE), lambda i: (i, 0, 0)),
                   pl.BlockSpec((ng, 2, 2 * STATE), lambda i: (i, 0, 0))] + r_out,
        out_shape=[jax.ShapeDtypeStruct((g, N_BLK, BLK, BLK), BF16),
                   jax.ShapeDtypeStruct((g, kdim, 2 * STATE), BF16),
                   jax.ShapeDtypeStruct((g, kdim, 2 * STATE), BF16),
                   jax.ShapeDtypeStruct((g, 2, 2 * STATE), F32)] + r_shape,
        scratch_shapes=[pltpu.VMEM((ng, BLK, 2 * STATE), F32), pltpu.VMEM((ng, BLK, 2 * STATE), F32),
                        pltpu.VMEM((ng, kdim, 2 * STATE), F32)],
        compiler_params=_params("arbitrary"),
        name="ssm_prep",
    )(dup(a_re), dup(a_im), f(log_dt).reshape(g, 1, 1), bt, bts, cc, ccs, d_pad, ride)


def _rms(x, g):
    return x * lax.rsqrt(jnp.mean(x * x, axis=-1, keepdims=True) + EPS) * g


def _in_proj_kernel(x_ref, n1_ref, w_ref, qg_ref, kg_ref, rc_ref, rm_ref, rp_ref, *rest):
    n_ride = (len(rest) - 5) // 2
    u_ref, q_ref, k_ref, v_ref, xn_ref = rest[n_ride:n_ride + 5]
    _run_riders(rest[:n_ride], rest[n_ride + 5:])
    hr = x_ref.shape[0] // 2
    lo = lax.broadcasted_iota(jnp.int32, (hr, LANES), 1) < HEAD_DIM

    for r in range(2):
        rows = slice(r * hr, (r + 1) * hr)
        xn = _rms(x_ref[rows, :], n1_ref[...]).astype(BF16)
        xn_ref[rows, :] = xn
        rc = rc_ref[rows, :]
        rm = rm_ref[rows, :]
        rp = rp_ref[rows, :]

        def norm_rope(z, gain, width, scale):
            outs = []
            for cb in range(width // LANES):
                zb = z[:, cb * LANES:(cb + 1) * LANES]
                sq = zb * zb
                s_lo = jnp.sum(jnp.where(lo, sq, 0.0), axis=-1, keepdims=True)
                s_hi = jnp.sum(jnp.where(lo, 0.0, sq), axis=-1, keepdims=True)
                ms = jnp.where(lo, s_lo, s_hi) * (1.0 / HEAD_DIM)
                zn = zb * lax.rsqrt(ms + EPS) * gain
                rot = zn * rc + pltpu.roll(zn, LANES - ROT_DIM // 2, axis=1) * rm \
                    + pltpu.roll(zn, ROT_DIM // 2, axis=1) * rp
                outs.append(rot * scale)
            return outs

        q = _dot(xn, w_ref[:, D_SSM:D_SSM + D_ATTN])
        k = _dot(xn, w_ref[:, D_SSM + D_ATTN:D_SSM + D_ATTN + D_KV])
        for cb, blk in enumerate(norm_rope(q, qg_ref[...], D_ATTN, HEAD_DIM ** -0.5 * LOG2E)):
            q_ref[rows, cb * LANES:(cb + 1) * LANES] = blk.astype(q_ref.dtype)
        for cb, blk in enumerate(norm_rope(k, kg_ref[...], D_KV, 1.0)):
            k_ref[rows, cb * LANES:(cb + 1) * LANES] = blk
        u = _dot(xn, w_ref[:, 0:D_SSM])
        for c in range(hr // CHUNK):
            base = (r * (hr // CHUNK) + c) * PITCH
            u_ref[base:base + CHUNK, :] = u[c * CHUNK:(c + 1) * CHUNK]
            u_ref[base + CHUNK:base + PITCH, :] = jnp.zeros((PITCH - CHUNK, D_SSM), F32)
        v_ref[rows, :] = _dot(xn, w_ref[:, D_SSM + D_ATTN + D_KV:])


def _rope_tables(pos):
    half = ROT_DIM // 2
    inv = ROPE_THETA ** (-np.arange(half, dtype=np.float64) * 2.0 / ROT_DIM)
    ang = pos.astype(np.float64)[:, None] * inv[None, :]
    cos, sin = np.cos(ang), np.sin(ang)
    n = pos.shape[0]
    ones = np.ones((n, HEAD_DIM - ROT_DIM))
    zeros = np.zeros((n, HEAD_DIM - ROT_DIM))
    zh = np.zeros((n, half))
    rc = np.concatenate([cos, cos, ones], axis=1)
    rm = np.concatenate([-sin, zh, zeros], axis=1)
    rp = np.concatenate([zh, sin, zeros], axis=1)
    two = lambda t: jnp.asarray(np.concatenate([t, t], axis=1), F32)
    return two(rc), two(rm), two(rp)


def _in_proj(x2d, pos, tm, norm1, w_in_b, qg, kg, ride=()):
    t = x2d.shape[0]
    r_in, r_out, r_shape = _cast_riders(ride, t // tm, lambda i: i)
    rc, rm, rp = _rope_tables(pos)
    nrope = pos.shape[0] // tm
    rope_spec = pl.BlockSpec((tm, LANES), lambda i: (i % nrope, 0))
    const = lambda shape: pl.BlockSpec(shape, lambda i: (0, 0))
    tok = lambda w: pl.BlockSpec((tm, w), lambda i: (i, 0))
    return pl.pallas_call(
        _in_proj_kernel,
        grid=(t // tm,),
        in_specs=[tok(D_MODEL), const((1, D_MODEL)), const(w_in_b.shape),
                  const((1, LANES)), const((1, LANES)), rope_spec, rope_spec, rope_spec] + r_in,
        out_specs=[pl.BlockSpec((tm // CHUNK * PITCH, D_SSM), lambda i: (i, 0)), tok(D_ATTN), tok(D_KV),
                   tok(D_KV), tok(D_MODEL)] + r_out,
        out_shape=[jax.ShapeDtypeStruct((t // CHUNK * PITCH, D_SSM), F32), jax.ShapeDtypeStruct((t, D_ATTN), BF16),
                   jax.ShapeDtypeStruct((t, D_KV), F32), jax.ShapeDtypeStruct((t, D_KV), F32),
                   jax.ShapeDtypeStruct((t, D_MODEL), BF16)] + r_shape,
        compiler_params=_params("arbitrary"),
        name="in_proj",
    )(x2d, norm1, w_in_b, qg, kg, rc, rm, rp, *[r[0] if isinstance(r, tuple) else r for r in ride])


def _ssm_state_kernel(up_ref, us_ref, wst_ref, xtp_ref, xts_ref, sp_ref, ss_ref, *, mp, bs):
    ng = xtp_ref.shape[0]
    pad = jnp.zeros((LANES - bs, LANES), F32)
    for s in range(CHUNK):
        vp = up_ref[pl.ds(s, mp, stride=PITCH), :].T.astype(BF16)
        vs = jnp.concatenate([us_ref[pl.ds(s, bs, stride=PITCH), :], pad], axis=0).T.astype(BF16)
        for g in range(ng):
            xtp_ref[g, s * GROUP_CH:(s + 1) * GROUP_CH, :] = vp[g * GROUP_CH:(g + 1) * GROUP_CH, :]
            xts_ref[g, s * GROUP_CH:(s + 1) * GROUP_CH, :] = vs[g * GROUP_CH:(g + 1) * GROUP_CH, :]
    tn = (((0,), (0,)), ((), ()))
    for g in range(ng):
        st = lax.dot_general(wst_ref[g], xtp_ref[g], tn, preferred_element_type=F32)
        sp_ref[g * mp:(g + 1) * mp, :] = st.T
        st = lax.dot_general(wst_ref[g], xts_ref[g], tn, preferred_element_type=F32)
        ss_ref[g * bs:(g + 1) * bs, :] = st.T[:bs]


def _cmul_add(h, ar2, ai2, s):
    return h * ar2 + pltpu.roll(h, STATE, axis=1) * ai2 + s


def _ssm_out_kernel(xtp_ref, xts_ref, blk_ref, wout_ref, sp_ref, ss_ref, arp_ref, aip_ref, ars_ref, ais_ref,
                    h0s_ref, yp_ref, ys_ref, hfp_ref, hfs_ref, hin_s, stp_s, sts_s, *, mp, bs, nchunk):
    ng = xtp_ref.shape[0]
    rows = mp // nchunk * ng
    ar = arp_ref[...]
    ai = aip_ref[...]
    ai_sw = pltpu.roll(ai, STATE, axis=1)
    s_all = [sp_ref[pl.ds(k, rows, stride=nchunk), :] for k in range(nchunk)]
    s_sw = [pltpu.roll(s, STATE, axis=1) for s in s_all]
    h = jnp.zeros((rows, 2 * STATE), F32)
    h_sw = h
    for k in range(nchunk):
        hin_s[pl.ds(k, rows, stride=nchunk), :] = h
        h, h_sw = h * ar + h_sw * ai + s_all[k], h_sw * ar + h * ai_sw + s_sw[k]
    hfp_ref[...] = h
    hfs_ref[...] = _cmul_add(h0s_ref[...], ars_ref[...], ais_ref[...], ss_ref[...])

    pad = jnp.zeros((LANES - bs, 2 * STATE), F32)
    hp = [hin_s[g * mp:(g + 1) * mp, :].astype(BF16) for g in range(ng)]
    hs = [jnp.concatenate([h0s_ref[g * bs:(g + 1) * bs, :], pad], axis=0).astype(BF16) for g in range(ng)]
    for tb in range(N_BLK):
        for g in range(ng):
            accp = accs = None
            for sb in range(tb + 1):
                b = blk_ref[g, tb - sb]
                dp = _dot(b, xtp_ref[g, sb * BLK:(sb + 1) * BLK, :])
                ds = _dot(b, xts_ref[g, sb * BLK:(sb + 1) * BLK, :])
                accp = dp if accp is None else accp + dp
                accs = ds if accs is None else accs + ds
            w = wout_ref[g, tb * BLK:(tb + 1) * BLK, :]
            accp = accp + _dot_nt(w, hp[g])
            accs = accs + _dot_nt(w, hs[g])
            for t0 in range(STEPS_PER_BLK):
                stp_s[t0, g * GROUP_CH:(g + 1) * GROUP_CH, :] = accp[t0 * GROUP_CH:(t0 + 1) * GROUP_CH, :]
                sts_s[t0, g * GROUP_CH:(g + 1) * GROUP_CH, :] = accs[t0 * GROUP_CH:(t0 + 1) * GROUP_CH, :]
        for t0 in range(STEPS_PER_BLK):
            t = tb * STEPS_PER_BLK + t0
            yp_ref[pl.ds(t, mp, stride=PITCH), :] = jax.nn.gelu(stp_s[t0].T)
            ys_ref[pl.ds(t, bs, stride=PITCH), :] = jax.nn.gelu(sts_s[t0].T[:bs])
    for r in range(CHUNK, PITCH):
        yp_ref[pl.ds(r, mp, stride=PITCH), :] = jnp.zeros((mp, LANES), F32)
        ys_ref[pl.ds(r, bs, stride=PITCH), :] = jnp.zeros((bs, LANES), F32)


def _ssm(up, us, blk, wst, wout, a2, h0s, bp, nchunk, bs):
    g = N_GROUPS
    ng = GROUPS_PER_BLOCK
    mp = bp * nchunk
    kdim = CHUNK * GROUP_CH
    tok = lambda a: pl.BlockSpec((a.shape[0], LANES), lambda i: (0, i))
    grp = lambda *tail: pl.BlockSpec((ng,) + tail, lambda i: (i,) + (0,) * len(tail))
    rows = lambda r: pl.BlockSpec((ng * r, 2 * STATE), lambda i: (i, 0))
    xtp, xts, sp, ss = pl.pallas_call(
        functools.partial(_ssm_state_kernel, mp=mp, bs=bs),
        grid=(g // ng,),
        in_specs=[tok(up), tok(us), grp(kdim, 2 * STATE)],
        out_specs=[grp(kdim, mp), grp(kdim, LANES), rows(mp), rows(bs)],
        out_shape=[jax.ShapeDtypeStruct((g, kdim, mp), BF16), jax.ShapeDtypeStruct((g, kdim, LANES), BF16),
                   jax.ShapeDtypeStruct((g * mp, 2 * STATE), F32),
                   jax.ShapeDtypeStruct((g * bs, 2 * STATE), F32)],
        compiler_params=_params("arbitrary"),
        name="ssm_state",
    )(up, us, wst)

    arp = jnp.repeat(a2[:, 0, :], bp, axis=0)
    aip = jnp.repeat(a2[:, 1, :], bp, axis=0)
    ars = jnp.repeat(a2[:, 0, :], bs, axis=0)
    ais = jnp.repeat(a2[:, 1, :], bs, axis=0)
    yp, ys, hfp, hfs = pl.pallas_call(
        functools.partial(_ssm_out_kernel, mp=mp, bs=bs, nchunk=nchunk),
        grid=(g // ng,),
        in_specs=[grp(kdim, mp), grp(kdim, LANES), grp(N_BLK, BLK, BLK), grp(kdim, 2 * STATE), rows(mp), rows(bs),
                  rows(bp), rows(bp), rows(bs), rows(bs), rows(bs)],
        out_specs=[tok(up), tok(us), rows(bp), rows(bs)],
        out_shape=[jax.ShapeDtypeStruct(up.shape, F32), jax.ShapeDtypeStruct(us.shape, F32),
                   jax.ShapeDtypeStruct((g * bp, 2 * STATE), F32), jax.ShapeDtypeStruct((g * bs, 2 * STATE), F32)],
        scratch_shapes=[pltpu.VMEM((ng * mp, 2 * STATE), F32), pltpu.VMEM((STEPS_PER_BLK, LANES, mp), F32),
                        pltpu.VMEM((STEPS_PER_BLK, LANES, LANES), F32)],
        compiler_params=_params("arbitrary"),
        name="ssm_out",
    )(xtp, xts, blk, wout, sp, ss, arp, aip, ars, ais, h0s)
    return yp, ys, hfp, hfs


def _attn_kernel(sinks_ref, bias_ref, q_ref, kp_ref, kc_ref, vp_ref, vc_ref, *rest, qp, kw, masked):
    n_ride = (len(rest) - 1) // 2
    o_ref = rest[n_ride]
    _run_riders(rest[:n_ride], rest[n_ride + 1:])
    for e in range(q_ref.shape[0]):
        _attn_tile(sinks_ref, bias_ref, q_ref.at[e], kp_ref.at[e], kc_ref.at[e], vp_ref.at[e], vc_ref.at[e],
                   o_ref.at[e], qp=qp, kw=kw, masked=masked)


def _attn_tile(sinks_ref, bias_ref, q_ref, kp_ref, kc_ref, vp_ref, vc_ref, o_ref, *, qp, kw, masked):
    i = pl.program_id(1)
    n_units = q_ref.shape[0] // qp
    gw = Q_PER_KV * HEAD_DIM
    half = HEAD_DIM
    nl = Q_PER_KV * qp

    def duplicated(prev_ref, cur_ref):
        a = jnp.concatenate([prev_ref[...], cur_ref[...]], axis=0)
        lo = lax.broadcasted_iota(jnp.int32, (a.shape[0], 2 * half), 1) < half
        out = []
        for c in range(N_KV // 2):
            col = a[:, c * 2 * half:(c + 1) * 2 * half]
            rot = pltpu.roll(col, half, axis=1)
            out.append(jnp.where(lo, col, rot).astype(BF16))
            out.append(jnp.where(lo, rot, col).astype(BF16))
        return out

    kdup = duplicated(kp_ref, kc_ref)
    vdup = duplicated(vp_ref, vc_ref)
    pair_lo = lax.broadcasted_iota(jnp.int32, (qp, 2 * half), 1) < half
    lane_g = lax.broadcasted_iota(jnp.int32, (1, nl), 1) // qp

    for u in range(n_units):
        rows = slice(u * qp, (u + 1) * qp)
        win = slice(u * qp, u * qp + kw)
        if masked:
            bias = bias_ref[jnp.where(i == 0, 0, 1)] if u == 0 else bias_ref[1]
        scores = []
        for kv in range(N_KV):
            qm = []
            for g in range(Q_PER_KV):
                pair = q_ref[rows, kv * gw + (g // 2) * 2 * half:kv * gw + (g // 2 + 1) * 2 * half]
                keep = pair_lo if g % 2 == 0 else jnp.logical_not(pair_lo)
                qm.append(jnp.where(keep, pair, jnp.zeros_like(pair)))
            scores.append(_dot_nt(kdup[kv][win], jnp.concatenate(qm, axis=0)))
        probs = []
        for kv in range(N_KV):
            s = scores[kv] + bias if masked else scores[kv]
            sink = jnp.zeros((1, nl), F32)
            for g in range(Q_PER_KV):
                sink = jnp.where(lane_g == g, sinks_ref[kv * Q_PER_KV + g], sink)
            m = jnp.maximum(jnp.max(s, axis=0, keepdims=True), sink)
            p = jnp.exp2(s - m)
            den = jnp.sum(p, axis=0, keepdims=True) + jnp.exp2(sink - m)
            probs.append((p * (1.0 / den)).astype(BF16))
        for kv in range(N_KV):
            o = lax.dot_general(probs[kv], vdup[kv][win], (((0,), (0,)), ((), ())),
                                preferred_element_type=F32)
            for h in range(Q_PER_KV // 2):
                both = jnp.where(pair_lo, o[2 * h * qp:(2 * h + 1) * qp], o[(2 * h + 1) * qp:(2 * h + 2) * qp])
                o_ref[rows, kv * gw + h * 2 * half:kv * gw + (h + 1) * 2 * half] = both.astype(o_ref.dtype)


def _attention(sinks, q, k_prev, k_cur, v_prev, v_cur, qt, qp, masked, bb=1, ride=()):
    b, l, _ = q.shape
    nt = l // qt
    r_in, r_out, r_shape = _cast_riders(ride, b // bb * nt, lambda bi, i: bi * nt + i)
    per = qt // WINDOW
    kw = WINDOW + qp
    kc = np.arange(kw)[:, None] // CHUNK
    qc = (np.arange(Q_PER_KV * qp)[None, :] % qp) // CHUNK
    band = (kc >= qc) & (kc <= qc + WINDOW // CHUNK)
    bias = np.stack([np.where(band & (kc >= WINDOW // CHUNK), 0.0, NEG), np.where(band, 0.0, NEG)])
    bias = jnp.asarray(bias, F32)
    if masked:
        prev_map = lambda bi, i: (bi, jnp.maximum(i * per - 1, 0), 0)
    else:
        prev_map = lambda bi, i: (bi, 0, 0)
    cur = lambda w: pl.BlockSpec((bb, qt, w), lambda bi, i: (bi, i, 0))
    prev = pl.BlockSpec((bb, WINDOW, D_KV), prev_map)
    return pl.pallas_call(
        functools.partial(_attn_kernel, qp=qp, kw=kw, masked=masked),
        grid=(b // bb, nt),
        in_specs=[pl.BlockSpec(memory_space=pltpu.SMEM), pl.BlockSpec(bias.shape, lambda bi, i: (0, 0, 0)),
                  cur(D_ATTN), prev, cur(D_KV), prev, cur(D_KV)] + r_in,
        out_specs=[cur(D_ATTN)] + r_out,
        out_shape=[jax.ShapeDtypeStruct((b, l, D_ATTN), BF16)] + r_shape,
        compiler_params=_params("arbitrary", "arbitrary"),
        name="attention",
    )(sinks, bias, q, k_prev, k_cur, v_prev, v_cur, *[r[0] if isinstance(r, tuple) else r for r in ride])


def _mix_kernel(x_ref, xn_ref, ya_ref, o_ref, wglu_ref, wga_ref, wgb_ref, wbs_ref, wba_ref, wout_ref,
                h_ref, ys_s):
    j = pl.program_id(1)

    @pl.when(j == 0)
    def _():
        hr = x_ref.shape[0] // 2
        for r in range(2):
            rows = slice(r * hr, (r + 1) * hr)
            c0 = r * (hr // CHUNK)
            ya = jnp.concatenate([ya_ref[c * PITCH:c * PITCH + CHUNK, :] for c in range(c0, c0 + hr // CHUNK)],
                                 axis=0)
            ys_s[rows, :] = (ya * jax.nn.sigmoid(_dot(ya.astype(BF16), wglu_ref[...]))).astype(BF16)
            h_ref[rows, :] = x_ref[rows, :]

    xn = xn_ref[...]
    ga = jax.nn.sigmoid(_dot(xn, wga_ref[...]))
    gb = jax.nn.sigmoid(_dot(xn, wgb_ref[...]))
    mixed = ga * _dot(ys_s[...], wbs_ref[...]) + gb * _dot(o_ref[...], wba_ref[...])
    h_ref[...] += _dot(mixed.astype(BF16), wout_ref[...])


def _mix(x2d, xn2d, ya2d, o2d, wglu, wgate, wbs, wba, wout, tm):
    t = x2d.shape[0]
    tn = wbs.shape[2]
    nj = D_MODEL // tn
    tok = lambda w: pl.BlockSpec((tm, w), lambda i, j: (i, 0))
    colblk = lambda rows, off: pl.BlockSpec((None, rows, tn), lambda i, j: (off + j, 0, 0))
    return pl.pallas_call(
        _mix_kernel,
        grid=(t // tm, nj),
        in_specs=[tok(D_MODEL), tok(D_MODEL), pl.BlockSpec((tm // CHUNK * PITCH, D_SSM), lambda i, j: (i, 0)),
                  tok(D_ATTN), pl.BlockSpec((D_SSM, D_SSM), lambda i, j: (0, 0)),
                  colblk(D_MODEL, 0), colblk(D_MODEL, nj), colblk(D_SSM, 0), colblk(D_ATTN, 0),
                  pl.BlockSpec((tn, D_MODEL), lambda i, j: (j, 0))],
        out_specs=tok(D_MODEL),
        out_shape=jax.ShapeDtypeStruct((t, D_MODEL), F32),
        scratch_shapes=[pltpu.VMEM((tm, D_SSM), BF16)],
        compiler_params=_params("arbitrary", "arbitrary"),
        name="mix",
    )(x2d, xn2d, ya2d, o2d, wglu, wgate, wgate, wbs, wba, wout)


def _ffn_kernel(h_ref, n2_ref, wg_ref, wu_ref, wd_ref, out_ref, hn_s):
    j = pl.program_id(1)

    @pl.when(j == 0)
    def _():
        h = h_ref[...]
        hn_s[...] = _rms(h, n2_ref[...]).astype(BF16)
        out_ref[...] = h

    hn = hn_s[...]
    act = jax.nn.silu(_dot(hn, wg_ref[...])) * _dot(hn, wu_ref[...])
    out_ref[...] += _dot(act.astype(BF16), wd_ref[...])


def _ffn(h2d, norm2, wg, wu, wd, tm):
    t = h2d.shape[0]
    nf, _, tf = wg.shape
    d_ff = nf * tf
    tok = pl.BlockSpec((tm, D_MODEL), lambda i, j: (i, 0))
    return pl.pallas_call(
        _ffn_kernel,
        grid=(t // tm, d_ff // tf),
        in_specs=[tok, pl.BlockSpec((1, D_MODEL), lambda i, j: (0, 0)),
                  pl.BlockSpec((None, D_MODEL, tf), lambda i, j: (j, 0, 0)),
                  pl.BlockSpec((None, D_MODEL, tf), lambda i, j: (j, 0, 0)),
                  pl.BlockSpec((tf, D_MODEL), lambda i, j: (j, 0))],
        out_specs=tok,
        out_shape=jax.ShapeDtypeStruct((t, D_MODEL), F32),
        scratch_shapes=[pltpu.VMEM((tm, D_MODEL), BF16)],
        compiler_params=_params("arbitrary", "arbitrary"),
        name="ffn",
    )(h2d, norm2, wg, wu, wd)


def kernel(x_prompt, x_sample, cache_k, cache_v, state_ssm_re, state_ssm_im, norm1, w_in, q_norm, k_norm,
           sinks, ssm_a_re, ssm_a_im, ssm_log_dt, ssm_b_re, ssm_b_im, ssm_c_re, ssm_c_im, ssm_d, w_glu,
           w_br_ssm, w_br_attn, w_gate, w_out, norm2, w_ffn_gate, w_ffn_up, w_ffn_down):
    bp, lp, _ = x_prompt.shape
    bs, ls, _ = x_sample.shape
    ncp = lp // CHUNK
    assert ls == CHUNK and cache_k.shape[2] == WINDOW and norm1.shape[0] == 1
    l = 0

    f32 = lambda a: a[l].astype(F32)
    n1 = norm1[l].astype(F32).reshape(1, D_MODEL)
    n2 = norm2[l].astype(F32).reshape(1, D_MODEL)
    qg = jnp.tile(q_norm[l].astype(F32), LANES // HEAD_DIM).reshape(1, LANES)
    kg = jnp.tile(k_norm[l].astype(F32), LANES // HEAD_DIM).reshape(1, LANES)
    sink = sinks[l].astype(F32) * LOG2E

    blk, wst, wout_t, a2, w_in_b = _ssm_prep(ssm_a_re[l], ssm_a_im[l], ssm_log_dt[l], ssm_b_re[l], ssm_b_im[l],
                                             ssm_c_re[l], ssm_c_im[l], ssm_d[l], f32(w_in))

    xp = x_prompt.astype(F32).reshape(bp * lp, D_MODEL)
    xs = x_sample.astype(F32).reshape(bs * ls, D_MODEL)
    tm = TM_IN
    up, qp, kp, vp, xnp, wfg_b, wfu_b, wfd_b = _in_proj(
        xp, np.arange(lp), tm, n1, w_in_b, qg, kg,
        ride=((f32(w_ffn_gate), TF_FFN), (f32(w_ffn_up), TF_FFN), f32(w_ffn_down)))
    us, qs, ks, vs, xns = _in_proj(xs, np.tile(PAST_LEN + np.arange(ls), tm // ls), tm, n1, w_in_b, qg, kg)

    h0s = jnp.concatenate([state_ssm_re[l], state_ssm_im[l]], axis=-1).astype(F32)
    h0s = jnp.swapaxes(h0s, 0, 1).reshape(N_GROUPS * bs, 2 * STATE)
    y_p, y_s, hfp, hfs = _ssm(up, us, blk, wst, wout_t, a2, h0s, bp, ncp, bs)

    def states(hf, b):
        hf = jnp.swapaxes(hf.reshape(N_GROUPS, b, 2 * STATE), 0, 1)
        return hf[None, :, :, :STATE], hf[None, :, :, STATE:]

    kp3, vp3 = kp.reshape(bp, lp, D_KV), vp.reshape(bp, lp, D_KV)
    o_p, wgate_b, wout_b, wbs_b, wba_b, wglu_b = _attention(
        sink, qp.reshape(bp, lp, D_ATTN), kp3, kp3, vp3, vp3, 2048, 2 * CHUNK, True,
        ride=((f32(w_gate), TN_MIX), f32(w_out), (f32(w_br_ssm), TN_MIX), (f32(w_br_attn), TN_MIX), f32(w_glu)))
    ck = cache_k[l].astype(F32).reshape(bs, WINDOW, D_KV)
    cv = cache_v[l].astype(F32).reshape(bs, WINDOW, D_KV)
    ks3, vs3 = ks.reshape(bs, ls, D_KV), vs.reshape(bs, ls, D_KV)
    o_s, = _attention(sink, qs.reshape(bs, ls, D_ATTN), ck, ks3, cv, vs3, ls, ls, False, bb=8)

    outs = []
    for x2d, xn2d, ya2d, o3d in ((xp, xnp, y_p, o_p), (xs, xns, y_s, o_s)):
        h = _mix(x2d, xn2d, ya2d, o3d.reshape(-1, D_ATTN), wglu_b, wgate_b, wbs_b, wba_b, wout_b, TM_MIX)
        outs.append(_ffn(h, n2, wfg_b, wfu_b, wfd_b, TM_FFN))

    win = lambda a, b, n: a.reshape(b, n, N_KV, HEAD_DIM)[None]
    k_win_p = win(kp3[:, lp - WINDOW:], bp, WINDOW)
    v_win_p = win(vp3[:, lp - WINDOW:], bp, WINDOW)
    k_win_s = win(jnp.concatenate([ck, ks3], axis=1)[:, ls:], bs, WINDOW)
    v_win_s = win(jnp.concatenate([cv, vs3], axis=1)[:, ls:], bs, WINDOW)
    re_p, im_p = states(hfp, bp)
    re_s, im_s = states(hfs, bs)
    dt = x_prompt.dtype
    return (outs[0].reshape(bp, lp, D_MODEL).astype(dt), outs[1].reshape(bs, ls, D_MODEL).astype(dt),
            k_win_p.astype(dt), v_win_p.astype(dt), re_p.astype(dt), im_p.astype(dt),
            k_win_s.astype(dt), v_win_s.astype(dt), re_s.astype(dt), im_s.astype(dt))
```

```python
import functools

import jax
import jax.numpy as jnp
import numpy as np
from jax import lax
from jax.experimental import pallas as pl
from jax.experimental.pallas import tpu as pltpu

D_MODEL = 2048
CHUNK = 64
D_SSM = 1024
GROUP_CH = 16
N_GROUPS = 64
STATE = 64
HEAD_DIM = 64
N_HEADS = 16
N_KV = 4
Q_PER_KV = 4
D_ATTN = 1024
D_KV = 256
WINDOW = 128
ROT_DIM = 16
ROPE_THETA = 500000.0
PAST_LEN = 1024
EPS = 1e-6
NEG = -1e30
LOG2E = 1.4426950408889634

LANES = 128
BLK = 256
GROUPS_PER_BLOCK = LANES // GROUP_CH
PITCH = CHUNK + 8
STEPS_PER_BLK = BLK // GROUP_CH
N_BLK = CHUNK // STEPS_PER_BLK
V7X_VMEM_LIMIT = 56 * 1024 * 1024
TM_IN, TM_MIX, TN_MIX, TM_FFN, TF_FFN = 512, 512, 512, 1024, 512

F32 = jnp.float32
BF16 = jnp.bfloat16


def _dot(a, b):
    return jnp.dot(a, b, preferred_element_type=F32)


def _dot_nt(a, b):
    return lax.dot_general(a, b, (((1,), (1,)), ((), ())), preferred_element_type=F32)


def _params(*sem):
    return pltpu.CompilerParams(dimension_semantics=sem, vmem_limit_bytes=V7X_VMEM_LIMIT)


def _cast_riders(weights, nsteps, step_of):
    in_specs, out_specs, out_shape = [], [], []
    for item in weights:
        w, cb = item if isinstance(item, tuple) else (item, None)
        rows, cols = w.shape
        slab = rows // nsteps
        assert slab * nsteps == rows and slab % 16 == 0, (w.shape, nsteps)
        in_specs.append(pl.BlockSpec((slab, cols), lambda *ids: (step_of(*ids), 0)))
        if cb is None:
            out_specs.append(pl.BlockSpec((slab, cols), lambda *ids: (step_of(*ids), 0)))
            out_shape.append(jax.ShapeDtypeStruct(w.shape, BF16))
        else:
            out_specs.append(pl.BlockSpec((cols // cb, slab, cb), lambda *ids: (0, step_of(*ids), 0)))
            out_shape.append(jax.ShapeDtypeStruct((cols // cb, rows, cb), BF16))
    return in_specs, out_specs, out_shape


def _run_riders(src_refs, dst_refs):
    for s, d in zip(src_refs, dst_refs):
        if len(d.shape) == 2:
            d[...] = s[...].astype(d.dtype)
        else:
            cb = d.shape[2]
            for c in range(d.shape[0]):
                d[c] = s[:, c * cb:(c + 1) * cb].astype(d.dtype)


def _ssm_prep_kernel(*refs):
    ins, wf_ref, outs, wb_ref, scratch = refs[:8], refs[8], refs[9:13], refs[13], refs[14:]
    _run_riders([wf_ref], [wb_ref])
    for g in range(ins[0].shape[0]):
        _ssm_prep_group(*[r.at[g] for r in ins + outs + scratch])


def _ssm_prep_group(are_ref, aim_ref, ldt_ref, bt_ref, bts_ref, c_ref, cs_ref, d_ref,
                    blk_ref, wst_ref, wout_ref, a2_ref, l_s, z_s, w_s):
    lane = lax.broadcasted_iota(jnp.int32, (1, 2 * STATE), 1)
    sg = jnp.where(lane < STATE, 1.0, -1.0).astype(F32)
    a_re = are_ref[...]
    a_im = aim_ref[...]
    dt = jnp.exp(ldt_ref[...])
    lre = a_re * dt
    lim = a_im * dt

    def cpow(e):
        mag = jnp.exp(e * lre)
        ang = e * lim
        return mag * jnp.cos(ang), mag * jnp.sin(ang)

    def cmul(ar, ai, br, bi):
        return ar * br - ai * bi, ar * bi + ai * br

    col = lax.broadcasted_iota(jnp.int32, (STEPS_PER_BLK, 1), 0).astype(F32)
    t_r, t_i = cpow(col)
    r_r, r_i = cpow(float(STEPS_PER_BLK - 1) - col)
    h_r, h_i = cpow(float(STEPS_PER_BLK) * col[:8])
    ab_r, ab_i = t_r[1:2], t_i[1:2]

    fr, fi = ab_r - 1.0, ab_i
    den = a_re * a_re + a_im * a_im
    cr = (fr * a_re + fi * a_im) / den
    ci = (fi * a_re - fr * a_im) / den
    bb = cr * bt_ref[...] - (ci * sg) * bts_ref[...]
    bbs = cr * bts_ref[...] + (ci * sg) * bt_ref[...]
    cc = c_ref[...]
    ccs = cs_ref[...]

    def outer_rows(pr, pi, m, ms, dst, row0, conj):
        pa = pr * sg if conj else pr
        pb = -pi if conj else -(pi * sg)
        for r in range(pr.shape[0]):
            z = pa[r:r + 1, :] * m + pb[r:r + 1, :] * ms
            dst[row0 + r * GROUP_CH:row0 + (r + 1) * GROUP_CH, :] = z

    e_r, e_i = cmul(ab_r, ab_i, r_r, r_i)
    outer_rows(e_r, e_i, bb, bbs, l_s, 0, False)
    l2 = l_s[...].astype(BF16)

    for d in range(N_BLK - 1):
        e_r, e_i = cmul(h_r[d:d + 1], h_i[d:d + 1], t_r, t_i)
        outer_rows(e_r, e_i, cc, ccs, z_s, 0, True)
        blk_ref[d + 1] = _dot_nt(z_s[...].astype(BF16), l2).astype(BF16)

    outer_rows(r_r, r_i, bb, bbs, z_s, 0, False)
    kr = _dot_nt((cc * sg).astype(BF16), z_s[...].astype(BF16))
    rows = lax.broadcasted_iota(jnp.int32, (GROUP_CH, BLK), 0)
    lanes = lax.broadcasted_iota(jnp.int32, (GROUP_CH, BLK), 1)
    kr = kr + jnp.where(lanes == rows + (BLK - GROUP_CH), d_ref[...], 0.0)
    for t0 in range(STEPS_PER_BLK):
        sh = GROUP_CH * (STEPS_PER_BLK - 1 - t0)
        piece = kr if sh == 0 else jnp.where(lanes < BLK - sh, pltpu.roll(kr, BLK - sh, axis=1), 0.0)
        blk_ref[0, t0 * GROUP_CH:(t0 + 1) * GROUP_CH, :] = piece.astype(BF16)

    for sb in range(N_BLK):
        m = N_BLK - 1 - sb
        e_r, e_i = cmul(h_r[m:m + 1], h_i[m:m + 1], r_r, r_i)
        outer_rows(e_r, e_i, bb, bbs, w_s, sb * BLK, False)
    wst_ref[...] = w_s[...].astype(BF16)

    for tb in range(N_BLK):
        e_r, e_i = cmul(*cmul(ab_r, ab_i, h_r[tb:tb + 1], h_i[tb:tb + 1]), t_r, t_i)
        outer_rows(e_r, e_i, cc, ccs, w_s, tb * BLK, True)
    wout_ref[...] = w_s[...].astype(BF16)

    a2_ref[0:1, :] = h_r[N_BLK:N_BLK + 1]
    a2_ref[1:2, :] = -(h_i[N_BLK:N_BLK + 1] * sg)


def _ssm_prep(a_re, a_im, log_dt, b_re, b_im, c_re, c_im, d_skip, ride):
    g = N_GROUPS
    ng = GROUPS_PER_BLOCK
    r_in, r_out, r_shape = _cast_riders([ride], g // ng, lambda i: i)
    f = lambda a: a.astype(F32)
    dup = lambda a: jnp.concatenate([f(a), f(a)], axis=-1).reshape(g, 1, 2 * STATE)
    bt_re = jnp.swapaxes(f(b_re), 1, 2)
    bt_im = jnp.swapaxes(f(b_im), 1, 2)
    bt = jnp.concatenate([bt_re, bt_im], axis=-1)
    bts = jnp.concatenate([bt_im, bt_re], axis=-1)
    cc = jnp.concatenate([f(c_re), f(c_im)], axis=-1)
    ccs = jnp.concatenate([f(c_im), f(c_re)], axis=-1)
    d_pad = jnp.pad(f(d_skip).reshape(g, 1, GROUP_CH), ((0, 0), (0, 0), (BLK - GROUP_CH, 0)))
    vec = lambda n: pl.BlockSpec((ng, 1, n), lambda i: (i, 0, 0))
    mat = pl.BlockSpec((ng, GROUP_CH, 2 * STATE), lambda i: (i, 0, 0))
    kdim = CHUNK * GROUP_CH
    return pl.pallas_call(
        _ssm_prep_kernel,
        grid=(g // ng,),
        in_specs=[vec(2 * STATE), vec(2 * STATE), vec(1), mat, mat, mat, mat, vec(BLK)] + r_in,
        out_specs=[pl.BlockSpec((ng, N_BLK, BLK, BLK), lambda i: (i, 0, 0, 0)),
                   pl.BlockSpec((ng, kdim, 2 * STATE), lambda i: (i, 0, 0)),
                   pl.BlockSpec((ng, kdim, 2 * STATE), lambda i: (i, 0, 0)),
                   pl.BlockSpec((ng, 2, 2 * STATE), lambda i: (i, 0, 0))] + r_out,
        out_shape=[jax.ShapeDtypeStruct((g, N_BLK, BLK, BLK), BF16),
                   jax.ShapeDtypeStruct((g, kdim, 2 * STATE), BF16),
                   jax.ShapeDtypeStruct((g, kdim, 2 * STATE), BF16),
                   jax.ShapeDtypeStruct((g, 2, 2 * STATE), F32)] + r_shape,
        scratch_shapes=[pltpu.VMEM((ng, BLK, 2 * STATE), F32), pltpu.VMEM((ng, BLK, 2 * STATE), F32),
                        pltpu.VMEM((ng, kdim, 2 * STATE), F32)],
        compiler_params=_params("arbitrary"),
        name="ssm_prep",
    )(dup(a_re), dup(a_im), f(log_dt).reshape(g, 1, 1), bt, bts, cc, ccs, d_pad, ride)


def _rms(x, g):
    return x * lax.rsqrt(jnp.mean(x * x, axis=-1, keepdims=True) + EPS) * g


def _in_proj_kernel(x_ref, n1_ref, w_ref, qg_ref, kg_ref, rc_ref, rm_ref, rp_ref, *rest):
    n_ride = (len(rest) - 4) // 2
    u_ref, q_ref, k_ref, v_ref = rest[n_ride:n_ride + 4]
    _run_riders(rest[:n_ride], rest[n_ride + 4:])
    hr = x_ref.shape[0] // 2
    lo = lax.broadcasted_iota(jnp.int32, (hr, LANES), 1) < HEAD_DIM

    for r in range(2):
        rows = slice(r * hr, (r + 1) * hr)
        xn = _rms(x_ref[rows, :], n1_ref[...]).astype(BF16)
        rc = rc_ref[rows, :]
        rm = rm_ref[rows, :]
        rp = rp_ref[rows, :]

        def norm_rope(z, gain, width, scale):
            outs = []
            for cb in range(width // LANES):
                zb = z[:, cb * LANES:(cb + 1) * LANES]
                sq = zb * zb
                s_lo = jnp.sum(jnp.where(lo, sq, 0.0), axis=-1, keepdims=True)
                s_hi = jnp.sum(jnp.where(lo, 0.0, sq), axis=-1, keepdims=True)
                ms = jnp.where(lo, s_lo, s_hi) * (1.0 / HEAD_DIM)
                zn = zb * lax.rsqrt(ms + EPS) * gain
                rot = zn * rc + pltpu.roll(zn, LANES - ROT_DIM // 2, axis=1) * rm \
                    + pltpu.roll(zn, ROT_DIM // 2, axis=1) * rp
                outs.append(rot * scale)
            return outs

        q = _dot(xn, w_ref[:, D_SSM:D_SSM + D_ATTN])
        k = _dot(xn, w_ref[:, D_SSM + D_ATTN:D_SSM + D_ATTN + D_KV])
        for cb, blk in enumerate(norm_rope(q, qg_ref[...], D_ATTN, HEAD_DIM ** -0.5 * LOG2E)):
            q_ref[rows, cb * LANES:(cb + 1) * LANES] = blk.astype(q_ref.dtype)
        for cb, blk in enumerate(norm_rope(k, kg_ref[...], D_KV, 1.0)):
            k_ref[rows, cb * LANES:(cb + 1) * LANES] = blk
        u = _dot(xn, w_ref[:, 0:D_SSM])
        for c in range(hr // CHUNK):
            base = (r * (hr // CHUNK) + c) * PITCH
            u_ref[base:base + CHUNK, :] = u[c * CHUNK:(c + 1) * CHUNK]
            u_ref[base + CHUNK:base + PITCH, :] = jnp.zeros((PITCH - CHUNK, D_SSM), F32)
        v_ref[rows, :] = _dot(xn, w_ref[:, D_SSM + D_ATTN + D_KV:])


def _rope_tables(pos):
    half = ROT_DIM // 2
    inv = ROPE_THETA ** (-np.arange(half, dtype=np.float64) * 2.0 / ROT_DIM)
    ang = pos.astype(np.float64)[:, None] * inv[None, :]
    cos, sin = np.cos(ang), np.sin(ang)
    n = pos.shape[0]
    ones = np.ones((n, HEAD_DIM - ROT_DIM))
    zeros = np.zeros((n, HEAD_DIM - ROT_DIM))
    zh = np.zeros((n, half))
    rc = np.concatenate([cos, cos, ones], axis=1)
    rm = np.concatenate([-sin, zh, zeros], axis=1)
    rp = np.concatenate([zh, sin, zeros], axis=1)
    two = lambda t: jnp.asarray(np.concatenate([t, t], axis=1), F32)
    return two(rc), two(rm), two(rp)


def _in_proj(x2d, pos, tm, norm1, w_in_b, qg, kg, ride=()):
    t = x2d.shape[0]
    r_in, r_out, r_shape = _cast_riders(ride, t // tm, lambda i: i)
    rc, rm, rp = _rope_tables(pos)
    nrope = pos.shape[0] // tm
    rope_spec = pl.BlockSpec((tm, LANES), lambda i: (i % nrope, 0))
    const = lambda shape: pl.BlockSpec(shape, lambda i: (0, 0))
    tok = lambda w: pl.BlockSpec((tm, w), lambda i: (i, 0))
    return pl.pallas_call(
        _in_proj_kernel,
        grid=(t // tm,),
        in_specs=[tok(D_MODEL), const((1, D_MODEL)), const(w_in_b.shape),
                  const((1, LANES)), const((1, LANES)), rope_spec, rope_spec, rope_spec] + r_in,
        out_specs=[pl.BlockSpec((tm // CHUNK * PITCH, D_SSM), lambda i: (i, 0)), tok(D_ATTN), tok(D_KV),
                   tok(D_KV)] + r_out,
        out_shape=[jax.ShapeDtypeStruct((t // CHUNK * PITCH, D_SSM), F32), jax.ShapeDtypeStruct((t, D_ATTN), BF16),
                   jax.ShapeDtypeStruct((t, D_KV), F32), jax.ShapeDtypeStruct((t, D_KV), F32)] + r_shape,
        compiler_params=_params("arbitrary"),
        name="in_proj",
    )(x2d, norm1, w_in_b, qg, kg, rc, rm, rp, *[r[0] if isinstance(r, tuple) else r for r in ride])


def _ssm_state_kernel(up_ref, us_ref, wst_ref, xtp_ref, xts_ref, sp_ref, ss_ref, *, mp, bs):
    ng = xtp_ref.shape[0]
    pad = jnp.zeros((LANES - bs, LANES), F32)
    for s in range(CHUNK):
        vp = up_ref[pl.ds(s, mp, stride=PITCH), :].T.astype(BF16)
        vs = jnp.concatenate([us_ref[pl.ds(s, bs, stride=PITCH), :], pad], axis=0).T.astype(BF16)
        for g in range(ng):
            xtp_ref[g, s * GROUP_CH:(s + 1) * GROUP_CH, :] = vp[g * GROUP_CH:(g + 1) * GROUP_CH, :]
            xts_ref[g, s * GROUP_CH:(s + 1) * GROUP_CH, :] = vs[g * GROUP_CH:(g + 1) * GROUP_CH, :]
    tn = (((0,), (0,)), ((), ()))
    for g in range(ng):
        st = lax.dot_general(wst_ref[g], xtp_ref[g], tn, preferred_element_type=F32)
        sp_ref[g * mp:(g + 1) * mp, :] = st.T
        st = lax.dot_general(wst_ref[g], xts_ref[g], tn, preferred_element_type=F32)
        ss_ref[g * bs:(g + 1) * bs, :] = st.T[:bs]


def _cmul_add(h, ar2, ai2, s):
    return h * ar2 + pltpu.roll(h, STATE, axis=1) * ai2 + s


def _ssm_out_kernel(xtp_ref, xts_ref, blk_ref, wout_ref, sp_ref, ss_ref, arp_ref, aip_ref, ars_ref, ais_ref,
                    h0s_ref, yp_ref, ys_ref, hfp_ref, hfs_ref, hin_s, stp_s, sts_s, *, mp, bs, nchunk):
    ng = xtp_ref.shape[0]
    rows = mp // nchunk * ng
    ar = arp_ref[...]
    ai = aip_ref[...]
    ai_sw = pltpu.roll(ai, STATE, axis=1)
    s_all = [sp_ref[pl.ds(k, rows, stride=nchunk), :] for k in range(nchunk)]
    s_sw = [pltpu.roll(s, STATE, axis=1) for s in s_all]
    h = jnp.zeros((rows, 2 * STATE), F32)
    h_sw = h
    for k in range(nchunk):
        hin_s[pl.ds(k, rows, stride=nchunk), :] = h
        h, h_sw = h * ar + h_sw * ai + s_all[k], h_sw * ar + h * ai_sw + s_sw[k]
    hfp_ref[...] = h
    hfs_ref[...] = _cmul_add(h0s_ref[...], ars_ref[...], ais_ref[...], ss_ref[...])

    pad = jnp.zeros((LANES - bs, 2 * STATE), F32)
    hp = [hin_s[g * mp:(g + 1) * mp, :].astype(BF16) for g in range(ng)]
    hs = [jnp.concatenate([h0s_ref[g * bs:(g + 1) * bs, :], pad], axis=0).astype(BF16) for g in range(ng)]
    for tb in range(N_BLK):
        for g in range(ng):
            accp = accs = None
            for sb in range(tb + 1):
                b = blk_ref[g, tb - sb]
                dp = _dot(b, xtp_ref[g, sb * BLK:(sb + 1) * BLK, :])
                ds = _dot(b, xts_ref[g, sb * BLK:(sb + 1) * BLK, :])
                accp = dp if accp is None else accp + dp
                accs = ds if accs is None else accs + ds
            w = wout_ref[g, tb * BLK:(tb + 1) * BLK, :]
            accp = accp + _dot_nt(w, hp[g])
            accs = accs + _dot_nt(w, hs[g])
            for t0 in range(STEPS_PER_BLK):
                stp_s[t0, g * GROUP_CH:(g + 1) * GROUP_CH, :] = accp[t0 * GROUP_CH:(t0 + 1) * GROUP_CH, :]
                sts_s[t0, g * GROUP_CH:(g + 1) * GROUP_CH, :] = accs[t0 * GROUP_CH:(t0 + 1) * GROUP_CH, :]
        for t0 in range(STEPS_PER_BLK):
            t = tb * STEPS_PER_BLK + t0
            yp_ref[pl.ds(t, mp, stride=PITCH), :] = stp_s[t0].T
            ys_ref[pl.ds(t, bs, stride=PITCH), :] = sts_s[t0].T[:bs]
    for r in range(CHUNK, PITCH):
        yp_ref[pl.ds(r, mp, stride=PITCH), :] = jnp.zeros((mp, LANES), F32)
        ys_ref[pl.ds(r, bs, stride=PITCH), :] = jnp.zeros((bs, LANES), F32)


def _ssm(up, us, blk, wst, wout, a2, h0s, bp, nchunk, bs):
    g = N_GROUPS
    ng = GROUPS_PER_BLOCK
    mp = bp * nchunk
    kdim = CHUNK * GROUP_CH
    tok = lambda a: pl.BlockSpec((a.shape[0], LANES), lambda i: (0, i))
    grp = lambda *tail: pl.BlockSpec((ng,) + tail, lambda i: (i,) + (0,) * len(tail))
    rows = lambda r: pl.BlockSpec((ng * r, 2 * STATE), lambda i: (i, 0))
    xtp, xts, sp, ss = pl.pallas_call(
        functools.partial(_ssm_state_kernel, mp=mp, bs=bs),
        grid=(g // ng,),
        in_specs=[tok(up), tok(us), grp(kdim, 2 * STATE)],
        out_specs=[grp(kdim, mp), grp(kdim, LANES), rows(mp), rows(bs)],
        out_shape=[jax.ShapeDtypeStruct((g, kdim, mp), BF16), jax.ShapeDtypeStruct((g, kdim, LANES), BF16),
                   jax.ShapeDtypeStruct((g * mp, 2 * STATE), F32),
                   jax.ShapeDtypeStruct((g * bs, 2 * STATE), F32)],
        compiler_params=_params("arbitrary"),
        name="ssm_state",
    )(up, us, wst)

    arp = jnp.repeat(a2[:, 0, :], bp, axis=0)
    aip = jnp.repeat(a2[:, 1, :], bp, axis=0)
    ars = jnp.repeat(a2[:, 0, :], bs, axis=0)
    ais = jnp.repeat(a2[:, 1, :], bs, axis=0)
    yp, ys, hfp, hfs = pl.pallas_call(
        functools.partial(_ssm_out_kernel, mp=mp, bs=bs, nchunk=nchunk),
        grid=(g // ng,),
        in_specs=[grp(kdim, mp), grp(kdim, LANES), grp(N_BLK, BLK, BLK), grp(kdim, 2 * STATE), rows(mp), rows(bs),
                  rows(bp), rows(bp), rows(bs), rows(bs), rows(bs)],
        out_specs=[tok(up), tok(us), rows(bp), rows(bs)],
        out_shape=[jax.ShapeDtypeStruct(up.shape, F32), jax.ShapeDtypeStruct(us.shape, F32),
                   jax.ShapeDtypeStruct((g * bp, 2 * STATE), F32), jax.ShapeDtypeStruct((g * bs, 2 * STATE), F32)],
        scratch_shapes=[pltpu.VMEM((ng * mp, 2 * STATE), F32), pltpu.VMEM((STEPS_PER_BLK, LANES, mp), F32),
                        pltpu.VMEM((STEPS_PER_BLK, LANES, LANES), F32)],
        compiler_params=_params("arbitrary"),
        name="ssm_out",
    )(xtp, xts, blk, wout, sp, ss, arp, aip, ars, ais, h0s)
    return yp, ys, hfp, hfs


def _attn_kernel(sinks_ref, bias_ref, q_ref, kp_ref, kc_ref, vp_ref, vc_ref, *rest, qp, kw, masked):
    n_ride = (len(rest) - 1) // 2
    o_ref = rest[n_ride]
    _run_riders(rest[:n_ride], rest[n_ride + 1:])
    for e in range(q_ref.shape[0]):
        _attn_tile(sinks_ref, bias_ref, q_ref.at[e], kp_ref.at[e], kc_ref.at[e], vp_ref.at[e], vc_ref.at[e],
                   o_ref.at[e], qp=qp, kw=kw, masked=masked)


def _attn_tile(sinks_ref, bias_ref, q_ref, kp_ref, kc_ref, vp_ref, vc_ref, o_ref, *, qp, kw, masked):
    i = pl.program_id(1)
    n_units = q_ref.shape[0] // qp
    gw = Q_PER_KV * HEAD_DIM
    half = HEAD_DIM
    nl = Q_PER_KV * qp

    def duplicated(prev_ref, cur_ref):
        a = jnp.concatenate([prev_ref[...], cur_ref[...]], axis=0)
        lo = lax.broadcasted_iota(jnp.int32, (a.shape[0], 2 * half), 1) < half
        out = []
        for c in range(N_KV // 2):
            col = a[:, c * 2 * half:(c + 1) * 2 * half]
            rot = pltpu.roll(col, half, axis=1)
            out.append(jnp.where(lo, col, rot).astype(BF16))
            out.append(jnp.where(lo, rot, col).astype(BF16))
        return out

    kdup = duplicated(kp_ref, kc_ref)
    vdup = duplicated(vp_ref, vc_ref)
    pair_lo = lax.broadcasted_iota(jnp.int32, (qp, 2 * half), 1) < half
    lane_g = lax.broadcasted_iota(jnp.int32, (1, nl), 1) // qp

    for u in range(n_units):
        rows = slice(u * qp, (u + 1) * qp)
        win = slice(u * qp, u * qp + kw)
        if masked:
            bias = bias_ref[jnp.where(i == 0, 0, 1)] if u == 0 else bias_ref[1]
        scores = []
        for kv in range(N_KV):
            qm = []
            for g in range(Q_PER_KV):
                pair = q_ref[rows, kv * gw + (g // 2) * 2 * half:kv * gw + (g // 2 + 1) * 2 * half]
                keep = pair_lo if g % 2 == 0 else jnp.logical_not(pair_lo)
                qm.append(jnp.where(keep, pair, jnp.zeros_like(pair)))
            scores.append(_dot_nt(kdup[kv][win], jnp.concatenate(qm, axis=0)))
        probs = []
        for kv in range(N_KV):
            s = scores[kv] + bias if masked else scores[kv]
            sink = jnp.zeros((1, nl), F32)
            for g in range(Q_PER_KV):
                sink = jnp.where(lane_g == g, sinks_ref[kv * Q_PER_KV + g], sink)
            m = jnp.maximum(jnp.max(s, axis=0, keepdims=True), sink)
            p = jnp.exp2(s - m)
            den = jnp.sum(p, axis=0, keepdims=True) + jnp.exp2(sink - m)
            probs.append((p * (1.0 / den)).astype(BF16))
        for kv in range(N_KV):
            o = lax.dot_general(probs[kv], vdup[kv][win], (((0,), (0,)), ((), ())),
                                preferred_element_type=F32)
            for h in range(Q_PER_KV // 2):
                both = jnp.where(pair_lo, o[2 * h * qp:(2 * h + 1) * qp], o[(2 * h + 1) * qp:(2 * h + 2) * qp])
                o_ref[rows, kv * gw + h * 2 * half:kv * gw + (h + 1) * 2 * half] = both.astype(o_ref.dtype)


def _attention(sinks, q, k_prev, k_cur, v_prev, v_cur, qt, qp, masked, bb=1, ride=()):
    b, l, _ = q.shape
    nt = l // qt
    r_in, r_out, r_shape = _cast_riders(ride, b // bb * nt, lambda bi, i: bi * nt + i)
    per = qt // WINDOW
    kw = WINDOW + qp
    kc = np.arange(kw)[:, None] // CHUNK
    qc = (np.arange(Q_PER_KV * qp)[None, :] % qp) // CHUNK
    band = (kc >= qc) & (kc <= qc + WINDOW // CHUNK)
    bias = np.stack([np.where(band & (kc >= WINDOW // CHUNK), 0.0, NEG), np.where(band, 0.0, NEG)])
    bias = jnp.asarray(bias, F32)
    if masked:
        prev_map = lambda bi, i: (bi, jnp.maximum(i * per - 1, 0), 0)
    else:
        prev_map = lambda bi, i: (bi, 0, 0)
    cur = lambda w: pl.BlockSpec((bb, qt, w), lambda bi, i: (bi, i, 0))
    prev = pl.BlockSpec((bb, WINDOW, D_KV), prev_map)
    return pl.pallas_call(
        functools.partial(_attn_kernel, qp=qp, kw=kw, masked=masked),
        grid=(b // bb, nt),
        in_specs=[pl.BlockSpec(memory_space=pltpu.SMEM), pl.BlockSpec(bias.shape, lambda bi, i: (0, 0, 0)),
                  cur(D_ATTN), prev, cur(D_KV), prev, cur(D_KV)] + r_in,
        out_specs=[cur(D_ATTN)] + r_out,
        out_shape=[jax.ShapeDtypeStruct((b, l, D_ATTN), BF16)] + r_shape,
        compiler_params=_params("arbitrary", "arbitrary"),
        name="attention",
    )(sinks, bias, q, k_prev, k_cur, v_prev, v_cur, *[r[0] if isinstance(r, tuple) else r for r in ride])


def _mix_kernel(x_ref, y_ref, o_ref, n1_ref, wglu_ref, wga_ref, wgb_ref, wbs_ref, wba_ref, wout_ref,
                h_ref, xn_s, ys_s):
    j = pl.program_id(1)

    @pl.when(j == 0)
    def _():
        hr = x_ref.shape[0] // 2
        for r in range(2):
            rows = slice(r * hr, (r + 1) * hr)
            x = x_ref[rows, :]
            xn_s[rows, :] = _rms(x, n1_ref[...]).astype(BF16)
            c0 = r * (hr // CHUNK)
            y = jnp.concatenate([y_ref[c * PITCH:c * PITCH + CHUNK, :] for c in range(c0, c0 + hr // CHUNK)], axis=0)
            ya = jax.nn.gelu(y)
            ys_s[rows, :] = (ya * jax.nn.sigmoid(_dot(ya.astype(BF16), wglu_ref[...]))).astype(BF16)
            h_ref[rows, :] = x

    xn = xn_s[...]
    ga = jax.nn.sigmoid(_dot(xn, wga_ref[...]))
    gb = jax.nn.sigmoid(_dot(xn, wgb_ref[...]))
    mixed = ga * _dot(ys_s[...], wbs_ref[...]) + gb * _dot(o_ref[...], wba_ref[...])
    h_ref[...] += _dot(mixed.astype(BF16), wout_ref[...])


def _mix(x2d, y2d, o2d, norm1, wglu, wgate, wbs, wba, wout, tm):
    t = x2d.shape[0]
    tn = wbs.shape[2]
    nj = D_MODEL // tn
    nt = t // tm
    tok = lambda w: pl.BlockSpec((tm, w), lambda i, j: (i, 0))
    early = lambda rows, w, at: pl.BlockSpec(
        (rows, w), lambda i, j: (jnp.minimum(i + jnp.where(j >= at, 1, 0), nt - 1), 0))
    colblk = lambda rows, off: pl.BlockSpec((None, rows, tn), lambda i, j: (off + j, 0, 0))
    return pl.pallas_call(
        _mix_kernel,
        grid=(nt, nj),
        in_specs=[early(tm, D_MODEL, nj // 2), early(tm // CHUNK * PITCH, D_SSM, nj // 2 + 1), tok(D_ATTN),
                  pl.BlockSpec((1, D_MODEL), lambda i, j: (0, 0)),
                  pl.BlockSpec((D_SSM, D_SSM), lambda i, j: (0, 0)),
                  colblk(D_MODEL, 0), colblk(D_MODEL, nj), colblk(D_SSM, 0), colblk(D_ATTN, 0),
                  pl.BlockSpec((tn, D_MODEL), lambda i, j: (j, 0))],
        out_specs=tok(D_MODEL),
        out_shape=jax.ShapeDtypeStruct((t, D_MODEL), F32),
        scratch_shapes=[pltpu.VMEM((tm, D_MODEL), BF16), pltpu.VMEM((tm, D_SSM), BF16)],
        compiler_params=_params("arbitrary", "arbitrary"),
        name="mix",
    )(x2d, y2d, o2d, norm1, wglu, wgate, wgate, wbs, wba, wout)


def _ffn_kernel(h_ref, n2_ref, wg_ref, wu_ref, wd_ref, out_ref, hn_s):
    j = pl.program_id(1)

    @pl.when(j == 0)
    def _():
        h = h_ref[...]
        hn_s[...] = _rms(h, n2_ref[...]).astype(BF16)
        out_ref[...] = h

    hn = hn_s[...]
    act = jax.nn.silu(_dot(hn, wg_ref[...])) * _dot(hn, wu_ref[...])
    out_ref[...] += _dot(act.astype(BF16), wd_ref[...])


def _ffn(h2d, norm2, wg, wu, wd, tm):
    t = h2d.shape[0]
    nf, _, tf = wg.shape
    d_ff = nf * tf
    nt = t // tm
    tok = pl.BlockSpec((tm, D_MODEL), lambda i, j: (i, 0))
    h_spec = pl.BlockSpec((tm, D_MODEL), lambda i, j: (jnp.minimum(i + jnp.where(j >= nf // 2, 1, 0), nt - 1), 0))
    return pl.pallas_call(
        _ffn_kernel,
        grid=(nt, nf),
        in_specs=[h_spec, pl.BlockSpec((1, D_MODEL), lambda i, j: (0, 0)),
                  pl.BlockSpec((None, D_MODEL, tf), lambda i, j: (j, 0, 0)),
                  pl.BlockSpec((None, D_MODEL, tf), lambda i, j: (j, 0, 0)),
                  pl.BlockSpec((tf, D_MODEL), lambda i, j: (j, 0))],
        out_specs=tok,
        out_shape=jax.ShapeDtypeStruct((t, D_MODEL), F32),
        scratch_shapes=[pltpu.VMEM((tm, D_MODEL), BF16)],
        compiler_params=_params("arbitrary", "arbitrary"),
        name="ffn",
    )(h2d, norm2, wg, wu, wd)


def kernel(x_prompt, x_sample, cache_k, cache_v, state_ssm_re, state_ssm_im, norm1, w_in, q_norm, k_norm,
           sinks, ssm_a_re, ssm_a_im, ssm_log_dt, ssm_b_re, ssm_b_im, ssm_c_re, ssm_c_im, ssm_d, w_glu,
           w_br_ssm, w_br_attn, w_gate, w_out, norm2, w_ffn_gate, w_ffn_up, w_ffn_down):
    bp, lp, _ = x_prompt.shape
    bs, ls, _ = x_sample.shape
    ncp = lp // CHUNK
    assert ls == CHUNK and cache_k.shape[2] == WINDOW and norm1.shape[0] == 1
    l = 0

    f32 = lambda a: a[l].astype(F32)
    n1 = norm1[l].astype(F32).reshape(1, D_MODEL)
    n2 = norm2[l].astype(F32).reshape(1, D_MODEL)
    qg = jnp.tile(q_norm[l].astype(F32), LANES // HEAD_DIM).reshape(1, LANES)
    kg = jnp.tile(k_norm[l].astype(F32), LANES // HEAD_DIM).reshape(1, LANES)
    sink = sinks[l].astype(F32) * LOG2E

    blk, wst, wout_t, a2, w_in_b = _ssm_prep(ssm_a_re[l], ssm_a_im[l], ssm_log_dt[l], ssm_b_re[l], ssm_b_im[l],
                                             ssm_c_re[l], ssm_c_im[l], ssm_d[l], f32(w_in))

    xp = x_prompt.astype(F32).reshape(bp * lp, D_MODEL)
    xs = x_sample.astype(F32).reshape(bs * ls, D_MODEL)
    tm = TM_IN
    up, qp, kp, vp, wfg_b, wfu_b, wfd_b = _in_proj(
        xp, np.arange(lp), tm, n1, w_in_b, qg, kg,
        ride=((f32(w_ffn_gate), TF_FFN), (f32(w_ffn_up), TF_FFN), f32(w_ffn_down)))
    us, qs, ks, vs = _in_proj(xs, np.tile(PAST_LEN + np.arange(ls), tm // ls), tm, n1, w_in_b, qg, kg)

    h0s = jnp.concatenate([state_ssm_re[l], state_ssm_im[l]], axis=-1).astype(F32)
    h0s = jnp.swapaxes(h0s, 0, 1).reshape(N_GROUPS * bs, 2 * STATE)
    y_p, y_s, hfp, hfs = _ssm(up, us, blk, wst, wout_t, a2, h0s, bp, ncp, bs)

    def states(hf, b):
        hf = jnp.swapaxes(hf.reshape(N_GROUPS, b, 2 * STATE), 0, 1)
        return hf[None, :, :, :STATE], hf[None, :, :, STATE:]

    kp3, vp3 = kp.reshape(bp, lp, D_KV), vp.reshape(bp, lp, D_KV)
    o_p, wgate_b, wout_b, wbs_b, wba_b, wglu_b = _attention(
        sink, qp.reshape(bp, lp, D_ATTN), kp3, kp3, vp3, vp3, 2048, 2 * CHUNK, True,
        ride=((f32(w_gate), TN_MIX), f32(w_out), (f32(w_br_ssm), TN_MIX), (f32(w_br_attn), TN_MIX), f32(w_glu)))
    ck = cache_k[l].astype(F32).reshape(bs, WINDOW, D_KV)
    cv = cache_v[l].astype(F32).reshape(bs, WINDOW, D_KV)
    ks3, vs3 = ks.reshape(bs, ls, D_KV), vs.reshape(bs, ls, D_KV)
    o_s, = _attention(sink, qs.reshape(bs, ls, D_ATTN), ck, ks3, cv, vs3, ls, ls, False, bb=8)

    outs = []
    for x2d, y2d, o3d in ((xp, y_p, o_p), (xs, y_s, o_s)):
        h = _mix(x2d, y2d, o3d.reshape(-1, D_ATTN), n1, wglu_b, wgate_b, wbs_b, wba_b, wout_b, TM_MIX)
        outs.append(_ffn(h, n2, wfg_b, wfu_b, wfd_b, TM_FFN))

    win = lambda a, b, n: a.reshape(b, n, N_KV, HEAD_DIM)[None]
    k_win_p = win(kp3[:, lp - WINDOW:], bp, WINDOW)
    v_win_p = win(vp3[:, lp - WINDOW:], bp, WINDOW)
    k_win_s = win(jnp.concatenate([ck, ks3], axis=1)[:, ls:], bs, WINDOW)
    v_win_s = win(jnp.concatenate([cv, vs3], axis=1)[:, ls:], bs, WINDOW)
    re_p, im_p = states(hfp, bp)
    re_s, im_s = states(hfs, bs)
    dt = x_prompt.dtype
    return (outs[0].reshape(bp, lp, D_MODEL).astype(dt), outs[1].reshape(bs, ls, D_MODEL).astype(dt),
            k_win_p.astype(dt), v_win_p.astype(dt), re_p.astype(dt), im_p.astype(dt),
            k_win_s.astype(dt), v_win_s.astype(dt), re_s.astype(dt), im_s.astype(dt))
```

```python
import functools

import jax
import jax.numpy as jnp
import numpy as np
from jax import lax
from jax.experimental import pallas as pl
from jax.experimental.pallas import tpu as pltpu

D_MODEL = 2048
CHUNK = 64
D_SSM = 1024
GROUP_CH = 16
N_GROUPS = 64
STATE = 64
HEAD_DIM = 64
N_HEADS = 16
N_KV = 4
Q_PER_KV = 4
D_ATTN = 1024
D_KV = 256
WINDOW = 128
ROT_DIM = 16
ROPE_THETA = 500000.0
PAST_LEN = 1024
EPS = 1e-6
NEG = -1e30
LOG2E = 1.4426950408889634

LANES = 128
BLK = 256
GROUPS_PER_BLOCK = LANES // GROUP_CH
PITCH = CHUNK + 8
STEPS_PER_BLK = BLK // GROUP_CH
N_BLK = CHUNK // STEPS_PER_BLK
V7X_VMEM_LIMIT = 56 * 1024 * 1024
TM_IN, TM_MIX, TN_MIX, TM_FFN, TF_FFN = 512, 512, 512, 1024, 512

F32 = jnp.float32
BF16 = jnp.bfloat16


def _dot(a, b):
    return jnp.dot(a, b, preferred_element_type=F32)


def _dot_nt(a, b):
    return lax.dot_general(a, b, (((1,), (1,)), ((), ())), preferred_element_type=F32)


def _params(*sem):
    return pltpu.CompilerParams(dimension_semantics=sem, vmem_limit_bytes=V7X_VMEM_LIMIT)


def _cast_riders(weights, nsteps, step_of):
    in_specs, out_specs, out_shape = [], [], []
    for item in weights:
        w, cb = item if isinstance(item, tuple) else (item, None)
        rows, cols = w.shape
        slab = rows // nsteps
        assert slab * nsteps == rows and slab % 16 == 0, (w.shape, nsteps)
        in_specs.append(pl.BlockSpec((slab, cols), lambda *ids: (step_of(*ids), 0)))
        if cb is None:
            out_specs.append(pl.BlockSpec((slab, cols), lambda *ids: (step_of(*ids), 0)))
            out_shape.append(jax.ShapeDtypeStruct(w.shape, BF16))
        else:
            out_specs.append(pl.BlockSpec((cols // cb, slab, cb), lambda *ids: (0, step_of(*ids), 0)))
            out_shape.append(jax.ShapeDtypeStruct((cols // cb, rows, cb), BF16))
    return in_specs, out_specs, out_shape


def _run_riders(src_refs, dst_refs):
    for s, d in zip(src_refs, dst_refs):
        if len(d.shape) == 2:
            d[...] = s[...].astype(d.dtype)
        else:
            cb = d.shape[2]
            for c in range(d.shape[0]):
                d[c] = s[:, c * cb:(c + 1) * cb].astype(d.dtype)


def _ssm_prep_kernel(*refs):
    ins, wf_ref, outs, wb_ref, scratch = refs[:8], refs[8], refs[9:13], refs[13], refs[14:]
    _run_riders([wf_ref], [wb_ref])
    for g in range(ins[0].shape[0]):
        _ssm_prep_group(*[r.at[g] for r in ins + outs + scratch])


def _ssm_prep_group(are_ref, aim_ref, ldt_ref, bt_ref, bts_ref, c_ref, cs_ref, d_ref,
                    blk_ref, wst_ref, wout_ref, a2_ref, l_s, z_s, w_s):
    lane = lax.broadcasted_iota(jnp.int32, (1, 2 * STATE), 1)
    sg = jnp.where(lane < STATE, 1.0, -1.0).astype(F32)
    a_re = are_ref[...]
    a_im = aim_ref[...]
    dt = jnp.exp(ldt_ref[...])
    lre = a_re * dt
    lim = a_im * dt

    def cpow(e):
        mag = jnp.exp(e * lre)
        ang = e * lim
        return mag * jnp.cos(ang), mag * jnp.sin(ang)

    def cmul(ar, ai, br, bi):
        return ar * br - ai * bi, ar * bi + ai * br

    col = lax.broadcasted_iota(jnp.int32, (STEPS_PER_BLK, 1), 0).astype(F32)
    t_r, t_i = cpow(col)
    r_r, r_i = cpow(float(STEPS_PER_BLK - 1) - col)
    h_r, h_i = cpow(float(STEPS_PER_BLK) * col[:8])
    ab_r, ab_i = t_r[1:2], t_i[1:2]

    fr, fi = ab_r - 1.0, ab_i
    den = a_re * a_re + a_im * a_im
    cr = (fr * a_re + fi * a_im) / den
    ci = (fi * a_re - fr * a_im) / den
    bb = cr * bt_ref[...] - (ci * sg) * bts_ref[...]
    bbs = cr * bts_ref[...] + (ci * sg) * bt_ref[...]
    cc = c_ref[...]
    ccs = cs_ref[...]

    def outer_rows(pr, pi, m, ms, dst, row0, conj):
        pa = pr * sg if conj else pr
        pb = -pi if conj else -(pi * sg)
        for r in range(pr.shape[0]):
            z = pa[r:r + 1, :] * m + pb[r:r + 1, :] * ms
            dst[row0 + r * GROUP_CH:row0 + (r + 1) * GROUP_CH, :] = z

    e_r, e_i = cmul(ab_r, ab_i, r_r, r_i)
    outer_rows(e_r, e_i, bb, bbs, l_s, 0, False)
    l2 = l_s[...].astype(BF16)

    for d in range(N_BLK - 1):
        e_r, e_i = cmul(h_r[d:d + 1], h_i[d:d + 1], t_r, t_i)
        outer_rows(e_r, e_i, cc, ccs, z_s, 0, True)
        blk_ref[d + 1] = _dot_nt(z_s[...].astype(BF16), l2).astype(BF16)

    outer_rows(r_r, r_i, bb, bbs, z_s, 0, False)
    kr = _dot_nt((cc * sg).astype(BF16), z_s[...].astype(BF16))
    rows = lax.broadcasted_iota(jnp.int32, (GROUP_CH, BLK), 0)
    lanes = lax.broadcasted_iota(jnp.int32, (GROUP_CH, BLK), 1)
    kr = kr + jnp.where(lanes == rows + (BLK - GROUP_CH), d_ref[...], 0.0)
    for t0 in range(STEPS_PER_BLK):
        sh = GROUP_CH * (STEPS_PER_BLK - 1 - t0)
        piece = kr if sh == 0 else jnp.where(lanes < BLK - sh, pltpu.roll(kr, BLK - sh, axis=1), 0.0)
        blk_ref[0, t0 * GROUP_CH:(t0 + 1) * GROUP_CH, :] = piece.astype(BF16)

    for sb in range(N_BLK):
        m = N_BLK - 1 - sb
        e_r, e_i = cmul(h_r[m:m + 1], h_i[m:m + 1], r_r, r_i)
        outer_rows(e_r, e_i, bb, bbs, w_s, sb * BLK, False)
    wst_ref[...] = w_s[...].astype(BF16)

    for tb in range(N_BLK):
        e_r, e_i = cmul(*cmul(ab_r, ab_i, h_r[tb:tb + 1], h_i[tb:tb + 1]), t_r, t_i)
        outer_rows(e_r, e_i, cc, ccs, w_s, tb * BLK, True)
    wout_ref[...] = w_s[...].astype(BF16)

    a2_ref[0:1, :] = h_r[N_BLK:N_BLK + 1]
    a2_ref[1:2, :] = -(h_i[N_BLK:N_BLK + 1] * sg)


def _ssm_prep(a_re, a_im, log_dt, b_re, b_im, c_re, c_im, d_skip, ride):
    g = N_GROUPS
    ng = GROUPS_PER_BLOCK
    r_in, r_out, r_shape = _cast_riders([ride], g // ng, lambda i: i)
    f = lambda a: a.astype(F32)
    dup = lambda a: jnp.concatenate([f(a), f(a)], axis=-1).reshape(g, 1, 2 * STATE)
    bt_re = jnp.swapaxes(f(b_re), 1, 2)
    bt_im = jnp.swapaxes(f(b_im), 1, 2)
    bt = jnp.concatenate([bt_re, bt_im], axis=-1)
    bts = jnp.concatenate([bt_im, bt_re], axis=-1)
    cc = jnp.concatenate([f(c_re), f(c_im)], axis=-1)
    ccs = jnp.concatenate([f(c_im), f(c_re)], axis=-1)
    d_pad = jnp.pad(f(d_skip).reshape(g, 1, GROUP_CH), ((0, 0), (0, 0), (BLK - GROUP_CH, 0)))
    vec = lambda n: pl.BlockSpec((ng, 1, n), lambda i: (i, 0, 0))
    mat = pl.BlockSpec((ng, GROUP_CH, 2 * STATE), lambda i: (i, 0, 0))
    kdim = CHUNK * GROUP_CH
    return pl.pallas_call(
        _ssm_prep_kernel,
        grid=(g // ng,),
        in_specs=[vec(2 * STATE), vec(2 * STATE), vec(1), mat, mat, mat, mat, vec(BLK)] + r_in,
        out_specs=[pl.BlockSpec((ng, N_BLK, BLK, BLK), lambda i: (i, 0, 0, 0)),
                   pl.BlockSpec((ng, kdim, 2 * STATE), lambda i: (i, 0, 0)),
                   pl.BlockSpec((ng, kdim, 2 * STATE), lambda i: (i, 0, 0)),
                   pl.BlockSpec((ng, 2, 2 * STATE), lambda i: (i, 0, 0))] + r_out,
        out_shape=[jax.ShapeDtypeStruct((g, N_BLK, BLK, BLK), BF16),
                   jax.ShapeDtypeStruct((g, kdim, 2 * STATE), BF16),
                   jax.ShapeDtypeStruct((g, kdim, 2 * STATE), BF16),
                   jax.ShapeDtypeStruct((g, 2, 2 * STATE), F32)] + r_shape,
        scratch_shapes=[pltpu.VMEM((ng, BLK, 2 * STATE), F32), pltpu.VMEM((ng, BLK, 2 * STATE), F32),
                        pltpu.VMEM((ng, kdim, 2 * STATE), F32)],
        compiler_params=_params("arbitrary"),
        name="ssm_prep",
    )(dup(a_re), dup(a_im), f(log_dt).reshape(g, 1, 1), bt, bts, cc, ccs, d_pad, ride)


def _rms(x, g):
    return x * lax.rsqrt(jnp.mean(x * x, axis=-1, keepdims=True) + EPS) * g


def _in_proj_kernel(x_ref, n1_ref, w_ref, qg_ref, kg_ref, rc_ref, rm_ref, rp_ref, *rest):
    n_ride = (len(rest) - 4) // 2
    u_ref, q_ref, k_ref, v_ref = rest[n_ride:n_ride + 4]
    _run_riders(rest[:n_ride], rest[n_ride + 4:])
    hr = x_ref.shape[0] // 2
    lo = lax.broadcasted_iota(jnp.int32, (hr, LANES), 1) < HEAD_DIM

    for r in range(2):
        rows = slice(r * hr, (r + 1) * hr)
        xn = _rms(x_ref[rows, :], n1_ref[...]).astype(BF16)
        rc = rc_ref[rows, :]
        rm = rm_ref[rows, :]
        rp = rp_ref[rows, :]

        def norm_rope(z, gain, width, scale):
            outs = []
            for cb in range(width // LANES):
                zb = z[:, cb * LANES:(cb + 1) * LANES]
                sq = zb * zb
                s_lo = jnp.sum(jnp.where(lo, sq, 0.0), axis=-1, keepdims=True)
                s_hi = jnp.sum(jnp.where(lo, 0.0, sq), axis=-1, keepdims=True)
                ms = jnp.where(lo, s_lo, s_hi) * (1.0 / HEAD_DIM)
                zn = zb * lax.rsqrt(ms + EPS) * gain
                rot = zn * rc + pltpu.roll(zn, LANES - ROT_DIM // 2, axis=1) * rm \
                    + pltpu.roll(zn, ROT_DIM // 2, axis=1) * rp
                outs.append(rot * scale)
            return outs

        q = _dot(xn, w_ref[:, D_SSM:D_SSM + D_ATTN])
        k = _dot(xn, w_ref[:, D_SSM + D_ATTN:D_SSM + D_ATTN + D_KV])
        for cb, blk in enumerate(norm_rope(q, qg_ref[...], D_ATTN, HEAD_DIM ** -0.5 * LOG2E)):
            q_ref[rows, cb * LANES:(cb + 1) * LANES] = blk.astype(q_ref.dtype)
        for cb, blk in enumerate(norm_rope(k, kg_ref[...], D_KV, 1.0)):
            k_ref[rows, cb * LANES:(cb + 1) * LANES] = blk
        u = _dot(xn, w_ref[:, 0:D_SSM])
        for c in range(hr // CHUNK):
            base = (r * (hr // CHUNK) + c) * PITCH
            u_ref[base:base + CHUNK, :] = u[c * CHUNK:(c + 1) * CHUNK]
            u_ref[base + CHUNK:base + PITCH, :] = jnp.zeros((PITCH - CHUNK, D_SSM), F32)
        v_ref[rows, :] = _dot(xn, w_ref[:, D_SSM + D_ATTN + D_KV:])


def _rope_tables(pos):
    half = ROT_DIM // 2
    inv = ROPE_THETA ** (-np.arange(half, dtype=np.float64) * 2.0 / ROT_DIM)
    ang = pos.astype(np.float64)[:, None] * inv[None, :]
    cos, sin = np.cos(ang), np.sin(ang)
    n = pos.shape[0]
    ones = np.ones((n, HEAD_DIM - ROT_DIM))
    zeros = np.zeros((n, HEAD_DIM - ROT_DIM))
    zh = np.zeros((n, half))
    rc = np.concatenate([cos, cos, ones], axis=1)
    rm = np.concatenate([-sin, zh, zeros], axis=1)
    rp = np.concatenate([zh, sin, zeros], axis=1)
    two = lambda t: jnp.asarray(np.concatenate([t, t], axis=1), F32)
    return two(rc), two(rm), two(rp)


def _in_proj(x2d, pos, tm, norm1, w_in_b, qg, kg, ride=()):
    t = x2d.shape[0]
    r_in, r_out, r_shape = _cast_riders(ride, t // tm, lambda i: i)
    rc, rm, rp = _rope_tables(pos)
    nrope = pos.shape[0] // tm
    rope_spec = pl.BlockSpec((tm, LANES), lambda i: (i % nrope, 0))
    const = lambda shape: pl.BlockSpec(shape, lambda i: (0, 0))
    tok = lambda w: pl.BlockSpec((tm, w), lambda i: (i, 0))
    return pl.pallas_call(
        _in_proj_kernel,
        grid=(t // tm,),
        in_specs=[tok(D_MODEL), const((1, D_MODEL)), const(w_in_b.shape),
                  const((1, LANES)), const((1, LANES)), rope_spec, rope_spec, rope_spec] + r_in,
        out_specs=[pl.BlockSpec((tm // CHUNK * PITCH, D_SSM), lambda i: (i, 0)), tok(D_ATTN), tok(D_KV),
                   tok(D_KV)] + r_out,
        out_shape=[jax.ShapeDtypeStruct((t // CHUNK * PITCH, D_SSM), F32), jax.ShapeDtypeStruct((t, D_ATTN), BF16),
                   jax.ShapeDtypeStruct((t, D_KV), F32), jax.ShapeDtypeStruct((t, D_KV), F32)] + r_shape,
        compiler_params=_params("arbitrary"),
        name="in_proj",
    )(x2d, norm1, w_in_b, qg, kg, rc, rm, rp, *[r[0] if isinstance(r, tuple) else r for r in ride])


def _ssm_state_kernel(up_ref, us_ref, wst_ref, xtp_ref, xts_ref, sp_ref, ss_ref, *, mp, bs):
    ng = xtp_ref.shape[0]
    pad = jnp.zeros((LANES - bs, LANES), F32)
    for s in range(CHUNK):
        vp = up_ref[pl.ds(s, mp, stride=PITCH), :].T.astype(BF16)
        vs = jnp.concatenate([us_ref[pl.ds(s, bs, stride=PITCH), :], pad], axis=0).T.astype(BF16)
        for g in range(ng):
            xtp_ref[g, s * GROUP_CH:(s + 1) * GROUP_CH, :] = vp[g * GROUP_CH:(g + 1) * GROUP_CH, :]
            xts_ref[g, s * GROUP_CH:(s + 1) * GROUP_CH, :] = vs[g * GROUP_CH:(g + 1) * GROUP_CH, :]
    tn = (((0,), (0,)), ((), ()))
    for g in range(ng):
        st = lax.dot_general(wst_ref[g], xtp_ref[g], tn, preferred_element_type=F32)
        sp_ref[g * mp:(g + 1) * mp, :] = st.T
        st = lax.dot_general(wst_ref[g], xts_ref[g], tn, preferred_element_type=F32)
        ss_ref[g * bs:(g + 1) * bs, :] = st.T[:bs]


def _cmul_add(h, ar2, ai2, s):
    return h * ar2 + pltpu.roll(h, STATE, axis=1) * ai2 + s


def _ssm_out_kernel(xtp_ref, xts_ref, blk_ref, wout_ref, sp_ref, ss_ref, arp_ref, aip_ref, ars_ref, ais_ref,
                    h0s_ref, yp_ref, ys_ref, hfp_ref, hfs_ref, hin_s, stp_s, sts_s, *, mp, bs, nchunk):
    ng = xtp_ref.shape[0]
    rows = mp // nchunk * ng
    ar = arp_ref[...]
    ai = aip_ref[...]
    ai_sw = pltpu.roll(ai, STATE, axis=1)
    s_all = [sp_ref[pl.ds(k, rows, stride=nchunk), :] for k in range(nchunk)]
    s_sw = [pltpu.roll(s, STATE, axis=1) for s in s_all]
    h = jnp.zeros((rows, 2 * STATE), F32)
    h_sw = h
    for k in range(nchunk):
        hin_s[pl.ds(k, rows, stride=nchunk), :] = h
        h, h_sw = h * ar + h_sw * ai + s_all[k], h_sw * ar + h * ai_sw + s_sw[k]
    hfp_ref[...] = h
    hfs_ref[...] = _cmul_add(h0s_ref[...], ars_ref[...], ais_ref[...], ss_ref[...])

    pad = jnp.zeros((LANES - bs, 2 * STATE), F32)
    hp = [hin_s[g * mp:(g + 1) * mp, :].astype(BF16) for g in range(ng)]
    hs = [jnp.concatenate([h0s_ref[g * bs:(g + 1) * bs, :], pad], axis=0).astype(BF16) for g in range(ng)]
    for tb in range(N_BLK):
        for g in range(ng):
            accp = accs = None
            for sb in range(tb + 1):
                b = blk_ref[g, tb - sb]
                dp = _dot(b, xtp_ref[g, sb * BLK:(sb + 1) * BLK, :])
                ds = _dot(b, xts_ref[g, sb * BLK:(sb + 1) * BLK, :])
                accp = dp if accp is None else accp + dp
                accs = ds if accs is None else accs + ds
            w = wout_ref[g, tb * BLK:(tb + 1) * BLK, :]
            accp = accp + _dot_nt(w, hp[g])
            accs = accs + _dot_nt(w, hs[g])
            for t0 in range(STEPS_PER_BLK):
                stp_s[t0, g * GROUP_CH:(g + 1) * GROUP_CH, :] = accp[t0 * GROUP_CH:(t0 + 1) * GROUP_CH, :]
                sts_s[t0, g * GROUP_CH:(g + 1) * GROUP_CH, :] = accs[t0 * GROUP_CH:(t0 + 1) * GROUP_CH, :]
        for t0 in range(STEPS_PER_BLK):
            t = tb * STEPS_PER_BLK + t0
            yp_ref[pl.ds(t, mp, stride=PITCH), :] = stp_s[t0].T
            ys_ref[pl.ds(t, bs, stride=PITCH), :] = sts_s[t0].T[:bs]
    for r in range(CHUNK, PITCH):
        yp_ref[pl.ds(r, mp, stride=PITCH), :] = jnp.zeros((mp, LANES), F32)
        ys_ref[pl.ds(r, bs, stride=PITCH), :] = jnp.zeros((bs, LANES), F32)


def _ssm(up, us, blk, wst, wout, a2, h0s, bp, nchunk, bs):
    g = N_GROUPS
    ng = GROUPS_PER_BLOCK
    mp = bp * nchunk
    kdim = CHUNK * GROUP_CH
    tok = lambda a: pl.BlockSpec((a.shape[0], LANES), lambda i: (0, i))
    grp = lambda *tail: pl.BlockSpec((ng,) + tail, lambda i: (i,) + (0,) * len(tail))
    rows = lambda r: pl.BlockSpec((ng * r, 2 * STATE), lambda i: (i, 0))
    xtp, xts, sp, ss = pl.pallas_call(
        functools.partial(_ssm_state_kernel, mp=mp, bs=bs),
        grid=(g // ng,),
        in_specs=[tok(up), tok(us), grp(kdim, 2 * STATE)],
        out_specs=[grp(kdim, mp), grp(kdim, LANES), rows(mp), rows(bs)],
        out_shape=[jax.ShapeDtypeStruct((g, kdim, mp), BF16), jax.ShapeDtypeStruct((g, kdim, LANES), BF16),
                   jax.ShapeDtypeStruct((g * mp, 2 * STATE), F32),
                   jax.ShapeDtypeStruct((g * bs, 2 * STATE), F32)],
        compiler_params=_params("arbitrary"),
        name="ssm_state",
    )(up, us, wst)

    arp = jnp.repeat(a2[:, 0, :], bp, axis=0)
    aip = jnp.repeat(a2[:, 1, :], bp, axis=0)
    ars = jnp.repeat(a2[:, 0, :], bs, axis=0)
    ais = jnp.repeat(a2[:, 1, :], bs, axis=0)
    yp, ys, hfp, hfs = pl.pallas_call(
        functools.partial(_ssm_out_kernel, mp=mp, bs=bs, nchunk=nchunk),
        grid=(g // ng,),
        in_specs=[grp(kdim, mp), grp(kdim, LANES), grp(N_BLK, BLK, BLK), grp(kdim, 2 * STATE), rows(mp), rows(bs),
                  rows(bp), rows(bp), rows(bs), rows(bs), rows(bs)],
        out_specs=[tok(up), tok(us), rows(bp), rows(bs)],
        out_shape=[jax.ShapeDtypeStruct(up.shape, F32), jax.ShapeDtypeStruct(us.shape, F32),
                   jax.ShapeDtypeStruct((g * bp, 2 * STATE), F32), jax.ShapeDtypeStruct((g * bs, 2 * STATE), F32)],
        scratch_shapes=[pltpu.VMEM((ng * mp, 2 * STATE), F32), pltpu.VMEM((STEPS_PER_BLK, LANES, mp), F32),
                        pltpu.VMEM((STEPS_PER_BLK, LANES, LANES), F32)],
        compiler_params=_params("arbitrary"),
        name="ssm_out",
    )(xtp, xts, blk, wout, sp, ss, arp, aip, ars, ais, h0s)
    return yp, ys, hfp, hfs


def _attn_kernel(sinks_ref, bias_ref, q_ref, kp_ref, kc_ref, vp_ref, vc_ref, *rest, qp, kw, masked):
    n_ride = (len(rest) - 1) // 2
    o_ref = rest[n_ride]
    _run_riders(rest[:n_ride], rest[n_ride + 1:])
    for e in range(q_ref.shape[0]):
        _attn_tile(sinks_ref, bias_ref, q_ref.at[e], kp_ref.at[e], kc_ref.at[e], vp_ref.at[e], vc_ref.at[e],
                   o_ref.at[e], qp=qp, kw=kw, masked=masked)


def _attn_tile(sinks_ref, bias_ref, q_ref, kp_ref, kc_ref, vp_ref, vc_ref, o_ref, *, qp, kw, masked):
    i = pl.program_id(1)
    n_units = q_ref.shape[0] // qp
    gw = Q_PER_KV * HEAD_DIM
    half = HEAD_DIM
    nl = Q_PER_KV * qp

    def duplicated(prev_ref, cur_ref):
        a = jnp.concatenate([prev_ref[...], cur_ref[...]], axis=0)
        lo = lax.broadcasted_iota(jnp.int32, (a.shape[0], 2 * half), 1) < half
        out = []
        for c in range(N_KV // 2):
            col = a[:, c * 2 * half:(c + 1) * 2 * half]
            rot = pltpu.roll(col, half, axis=1)
            out.append(jnp.where(lo, col, rot).astype(BF16))
            out.append(jnp.where(lo, rot, col).astype(BF16))
        return out

    kdup = duplicated(kp_ref, kc_ref)
    vdup = duplicated(vp_ref, vc_ref)
    pair_lo = lax.broadcasted_iota(jnp.int32, (qp, 2 * half), 1) < half
    lane_g = lax.broadcasted_iota(jnp.int32, (1, nl), 1) // qp

    for u in range(n_units):
        rows = slice(u * qp, (u + 1) * qp)
        win = slice(u * qp, u * qp + kw)
        if masked:
            bias = bias_ref[jnp.where(i == 0, 0, 1)] if u == 0 else bias_ref[1]
        scores = []
        for kv in range(N_KV):
            qm = []
            for g in range(Q_PER_KV):
                pair = q_ref[rows, kv * gw + (g // 2) * 2 * half:kv * gw + (g // 2 + 1) * 2 * half]
                keep = pair_lo if g % 2 == 0 else jnp.logical_not(pair_lo)
                qm.append(jnp.where(keep, pair, jnp.zeros_like(pair)))
            scores.append(_dot_nt(kdup[kv][win], jnp.concatenate(qm, axis=0)))
        probs = []
        for kv in range(N_KV):
            s = scores[kv] + bias if masked else scores[kv]
            sink = jnp.zeros((1, nl), F32)
            for g in range(Q_PER_KV):
                sink = jnp.where(lane_g == g, sinks_ref[kv * Q_PER_KV + g], sink)
            m = jnp.maximum(jnp.max(s, axis=0, keepdims=True), sink)
            p = jnp.exp2(s - m)
            den = jnp.sum(p, axis=0, keepdims=True) + jnp.exp2(sink - m)
            probs.append((p * (1.0 / den)).astype(BF16))
        for kv in range(N_KV):
            o = lax.dot_general(probs[kv], vdup[kv][win], (((0,), (0,)), ((), ())),
                                preferred_element_type=F32)
            for h in range(Q_PER_KV // 2):
                both = jnp.where(pair_lo, o[2 * h * qp:(2 * h + 1) * qp], o[(2 * h + 1) * qp:(2 * h + 2) * qp])
                o_ref[rows, kv * gw + h * 2 * half:kv * gw + (h + 1) * 2 * half] = both.astype(o_ref.dtype)


def _attention(sinks, q, k_prev, k_cur, v_prev, v_cur, qt, qp, masked, bb=1, ride=()):
    b, l, _ = q.shape
    nt = l // qt
    r_in, r_out, r_shape = _cast_riders(ride, b // bb * nt, lambda bi, i: bi * nt + i)
    per = qt // WINDOW
    kw = WINDOW + qp
    kc = np.arange(kw)[:, None] // CHUNK
    qc = (np.arange(Q_PER_KV * qp)[None, :] % qp) // CHUNK
    band = (kc >= qc) & (kc <= qc + WINDOW // CHUNK)
    bias = np.stack([np.where(band & (kc >= WINDOW // CHUNK), 0.0, NEG), np.where(band, 0.0, NEG)])
    bias = jnp.asarray(bias, F32)
    if masked:
        prev_map = lambda bi, i: (bi, jnp.maximum(i * per - 1, 0), 0)
    else:
        prev_map = lambda bi, i: (bi, 0, 0)
    cur = lambda w: pl.BlockSpec((bb, qt, w), lambda bi, i: (bi, i, 0))
    prev = pl.BlockSpec((bb, WINDOW, D_KV), prev_map)
    return pl.pallas_call(
        functools.partial(_attn_kernel, qp=qp, kw=kw, masked=masked),
        grid=(b // bb, nt),
        in_specs=[pl.BlockSpec(memory_space=pltpu.SMEM), pl.BlockSpec(bias.shape, lambda bi, i: (0, 0, 0)),
                  cur(D_ATTN), prev, cur(D_KV), prev, cur(D_KV)] + r_in,
        out_specs=[cur(D_ATTN)] + r_out,
        out_shape=[jax.ShapeDtypeStruct((b, l, D_ATTN), BF16)] + r_shape,
        compiler_params=_params("arbitrary", "arbitrary"),
        name="attention",
    )(sinks, bias, q, k_prev, k_cur, v_prev, v_cur, *[r[0] if isinstance(r, tuple) else r for r in ride])


def _mix_kernel(x_ref, y_ref, o_ref, n1_ref, wglu_ref, wga_ref, wgb_ref, wbs_ref, wba_ref, wout_ref,
                h_ref, xn_s, ys_s, o_s):
    j = pl.program_id(1)

    @pl.when(j == 0)
    def _():
        o_s[...] = o_ref[...]
        hr = x_ref.shape[0] // 2
        for r in range(2):
            rows = slice(r * hr, (r + 1) * hr)
            x = x_ref[rows, :]
            xn_s[rows, :] = _rms(x, n1_ref[...]).astype(BF16)
            c0 = r * (hr // CHUNK)
            y = jnp.concatenate([y_ref[c * PITCH:c * PITCH + CHUNK, :] for c in range(c0, c0 + hr // CHUNK)], axis=0)
            ya = jax.nn.gelu(y)
            ys_s[rows, :] = (ya * jax.nn.sigmoid(_dot(ya.astype(BF16), wglu_ref[...]))).astype(BF16)
            h_ref[rows, :] = x

    xn = xn_s[...]
    ga = jax.nn.sigmoid(_dot(xn, wga_ref[...]))
    gb = jax.nn.sigmoid(_dot(xn, wgb_ref[...]))
    mixed = ga * _dot(ys_s[...], wbs_ref[...]) + gb * _dot(o_s[...], wba_ref[...])
    h_ref[...] += _dot(mixed.astype(BF16), wout_ref[...])


def _mix(x2d, y2d, o2d, norm1, wglu, wgate, wbs, wba, wout, tm):
    t = x2d.shape[0]
    tn = wbs.shape[2]
    nj = D_MODEL // tn
    nt = t // tm
    tok = lambda w: pl.BlockSpec((tm, w), lambda i, j: (i, 0))
    early = lambda rows, w, at: pl.BlockSpec(
        (rows, w), lambda i, j: (jnp.minimum(i + jnp.where(j >= at, 1, 0), nt - 1), 0))
    colblk = lambda rows, off: pl.BlockSpec((None, rows, tn), lambda i, j: (off + j, 0, 0))
    return pl.pallas_call(
        _mix_kernel,
        grid=(nt, nj),
        in_specs=[early(tm, D_MODEL, 1), early(tm // CHUNK * PITCH, D_SSM, 2), early(tm, D_ATTN, 3),
                  pl.BlockSpec((1, D_MODEL), lambda i, j: (0, 0)),
                  pl.BlockSpec((D_SSM, D_SSM), lambda i, j: (0, 0)),
                  colblk(D_MODEL, 0), colblk(D_MODEL, nj), colblk(D_SSM, 0), colblk(D_ATTN, 0),
                  pl.BlockSpec((tn, D_MODEL), lambda i, j: (j, 0))],
        out_specs=tok(D_MODEL),
        out_shape=jax.ShapeDtypeStruct((t, D_MODEL), F32),
        scratch_shapes=[pltpu.VMEM((tm, D_MODEL), BF16), pltpu.VMEM((tm, D_SSM), BF16),
                        pltpu.VMEM((tm, D_ATTN), BF16)],
        compiler_params=_params("arbitrary", "arbitrary"),
        name="mix",
    )(x2d, y2d, o2d, norm1, wglu, wgate, wgate, wbs, wba, wout)


def _ffn_kernel(h_ref, n2_ref, wg_ref, wu_ref, wd_ref, out_ref, hn_s):
    j = pl.program_id(1)

    @pl.when(j == 0)
    def _():
        h = h_ref[...]
        hn_s[...] = _rms(h, n2_ref[...]).astype(BF16)
        out_ref[...] = h

    hn = hn_s[...]
    act = jax.nn.silu(_dot(hn, wg_ref[...])) * _dot(hn, wu_ref[...])
    out_ref[...] += _dot(act.astype(BF16), wd_ref[...])


def _ffn(h2d, norm2, wg, wu, wd, tm):
    t = h2d.shape[0]
    nf, _, tf = wg.shape
    d_ff = nf * tf
    nt = t // tm
    tok = pl.BlockSpec((tm, D_MODEL), lambda i, j: (i, 0))
    h_spec = pl.BlockSpec((tm, D_MODEL), lambda i, j: (jnp.minimum(i + jnp.where(j >= nf // 2, 1, 0), nt - 1), 0))
    return pl.pallas_call(
        _ffn_kernel,
        grid=(nt, nf),
        in_specs=[h_spec, pl.BlockSpec((1, D_MODEL), lambda i, j: (0, 0)),
                  pl.BlockSpec((None, D_MODEL, tf), lambda i, j: (j, 0, 0)),
                  pl.BlockSpec((None, D_MODEL, tf), lambda i, j: (j, 0, 0)),
                  pl.BlockSpec((tf, D_MODEL), lambda i, j: (j, 0))],
        out_specs=tok,
        out_shape=jax.ShapeDtypeStruct((t, D_MODEL), F32),
        scratch_shapes=[pltpu.VMEM((tm, D_MODEL), BF16)],
        compiler_params=_params("arbitrary", "arbitrary"),
        name="ffn",
    )(h2d, norm2, wg, wu, wd)


def kernel(x_prompt, x_sample, cache_k, cache_v, state_ssm_re, state_ssm_im, norm1, w_in, q_norm, k_norm,
           sinks, ssm_a_re, ssm_a_im, ssm_log_dt, ssm_b_re, ssm_b_im, ssm_c_re, ssm_c_im, ssm_d, w_glu,
           w_br_ssm, w_br_attn, w_gate, w_out, norm2, w_ffn_gate, w_ffn_up, w_ffn_down):
    bp, lp, _ = x_prompt.shape
    bs, ls, _ = x_sample.shape
    ncp = lp // CHUNK
    assert ls == CHUNK and cache_k.shape[2] == WINDOW and norm1.shape[0] == 1
    l = 0

    f32 = lambda a: a[l].astype(F32)
    n1 = norm1[l].astype(F32).reshape(1, D_MODEL)
    n2 = norm2[l].astype(F32).reshape(1, D_MODEL)
    qg = jnp.tile(q_norm[l].astype(F32), LANES // HEAD_DIM).reshape(1, LANES)
    kg = jnp.tile(k_norm[l].astype(F32), LANES // HEAD_DIM).reshape(1, LANES)
    sink = sinks[l].astype(F32) * LOG2E

    blk, wst, wout_t, a2, w_in_b = _ssm_prep(ssm_a_re[l], ssm_a_im[l], ssm_log_dt[l], ssm_b_re[l], ssm_b_im[l],
                                             ssm_c_re[l], ssm_c_im[l], ssm_d[l], f32(w_in))

    xp = x_prompt.astype(F32).reshape(bp * lp, D_MODEL)
    xs = x_sample.astype(F32).reshape(bs * ls, D_MODEL)
    tm = TM_IN
    up, qp, kp, vp, wfg_b, wfu_b, wfd_b = _in_proj(
        xp, np.arange(lp), tm, n1, w_in_b, qg, kg,
        ride=((f32(w_ffn_gate), TF_FFN), (f32(w_ffn_up), TF_FFN), f32(w_ffn_down)))
    us, qs, ks, vs = _in_proj(xs, np.tile(PAST_LEN + np.arange(ls), tm // ls), tm, n1, w_in_b, qg, kg)

    h0s = jnp.concatenate([state_ssm_re[l], state_ssm_im[l]], axis=-1).astype(F32)
    h0s = jnp.swapaxes(h0s, 0, 1).reshape(N_GROUPS * bs, 2 * STATE)
    y_p, y_s, hfp, hfs = _ssm(up, us, blk, wst, wout_t, a2, h0s, bp, ncp, bs)

    def states(hf, b):
        hf = jnp.swapaxes(hf.reshape(N_GROUPS, b, 2 * STATE), 0, 1)
        return hf[None, :, :, :STATE], hf[None, :, :, STATE:]

    kp3, vp3 = kp.reshape(bp, lp, D_KV), vp.reshape(bp, lp, D_KV)
    o_p, wgate_b, wout_b, wbs_b, wba_b, wglu_b = _attention(
        sink, qp.reshape(bp, lp, D_ATTN), kp3, kp3, vp3, vp3, 2048, 2 * CHUNK, True,
        ride=((f32(w_gate), TN_MIX), f32(w_out), (f32(w_br_ssm), TN_MIX), (f32(w_br_attn), TN_MIX), f32(w_glu)))
    ck = cache_k[l].astype(F32).reshape(bs, WINDOW, D_KV)
    cv = cache_v[l].astype(F32).reshape(bs, WINDOW, D_KV)
    ks3, vs3 = ks.reshape(bs, ls, D_KV), vs.reshape(bs, ls, D_KV)
    o_s, = _attention(sink, qs.reshape(bs, ls, D_ATTN), ck, ks3, cv, vs3, ls, ls, False, bb=8)

    outs = []
    for x2d, y2d, o3d in ((xp, y_p, o_p), (xs, y_s, o_s)):
        h = _mix(x2d, y2d, o3d.reshape(-1, D_ATTN), n1, wglu_b, wgate_b, wbs_b, wba_b, wout_b, TM_MIX)
        outs.append(_ffn(h, n2, wfg_b, wfu_b, wfd_b, TM_FFN))

    win = lambda a, b, n: a.reshape(b, n, N_KV, HEAD_DIM)[None]
    k_win_p = win(kp3[:, lp - WINDOW:], bp, WINDOW)
    v_win_p = win(vp3[:, lp - WINDOW:], bp, WINDOW)
    k_win_s = win(jnp.concatenate([ck, ks3], axis=1)[:, ls:], bs, WINDOW)
    v_win_s = win(jnp.concatenate([cv, vs3], axis=1)[:, ls:], bs, WINDOW)
    re_p, im_p = states(hfp, bp)
    re_s, im_s = states(hfs, bs)
    dt = x_prompt.dtype
    return (outs[0].reshape(bp, lp, D_MODEL).astype(dt), outs[1].reshape(bs, ls, D_MODEL).astype(dt),
            k_win_p.astype(dt), v_win_p.astype(dt), re_p.astype(dt), im_p.astype(dt),
            k_win_s.astype(dt), v_win_s.astype(dt), re_s.astype(dt), im_s.astype(dt))
```
